```python
import jax
import jax.numpy as jnp
from jax import lax
import numpy as np

D_MODEL = 1024
BATCH = 4
SEQ = 8192
DEPTH = 4

GRID_W = 64
CTX_LEN = 256
N_MIXERS = 3
N_MLA_LAYERS = (DEPTH + 2) // 3
N_NA_LAYERS = (DEPTH + 1) // 3
N_SWA_LAYERS = DEPTH // 3

D_FF = -(-8 * D_MODEL // (3 * 256)) * 256
NORM_EPS = 1e-6
ROPE_THETA = 10000.0
NEG_INF = -1e30
Q_BLOCK = 128

MLA_HEADS = D_MODEL // 128
MLA_Q_LORA = D_MODEL // 4
MLA_KV_LORA = D_MODEL // 4
MLA_NOPE = 128
MLA_ROPE = 64
MLA_V = 128

NA_HEADS = D_MODEL // 64
NA_HEAD_DIM = 64
NA_WIN_ROWS = 8
NA_WIN_COLS = 16

SWA_HEADS = D_MODEL // 64
SWA_KV_HEADS = SWA_HEADS // 8
SWA_HEAD_DIM = 64
SWA_WINDOW = 128
SWA_BLOCK = 128

kernel_name = "hybrid_mla_na_swa_dit_trunk"


def rms_norm(x, g):
    xf = x.astype(jnp.float32)
    y = xf * lax.rsqrt(jnp.mean(xf * xf, axis=-1, keepdims=True) + NORM_EPS)
    return (y * g.astype(jnp.float32)).astype(x.dtype)


def modulate(h, shift, scale):
    return h * (1 + scale) + shift


def axial_rope_tables(n_tokens, rot_dim):
    t = jnp.arange(n_tokens)
    row = (t // GRID_W).astype(jnp.float32)
    col = (t % GRID_W).astype(jnp.float32)
    n_freq = rot_dim // 4
    inv = ROPE_THETA ** (-jnp.arange(n_freq, dtype=jnp.float32) / n_freq)
    ang = jnp.concatenate([row[:, None] * inv, col[:, None] * inv], axis=-1)
    return jnp.cos(ang), jnp.sin(ang)


def apply_rope(x, cos, sin):
    xf = x.astype(jnp.float32)
    half = xf.shape[-1] // 2
    x1, x2 = xf[..., :half], xf[..., half:]
    cs, sn = cos[None, :, None, :], sin[None, :, None, :]
    return jnp.concatenate([x1 * cs - x2 * sn, x2 * cs + x1 * sn], axis=-1).astype(x.dtype)


def softmax_attend(q, k, v, scale):
    s = jnp.einsum('bqhd,bkhd->bhqk', q, k).astype(jnp.float32) * scale
    p = jax.nn.softmax(s, axis=-1).astype(v.dtype)
    return jnp.einsum('bhqk,bkhd->bqhd', p, v)


def dense_block_attention(q, k, v, scale):
    b, n, h, dk = q.shape
    nb = n // Q_BLOCK
    qb = jnp.moveaxis(q.reshape(b, nb, Q_BLOCK, h, dk), 1, 0)
    out = lax.map(lambda q_blk: softmax_attend(q_blk, k, v, scale), qb)
    return jnp.moveaxis(out, 0, 1).reshape(b, n, h, v.shape[-1])


def swiglu(h, w_in, w_out):
    gate, up = jnp.split(h @ w_in, 2, axis=-1)
    return (jax.nn.silu(gate) * up) @ w_out


def mla_project(h, w_in, g_q, w_uq, g_kv, w_ukv):
    b, n, _ = h.shape
    a = h @ w_in
    q_lat = a[..., :MLA_Q_LORA]
    kv_lat = a[..., MLA_Q_LORA:MLA_Q_LORA + MLA_KV_LORA]
    k_pe = a[..., MLA_Q_LORA + MLA_KV_LORA:][:, :, None, :]
    q = (rms_norm(q_lat, g_q) @ w_uq).reshape(b, n, MLA_HEADS, MLA_NOPE + MLA_ROPE)
    kv = (rms_norm(kv_lat, g_kv) @ w_ukv).reshape(b, n, MLA_HEADS, MLA_NOPE + MLA_V)
    return q[..., :MLA_NOPE], q[..., MLA_NOPE:], kv[..., :MLA_NOPE], k_pe, kv[..., MLA_NOPE:]


def mla_assemble(q_nope, q_pe, k_nope, k_pe):
    q = jnp.concatenate([q_nope, q_pe], axis=-1)
    k = jnp.concatenate([k_nope, jnp.broadcast_to(k_pe, k_nope.shape[:-1] + (MLA_ROPE,))], axis=-1)
    return q, k


def mla_mixer(hx, hc, w_in, g_q, w_uq, g_kv, w_ukv, w_o, ctx_out):
    b, n, _ = hx.shape
    scale = (MLA_NOPE + MLA_ROPE) ** -0.5
    q_nope, q_pe, k_nope, k_pe, v = mla_project(hx, w_in, g_q, w_uq, g_kv, w_ukv)
    cos, sin = axial_rope_tables(n, MLA_ROPE)
    q, k = mla_assemble(q_nope, apply_rope(q_pe, cos, sin), k_nope, apply_rope(k_pe, cos, sin))
    cq_nope, cq_pe, ck_nope, ck_pe, vc = mla_project(hc, w_in, g_q, w_uq, g_kv, w_ukv)
    qc, kc = mla_assemble(cq_nope, cq_pe, ck_nope, ck_pe)
    k_all = jnp.concatenate([kc, k], axis=1)
    v_all = jnp.concatenate([vc, v], axis=1)
    ox = dense_block_attention(q, k_all, v_all, scale).reshape(b, n, MLA_HEADS * MLA_V) @ w_o
    oc = None
    if ctx_out:
        oc = softmax_attend(qc, kc, vc, scale).reshape(b, hc.shape[1], MLA_HEADS * MLA_V) @ w_o
    return ox, oc


def na_mixer(hx, hc, w_qkv, rpb, w_o, ctx_out):
    b, n, _ = hx.shape
    rows = n // GRID_W
    wr = min(NA_WIN_ROWS, rows)
    wc = NA_WIN_COLS
    hd = NA_HEADS * NA_HEAD_DIM
    scale = NA_HEAD_DIM ** -0.5
    qkv = (hx @ w_qkv).reshape(b, rows, GRID_W, 3, NA_HEADS, NA_HEAD_DIM)
    q, k, v = qkv[..., 0, :, :], qkv[..., 1, :, :], qkv[..., 2, :, :]
    lc = hc.shape[1]
    qkv_c = (hc @ w_qkv).reshape(b, lc, 3, NA_HEADS, NA_HEAD_DIM)
    qc, kc, vc = qkv_c[:, :, 0], qkv_c[:, :, 1], qkv_c[:, :, 2]
    cols = jnp.arange(GRID_W)
    col_start = jnp.clip(cols - wc // 2, 0, GRID_W - wc)
    col_ok = (cols[None, :] >= col_start[:, None]) & (cols[None, :] < col_start[:, None] + wc)
    col_idx = jnp.clip(cols[None, :] - cols[:, None], 1 - wc, wc - 1) + (NA_WIN_COLS - 1)

    def one_row(r):
        rs = jnp.clip(r - wr // 2, 0, rows - wr)
        q_r = lax.dynamic_index_in_dim(q, r, axis=1, keepdims=False)
        k_r = lax.dynamic_slice_in_dim(k, rs, wr, axis=1)
        v_r = lax.dynamic_slice_in_dim(v, rs, wr, axis=1)
        s = jnp.einsum('bqhd,bikhd->bhqik', q_r, k_r).astype(jnp.float32) * scale
        row_idx = rs + jnp.arange(wr) - r + (NA_WIN_ROWS - 1)
        bias = rpb[:, row_idx[None, :, None], col_idx[:, None, :]]
        s = jnp.where(col_ok[:, None, :], s + bias.astype(jnp.float32), NEG_INF)
        s_c = jnp.einsum('bqhd,bkhd->bhqk', q_r, kc).astype(jnp.float32) * scale
        logits = jnp.concatenate([s.reshape(b, NA_HEADS, GRID_W, wr * GRID_W), s_c], axis=-1)
        p = jax.nn.softmax(logits, axis=-1).astype(v.dtype)
        p_n = p[..., :wr * GRID_W].reshape(b, NA_HEADS, GRID_W, wr, GRID_W)
        p_c = p[..., wr * GRID_W:]
        return jnp.einsum('bhqik,bikhd->bqhd', p_n, v_r) + jnp.einsum('bhqk,bkhd->bqhd', p_c, vc)

    out = lax.map(one_row, jnp.arange(rows))
    ox = jnp.moveaxis(out, 0, 1).reshape(b, n, hd) @ w_o
    oc = None
    if ctx_out:
        oc = softmax_attend(qc, kc, vc, scale).reshape(b, lc, hd) @ w_o
    return ox, oc


def swa_mixer(hx, hc, w_qkv, sink, w_o, ctx_out):
    b, n, _ = hx.shape
    hq, hk, dh = SWA_HEADS, SWA_KV_HEADS, SWA_HEAD_DIM
    g = hq // hk
    scale = dh ** -0.5

    def split_qkv(h):
        a = h @ w_qkv
        lead = a.shape[:2]
        q_ = a[..., :hq * dh].reshape(lead + (hq, dh))
        k_ = a[..., hq * dh:(hq + hk) * dh].reshape(lead + (hk, dh))
        v_ = a[..., (hq + hk) * dh:].reshape(lead + (hk, dh))
        return q_, k_, v_

    q, k, v = split_qkv(hx)
    qc, kc, vc = split_qkv(hc)
    lc = hc.shape[1]
    cos, sin = axial_rope_tables(n, dh)
    q = apply_rope(q, cos, sin)
    k = apply_rope(k, cos, sin)
    sink_l = sink.astype(jnp.float32).reshape(hk, g)
    pad = ((0, 0), (SWA_WINDOW, SWA_WINDOW), (0, 0), (0, 0))
    kp = jnp.pad(k, pad)
    vp = jnp.pad(v, pad)
    nb = n // SWA_BLOCK
    span = SWA_BLOCK + 2 * SWA_WINDOW
    qb = jnp.moveaxis(q.reshape(b, nb, SWA_BLOCK, hk, g, dh), 1, 0)
    rel = jnp.arange(span)[None, :] - SWA_WINDOW - jnp.arange(SWA_BLOCK)[:, None]

    def one_block(args):
        blk, q_blk = args
        start = blk * SWA_BLOCK
        k_blk = lax.dynamic_slice_in_dim(kp, start, span, axis=1)
        v_blk = lax.dynamic_slice_in_dim(vp, start, span, axis=1)
        kpos = start - SWA_WINDOW + jnp.arange(span)
        ok = (jnp.abs(rel) <= SWA_WINDOW) & ((kpos >= 0) & (kpos < n))[None, :]
        s = jnp.einsum('bqkgd,bjkd->bkgqj', q_blk, k_blk).astype(jnp.float32) * scale
        s = jnp.where(ok, s, NEG_INF)
        s_c = jnp.einsum('bqkgd,bjkd->bkgqj', q_blk, kc).astype(jnp.float32) * scale
        sink_col = jnp.broadcast_to(sink_l[None, :, :, None, None], s.shape[:-1] + (1,))
        p = jax.nn.softmax(jnp.concatenate([s, s_c, sink_col], axis=-1), axis=-1).astype(v.dtype)
        return (jnp.einsum('bkgqj,bjkd->bqkgd', p[..., :span], v_blk)
                + jnp.einsum('bkgqj,bjkd->bqkgd', p[..., span:span + lc], vc))

    out = lax.map(one_block, (jnp.arange(nb), qb))
    ox = jnp.moveaxis(out, 0, 1).reshape(b, n, hq * dh) @ w_o
    oc = None
    if ctx_out:
        qcg = qc.reshape(b, lc, hk, g, dh)
        s = jnp.einsum('bqkgd,bjkd->bkgqj', qcg, kc).astype(jnp.float32) * scale
        sink_col = jnp.broadcast_to(sink_l[None, :, :, None, None], s.shape[:-1] + (1,))
        p = jax.nn.softmax(jnp.concatenate([s, sink_col], axis=-1), axis=-1)[..., :lc].astype(vc.dtype)
        oc = jnp.einsum('bkgqj,bjkd->bqkgd', p, vc).reshape(b, lc, hq * dh) @ w_o
    return ox, oc


def setup_inputs(seed: int = 0) -> dict:
    key = jax.random.key(seed)
    keys = iter(jax.random.split(key, 32))
    D = D_MODEL

    def normal(shape, std=1.0):
        return jax.random.normal(next(keys), shape, jnp.float32) * std

    def weight(shape, fan_in, gain=1.0):
        return normal(shape, gain * fan_in ** -0.5)

    def norm_gain(shape):
        return 1.0 + normal(shape, 0.05)

    return {
        "x": normal((BATCH, SEQ, D)),
        "c": normal((BATCH, D)),
        "ctx": normal((BATCH, CTX_LEN, D)),
        "c_ctx": normal((D,)),
        "w_mod": weight((DEPTH, D, 6 * D), D, 0.5),
        "b_mod": normal((DEPTH, 6 * D), 0.02),
        "g_pre_mix": norm_gain((DEPTH, D)),
        "g_post_mix": norm_gain((DEPTH, D)),
        "g_pre_ffn": norm_gain((DEPTH, D)),
        "g_post_ffn": norm_gain((DEPTH, D)),
        "w_ffn_in": weight((DEPTH, D, 2 * D_FF), D),
        "w_ffn_out": weight((DEPTH, D_FF, D), D_FF),
        "mla_w_in": weight((N_MLA_LAYERS, D, MLA_Q_LORA + MLA_KV_LORA + MLA_ROPE), D),
        "mla_g_q": norm_gain((N_MLA_LAYERS, MLA_Q_LORA)),
        "mla_w_uq": weight((N_MLA_LAYERS, MLA_Q_LORA, MLA_HEADS * (MLA_NOPE + MLA_ROPE)), MLA_Q_LORA),
        "mla_g_kv": norm_gain((N_MLA_LAYERS, MLA_KV_LORA)),
        "mla_w_ukv": weight((N_MLA_LAYERS, MLA_KV_LORA, MLA_HEADS * (MLA_NOPE + MLA_V)), MLA_KV_LORA),
        "mla_w_o": weight((N_MLA_LAYERS, MLA_HEADS * MLA_V, D), MLA_HEADS * MLA_V),
        "na_w_qkv": weight((N_NA_LAYERS, D, 3 * NA_HEADS * NA_HEAD_DIM), D),
        "na_rpb": normal((N_NA_LAYERS, NA_HEADS, 2 * NA_WIN_ROWS - 1, 2 * NA_WIN_COLS - 1), 0.1),
        "na_w_o": weight((N_NA_LAYERS, NA_HEADS * NA_HEAD_DIM, D), NA_HEADS * NA_HEAD_DIM),
        "swa_w_qkv": weight((N_SWA_LAYERS, D, (SWA_HEADS + 2 * SWA_KV_HEADS) * SWA_HEAD_DIM), D),
        "swa_sink": normal((N_SWA_LAYERS, SWA_HEADS), 0.5),
        "swa_w_o": weight((N_SWA_LAYERS, SWA_HEADS * SWA_HEAD_DIM, D), SWA_HEADS * SWA_HEAD_DIM),
    }


def reference(x, c, ctx, c_ctx, w_mod, b_mod, g_pre_mix, g_post_mix, g_pre_ffn, g_post_ffn,
              w_ffn_in, w_ffn_out, mla_w_in, mla_g_q, mla_w_uq, mla_g_kv, mla_w_ukv, mla_w_o,
              na_w_qkv, na_rpb, na_w_o, swa_w_qkv, swa_sink, swa_w_o):
    for i in range(DEPTH):
        last = i == DEPTH - 1
        mod_x = jax.nn.silu(c) @ w_mod[i] + b_mod[i]
        mod_c = jax.nn.silu(c_ctx) @ w_mod[i] + b_mod[i]
        sh_m, sc_m, gt_m, sh_f, sc_f, gt_f = jnp.split(mod_x[:, None, :], 6, axis=-1)
        csh_m, csc_m, cgt_m, csh_f, csc_f, cgt_f = jnp.split(mod_c, 6, axis=-1)

        hx = modulate(rms_norm(x, g_pre_mix[i]), sh_m, sc_m)
        hc = modulate(rms_norm(ctx, g_pre_mix[i]), csh_m, csc_m)
        kind, j = i % N_MIXERS, i // N_MIXERS
        if kind == 0:
            ox, oc = mla_mixer(hx, hc, mla_w_in[j], mla_g_q[j], mla_w_uq[j], mla_g_kv[j],
                               mla_w_ukv[j], mla_w_o[j], not last)
        elif kind == 1:
            ox, oc = na_mixer(hx, hc, na_w_qkv[j], na_rpb[j], na_w_o[j], not last)
        else:
            ox, oc = swa_mixer(hx, hc, swa_w_qkv[j], swa_sink[j], swa_w_o[j], not last)

        x = x + gt_m * rms_norm(ox, g_post_mix[i])
        hx = modulate(rms_norm(x, g_pre_ffn[i]), sh_f, sc_f)
        x = x + gt_f * rms_norm(swiglu(hx, w_ffn_in[i], w_ffn_out[i]), g_post_ffn[i])

        if not last:
            ctx = ctx + cgt_m * rms_norm(oc, g_post_mix[i])
            hc = modulate(rms_norm(ctx, g_pre_ffn[i]), csh_f, csc_f)
            ctx = ctx + cgt_f * rms_norm(swiglu(hc, w_ffn_in[i], w_ffn_out[i]), g_post_ffn[i])
    return x
```

```python
import functools

import jax
import jax.numpy as jnp
from jax import lax
from jax.experimental import pallas as pl
from jax.experimental.pallas import tpu as pltpu

D_MODEL = 1024
BATCH = 4
SEQ = 8192
DEPTH = 4
GRID_W = 64
CTX_LEN = 256
D_FF = 2816
NORM_EPS = 1e-6
ROPE_THETA = 10000.0
NEG_INF = -1e30

MLA_HEADS = 8
MLA_Q_LORA = 256
MLA_KV_LORA = 256
MLA_NOPE = 128
MLA_ROPE = 64
MLA_V = 128
MLA_QK_PAD = 256

NA_HEADS = 16
NA_HEAD_DIM = 64
NA_WIN_ROWS = 8
NA_WIN_COLS = 16

SWA_HEADS = 16
SWA_KV_HEADS = 2
SWA_HEAD_DIM = 64
SWA_WINDOW = 128

TOKENS = SEQ + CTX_LEN
TILE = 256
N_TILES = TOKENS // TILE
N_LAT_TILES = SEQ // TILE
LANES = 128
HEAD_PAIRS = 8
N_TABLE_VARIANTS = 4

MLA_TQ = 512
MLA_TK = 1024

VMEM_LIMIT = 56 * 1024 * 1024

BF16 = jnp.bfloat16
F32 = jnp.float32


def _cparams(n_axes):
    return pltpu.CompilerParams(
        dimension_semantics=("parallel",) * n_axes, vmem_limit_bytes=VMEM_LIMIT)


def _const_spec(shape):
    nd = len(shape)
    return pl.BlockSpec(shape, lambda *_: (0,) * nd, pipeline_mode=pl.Buffered(1))


def _rms(x, g):
    return x * lax.rsqrt(jnp.mean(x * x, axis=-1, keepdims=True) + NORM_EPS) * g


def _dot(a, b):
    return jnp.dot(a, b, preferred_element_type=F32)


def _dot_nt(a, b):
    return lax.dot_general(a, b, (((1,), (1,)), ((), ())), preferred_element_type=F32)


def _mod_kernel(c_ref, w_ref, b_ref, o_ref):
    c = c_ref[...]
    a = c * jax.nn.sigmoid(c)
    o_ref[...] = _dot(a, w_ref[...]) + b_ref[...]


def _modulation(cc, w_mod, b_mod):
    tn = 1536
    return pl.pallas_call(
        _mod_kernel,
        grid=(DEPTH, 6 * D_MODEL // tn),
        in_specs=[
            pl.BlockSpec((8, D_MODEL), lambda l, n: (0, 0)),
            pl.BlockSpec((None, D_MODEL, tn), lambda l, n: (l, 0, n)),
            pl.BlockSpec((None, 1, tn), lambda l, n: (l, 0, n)),
        ],
        out_specs=pl.BlockSpec((None, 8, tn), lambda l, n: (l, 0, n)),
        out_shape=jax.ShapeDtypeStruct((DEPTH, 8, 6 * D_MODEL), F32),
        compiler_params=_cparams(2),
        name="modulation",
    )(cc, w_mod, b_mod.reshape(DEPTH, 1, 6 * D_MODEL))


def _mod_spec():
    return pl.BlockSpec((None, None, 6, D_MODEL),
                        lambda b, j: (b, (j == N_LAT_TILES).astype(jnp.int32), 0, 0))


def _x_spec():
    return pl.BlockSpec((None, TILE, D_MODEL), lambda b, j: (b, j, 0))


def _rope_spec():
    return pl.BlockSpec((TILE, LANES), lambda b, j: (j, 0))


def _pre_mix(x_ref, mod_ref, g_ref):
    shift = mod_ref[0:1, :]
    scale = mod_ref[1:2, :]
    h = _rms(x_ref[...], g_ref[...]) * (1.0 + scale) + shift
    return h.astype(BF16)


def _mla_proj_kernel(x_ref, mod_ref, g_ref, cos_ref, sin_ref, w_in_ref, gq_ref, gkv_ref,
                     wq_ref, wkv_ref, q_ref, k_ref, v_ref, *, scale):
    h = _pre_mix(x_ref, mod_ref, g_ref)
    a = _dot(h, w_in_ref[...])
    qn = _rms(a[:, :MLA_Q_LORA], gq_ref[...]).astype(BF16)
    kvn = _rms(a[:, MLA_Q_LORA:MLA_Q_LORA + MLA_KV_LORA], gkv_ref[...]).astype(BF16)
    cos = cos_ref[...]
    sin = sin_ref[...]
    k_rope = (a[:, 512:640] * cos + a[:, 640:768] * sin).astype(BF16)
    qq = _dot(qn, wq_ref[...])
    kv = _dot(kvn, wkv_ref[...])
    hn = MLA_HEADS * LANES
    for hd in range(MLA_HEADS):
        lo, hi = hd * LANES, (hd + 1) * LANES
        base = hd * MLA_QK_PAD
        q_ref[:, base:base + LANES] = (qq[:, lo:hi] * scale).astype(BF16)
        q_rope = qq[:, hn + lo:hn + hi] * cos + qq[:, 2 * hn + lo:2 * hn + hi] * sin
        q_ref[:, base + LANES:base + 2 * LANES] = (q_rope * scale).astype(BF16)
        k_ref[:, base:base + LANES] = kv[:, lo:hi].astype(BF16)
        k_ref[:, base + LANES:base + 2 * LANES] = k_rope
    v_ref[...] = kv[:, hn:].astype(BF16)


def _mla_proj(x, mod, g_pre, cos, sin, w_in, g_q, g_kv, wq, wkv):
    hq = MLA_HEADS * MLA_QK_PAD
    hv = MLA_HEADS * MLA_V
    out_spec_q = pl.BlockSpec((None, TILE, hq), lambda b, j: (b, j, 0))
    out_spec_v = pl.BlockSpec((None, TILE, hv), lambda b, j: (b, j, 0))
    scale = float((MLA_NOPE + MLA_ROPE) ** -0.5)
    return pl.pallas_call(
        functools.partial(_mla_proj_kernel, scale=scale),
        grid=(BATCH, N_TILES),
        in_specs=[_x_spec(), _mod_spec(), _const_spec((1, D_MODEL)), _rope_spec(), _rope_spec(),
                  _const_spec(w_in.shape), _const_spec((1, MLA_Q_LORA)),
                  _const_spec((1, MLA_KV_LORA)), _const_spec(wq.shape), _const_spec(wkv.shape)],
        out_specs=[out_spec_q, out_spec_q, out_spec_v],
        out_shape=[jax.ShapeDtypeStruct((BATCH, TOKENS, hq), BF16),
                   jax.ShapeDtypeStruct((BATCH, TOKENS, hq), BF16),
                   jax.ShapeDtypeStruct((BATCH, TOKENS, hv), BF16)],
        compiler_params=_cparams(2),
        name="mla_proj",
    )(x, mod, g_pre, cos, sin, w_in, g_q, g_kv, wq, wkv)


def _flash_step(q, k, v, carry):
    m, l, acc = carry
    s = _dot_nt(q, k)
    m_new = jnp.maximum(m, jnp.max(s, axis=-1, keepdims=True))
    alpha = jnp.exp(m - m_new)
    p = jnp.exp(s - m_new)
    l = alpha * l + jnp.sum(p, axis=-1, keepdims=True)
    acc = alpha * acc + _dot(p.astype(BF16), v)
    return m_new, l, acc


def _mla_attn_kernel(q_ref, k_ref, v_ref, o_ref, *, n_lat_chunks, ctx_start):
    q = q_ref[...]
    tq = q.shape[0]
    carry = (jnp.full((tq, 1), NEG_INF, F32), jnp.zeros((tq, 1), F32),
             jnp.zeros((tq, MLA_V), F32))

    def body(c, carry):
        start = pl.multiple_of(c * MLA_TK, MLA_TK)
        return _flash_step(q, k_ref[pl.ds(start, MLA_TK), :], v_ref[pl.ds(start, MLA_TK), :],
                           carry)

    if n_lat_chunks:
        carry = lax.fori_loop(0, n_lat_chunks, body, carry)
    carry = _flash_step(q, k_ref[ctx_start:ctx_start + CTX_LEN, :],
                        v_ref[ctx_start:ctx_start + CTX_LEN, :], carry)
    _, l, acc = carry
    o_ref[...] = (acc / l).astype(BF16)


def _mla_attn_ctx_kernel(q_ref, k_ref, v_ref, latent_out_ref, o_ref):
    del latent_out_ref
    _mla_attn_kernel(q_ref, k_ref, v_ref, o_ref, n_lat_chunks=0, ctx_start=0)


def _mla_attention(q, k, v):
    hv = MLA_HEADS * MLA_V
    o = pl.pallas_call(
        functools.partial(_mla_attn_kernel, n_lat_chunks=SEQ // MLA_TK, ctx_start=SEQ),
        grid=(BATCH, MLA_HEADS, SEQ // MLA_TQ),
        in_specs=[
            pl.BlockSpec((None, MLA_TQ, MLA_QK_PAD), lambda b, h, i: (b, i, h)),
            pl.BlockSpec((None, TOKENS, MLA_QK_PAD), lambda b, h, i: (b, 0, h)),
            pl.BlockSpec((None, TOKENS, MLA_V), lambda b, h, i: (b, 0, h)),
        ],
        out_specs=pl.BlockSpec((None, MLA_TQ, MLA_V), lambda b, h, i: (b, i, h)),
        out_shape=jax.ShapeDtypeStruct((BATCH, TOKENS, hv), BF16),
        compiler_params=_cparams(3),
        name="mla_attn",
    )(q, k, v)
    ctx_blk = SEQ // CTX_LEN
    return pl.pallas_call(
        _mla_attn_ctx_kernel,
        grid=(BATCH, MLA_HEADS),
        in_specs=[
            pl.BlockSpec((None, CTX_LEN, MLA_QK_PAD), lambda b, h: (b, ctx_blk, h)),
            pl.BlockSpec((None, CTX_LEN, MLA_QK_PAD), lambda b, h: (b, ctx_blk, h)),
            pl.BlockSpec((None, CTX_LEN, MLA_V), lambda b, h: (b, ctx_blk, h)),
            pl.BlockSpec(memory_space=pl.ANY),
        ],
        out_specs=pl.BlockSpec((None, CTX_LEN, MLA_V), lambda b, h: (b, ctx_blk, h)),
        out_shape=jax.ShapeDtypeStruct((BATCH, TOKENS, hv), BF16),
        input_output_aliases={3: 0},
        compiler_params=_cparams(2),
        name="mla_attn_ctx",
    )(q, k, v, o)


def _na_proj_kernel(x_ref, mod_ref, g_ref, w_ref, o_ref, *, scale):
    h = _pre_mix(x_ref, mod_ref, g_ref)
    a = _dot(h, w_ref[...])
    hd = NA_HEADS * NA_HEAD_DIM
    o_ref[:, :hd] = (a[:, :hd] * scale).astype(BF16)
    o_ref[:, hd:] = a[:, hd:].astype(BF16)


def _na_proj(x, mod, g_pre, w):
    n = w.shape[1]
    return pl.pallas_call(
        functools.partial(_na_proj_kernel, scale=float(NA_HEAD_DIM ** -0.5)),
        grid=(BATCH, N_TILES),
        in_specs=[_x_spec(), _mod_spec(), _const_spec((1, D_MODEL)), _const_spec(w.shape)],
        out_specs=pl.BlockSpec((None, TILE, n), lambda b, j: (b, j, 0)),
        out_shape=jax.ShapeDtypeStruct((BATCH, TOKENS, n), BF16),
        compiler_params=_cparams(2),
        name="na_proj",
    )(x, mod, g_pre, w)


def _swa_proj_kernel(x_ref, mod_ref, g_ref, cos_ref, sin_ref, w_ref, q_ref, k_ref, v_ref, *,
                     scale):
    h = _pre_mix(x_ref, mod_ref, g_ref)
    a = _dot(h, w_ref[...])
    cos = cos_ref[...]
    sin = sin_ref[...]
    hd = SWA_HEADS * SWA_HEAD_DIM
    for p in range(HEAD_PAIRS):
        lo, hi = p * LANES, (p + 1) * LANES
        q = a[:, lo:hi] * cos + a[:, hd + lo:hd + hi] * sin
        q_ref[:, lo:hi] = (q * scale).astype(BF16)
    kw = 2 * LANES
    for p in range(2):
        lo, hi = p * LANES, (p + 1) * LANES
        k = a[:, 2 * hd + lo:2 * hd + hi] * cos + a[:, 2 * hd + kw + lo:2 * hd + kw + hi] * sin
        k_ref[:, lo:hi] = k.astype(BF16)
    v_ref[...] = a[:, 2 * hd + 2 * kw:].astype(BF16)


def _swa_proj(x, mod, g_pre, cos, sin, w):
    hd = SWA_HEADS * SWA_HEAD_DIM
    kw = 2 * LANES
    return pl.pallas_call(
        functools.partial(_swa_proj_kernel, scale=float(SWA_HEAD_DIM ** -0.5)),
        grid=(BATCH, N_TILES),
        in_specs=[_x_spec(), _mod_spec(), _const_spec((1, D_MODEL)), _rope_spec(), _rope_spec(),
                  _const_spec(w.shape)],
        out_specs=[pl.BlockSpec((None, TILE, hd), lambda b, j: (b, j, 0)),
                   pl.BlockSpec((None, TILE, kw), lambda b, j: (b, j, 0)),
                   pl.BlockSpec((None, TILE, kw), lambda b, j: (b, j, 0))],
        out_shape=[jax.ShapeDtypeStruct((BATCH, TOKENS, hd), BF16),
                   jax.ShapeDtypeStruct((BATCH, TOKENS, kw), BF16),
                   jax.ShapeDtypeStruct((BATCH, TOKENS, kw), BF16)],
        compiler_params=_cparams(2),
        name="swa_proj",
    )(x, mod, g_pre, cos, sin, w)


def _local_attn_kernel(q_ref, k0_ref, k1_ref, k2_ref, kc_ref, v0_ref, v1_ref, v2_ref, vc_ref,
                       tbl_ref, sink_ref, o_ref, *, kv_slot, per_head_table, has_sink):
    lane = lax.broadcasted_iota(jnp.int32, (TILE, LANES), 1)
    low = lane < NA_HEAD_DIM
    for p in range(HEAD_PAIRS):
        qs = slice(p * LANES, (p + 1) * LANES)
        ks = slice(kv_slot(p) * LANES, (kv_slot(p) + 1) * LANES)
        qp = q_ref[:, qs]
        kn = jnp.concatenate([k0_ref[:, ks], k1_ref[:, ks], k2_ref[:, ks]], axis=0)
        vn = jnp.concatenate([v0_ref[:, ks], v1_ref[:, ks], v2_ref[:, ks]], axis=0)
        kc = kc_ref[:, ks]
        vc = vc_ref[:, ks]
        outs = []
        for half in range(2):
            hd = 2 * p + half
            qm = jnp.where(low if half == 0 else jnp.logical_not(low), qp, jnp.zeros_like(qp))
            s_n = _dot_nt(qm, kn) + tbl_ref[hd if per_head_table else 0]
            s_c = _dot_nt(qm, kc)
            m = jnp.maximum(jnp.max(s_n, axis=-1, keepdims=True),
                            jnp.max(s_c, axis=-1, keepdims=True))
            if has_sink:
                sink = sink_ref[hd:hd + 1, 0:1]
                m = jnp.maximum(m, sink)
            p_n = jnp.exp(s_n - m)
            p_c = jnp.exp(s_c - m)
            l = jnp.sum(p_n, axis=-1, keepdims=True) + jnp.sum(p_c, axis=-1, keepdims=True)
            if has_sink:
                l = l + jnp.exp(sink - m)
            o = _dot(p_n.astype(BF16), vn) + _dot(p_c.astype(BF16), vc)
            outs.append(o / l)
        o_ref[:, qs] = jnp.where(low, outs[0], outs[1]).astype(BF16)


def _local_attention(q, q_col, k, k_col, v, v_col, kv_width, table, sink, kv_slot, name):
    hd = NA_HEADS * NA_HEAD_DIM
    per_head_table = table.shape[1] > 1
    lat_hi = N_LAT_TILES - 1

    def nb_spec(col, off):
        return pl.BlockSpec(
            (None, TILE, kv_width),
            lambda b, j: (b, jnp.clip(j + off, 0, lat_hi), col))

    def ctx_spec(col):
        return pl.BlockSpec((None, TILE, kv_width), lambda b, j: (b, N_LAT_TILES, col))

    def variant(j):
        return jnp.where(j == 0, 0, jnp.where(j == lat_hi, 2, jnp.where(j > lat_hi, 3, 1)))

    tbl_spec = pl.BlockSpec((None,) + table.shape[1:], lambda b, j: (variant(j), 0, 0, 0))
    return pl.pallas_call(
        functools.partial(_local_attn_kernel, kv_slot=kv_slot, per_head_table=per_head_table,
                          has_sink=sink is not None),
        grid=(BATCH, N_TILES),
        in_specs=[pl.BlockSpec((None, TILE, hd), lambda b, j: (b, j, q_col)),
                  nb_spec(k_col, -1), nb_spec(k_col, 0), nb_spec(k_col, 1), ctx_spec(k_col),
                  nb_spec(v_col, -1), nb_spec(v_col, 0), nb_spec(v_col, 1), ctx_spec(v_col),
                  tbl_spec, _const_spec((NA_HEADS, LANES))],
        out_specs=pl.BlockSpec((None, TILE, hd), lambda b, j: (b, j, 0)),
        out_shape=jax.ShapeDtypeStruct((BATCH, TOKENS, hd), BF16),
        compiler_params=_cparams(2),
        name=name,
    )(q, k, k, k, k, v, v, v, v, table,
      jnp.zeros((NA_HEADS, LANES), F32) if sink is None else sink)


def _post_ffn_kernel(x_ref, o_ref, mod_ref, gpm_ref, gpf_ref, gof_ref, wo_ref, win_ref,
                     wout_ref, y_ref, act_ref):
    gate_m = mod_ref[2:3, :]
    shift_f = mod_ref[3:4, :]
    scale_f = mod_ref[4:5, :]
    gate_f = mod_ref[5:6, :]
    mixed = _dot(o_ref[...], wo_ref[...])
    x1 = x_ref[...] + gate_m * _rms(mixed, gpm_ref[...])
    h = (_rms(x1, gpf_ref[...]) * (1.0 + scale_f) + shift_f).astype(BF16)
    chunk = 256
    for f in range(D_FF // chunk):
        lo, hi = f * chunk, (f + 1) * chunk
        gate = _dot(h, win_ref[:, lo:hi])
        up = _dot(h, win_ref[:, D_FF + lo:D_FF + hi])
        act_ref[:, lo:hi] = (gate * jax.nn.sigmoid(gate) * up).astype(BF16)
    ffn = _dot(act_ref[...], wout_ref[...])
    y_ref[...] = x1 + gate_f * _rms(ffn, gof_ref[...])


def _post_ffn(x, o, mod, g_post_mix, g_pre_ffn, g_post_ffn, wo, win, wout, n_tiles):
    vec = _const_spec((1, D_MODEL))
    return pl.pallas_call(
        _post_ffn_kernel,
        grid=(BATCH, n_tiles),
        in_specs=[_x_spec(), _x_spec(), _mod_spec(), vec, vec, vec,
                  _const_spec(wo.shape), _const_spec(win.shape), _const_spec(wout.shape)],
        out_specs=_x_spec(),
        out_shape=jax.ShapeDtypeStruct((BATCH, n_tiles * TILE, D_MODEL), F32),
        scratch_shapes=[pltpu.VMEM((TILE, D_FF), BF16)],
        compiler_params=_cparams(2),
        name="post_ffn",
    )(x, o, mod, g_post_mix, g_pre_ffn, g_post_ffn, wo, win, wout)


def _rope_tables():
    t = jnp.arange(SEQ)
    row = (t // GRID_W).astype(F32)
    col = (t % GRID_W).astype(F32)
    n_freq = MLA_ROPE // 4
    inv = ROPE_THETA ** (-jnp.arange(n_freq, dtype=F32) / n_freq)
    ang = jnp.concatenate([row[:, None] * inv, col[:, None] * inv], axis=-1)
    cos, sin = jnp.cos(ang), jnp.sin(ang)
    cos_t = jnp.concatenate([cos, cos, cos, cos], axis=-1)
    sin_t = jnp.concatenate([-sin, sin, -sin, sin], axis=-1)
    cos_t = jnp.concatenate([cos_t, jnp.ones((CTX_LEN, LANES), F32)], axis=0)
    sin_t = jnp.concatenate([sin_t, jnp.zeros((CTX_LEN, LANES), F32)], axis=0)
    return cos_t, sin_t


def _variant_tiles():
    return jnp.array([0, 1, N_LAT_TILES - 1], jnp.int32)


def _na_table(rpb):
    rows = SEQ // GRID_W
    rows_per_tile = TILE // GRID_W
    j = _variant_tiles()[:, None, None]
    ql = jnp.arange(TILE)[None, :, None]
    kl = jnp.arange(3 * TILE)[None, None, :]
    qr = rows_per_tile * j + ql // GRID_W
    qc = ql % GRID_W
    kr = rows_per_tile * (j - 1) + kl // GRID_W
    kc = kl % GRID_W
    rs = jnp.clip(qr - NA_WIN_ROWS // 2, 0, rows - NA_WIN_ROWS)
    cs = jnp.clip(qc - NA_WIN_COLS // 2, 0, GRID_W - NA_WIN_COLS)
    ok = (kr >= rs) & (kr < rs + NA_WIN_ROWS) & (kc >= cs) & (kc < cs + NA_WIN_COLS)
    row_idx = jnp.clip(kr - qr + NA_WIN_ROWS - 1, 0, 2 * NA_WIN_ROWS - 2)
    col_idx = jnp.clip(kc - qc, 1 - NA_WIN_COLS, NA_WIN_COLS - 1) + NA_WIN_COLS - 1
    bias = rpb[:, row_idx, col_idx]
    tbl = jnp.where(ok[None], bias.astype(F32), NEG_INF)
    tbl = jnp.moveaxis(tbl, 0, 1)
    return jnp.concatenate([tbl, jnp.full((1,) + tbl.shape[1:], NEG_INF, F32)], axis=0)


def _swa_table():
    j = _variant_tiles()[:, None, None]
    qpos = TILE * j + jnp.arange(TILE)[None, :, None]
    kpos = TILE * (j - 1) + jnp.arange(3 * TILE)[None, None, :]
    ok = (jnp.abs(kpos - qpos) <= SWA_WINDOW) & (kpos >= 0) & (kpos < SEQ)
    tbl = jnp.where(ok, 0.0, NEG_INF).astype(F32)
    tbl = jnp.concatenate([tbl, jnp.full((1, TILE, 3 * TILE), NEG_INF, F32)], axis=0)
    return tbl[:, None]


def _swap_halves(w):
    half = w.shape[-1] // 2
    return jnp.concatenate([w[..., half:], w[..., :half]], axis=-1)


def _mla_weights(w_in, w_uq, w_ukv):
    lat = MLA_Q_LORA + MLA_KV_LORA
    kpe = w_in[:, lat:]
    z = jnp.zeros((D_MODEL, LANES - MLA_ROPE), w_in.dtype)
    w_in_x = jnp.concatenate([w_in[:, :lat], kpe, z, _swap_halves(kpe), z], axis=1)
    uq = w_uq.reshape(MLA_Q_LORA, MLA_HEADS, MLA_NOPE + MLA_ROPE)
    rope = uq[:, :, MLA_NOPE:]
    zq = jnp.zeros((MLA_Q_LORA, MLA_HEADS, LANES - MLA_ROPE), w_uq.dtype)
    flat = lambda a: a.reshape(a.shape[0], -1)
    wq_x = jnp.concatenate([flat(uq[:, :, :MLA_NOPE]),
                            flat(jnp.concatenate([rope, zq], axis=-1)),
                            flat(jnp.concatenate([_swap_halves(rope), zq], axis=-1))], axis=1)
    ukv = w_ukv.reshape(MLA_KV_LORA, MLA_HEADS, MLA_NOPE + MLA_V)
    wkv_x = jnp.concatenate([flat(ukv[:, :, :MLA_NOPE]), flat(ukv[:, :, MLA_NOPE:])], axis=1)
    return w_in_x.astype(BF16), wq_x.astype(BF16), wkv_x.astype(BF16)


def _swa_weights(w_qkv):
    hd = SWA_HEADS * SWA_HEAD_DIM
    kd = SWA_KV_HEADS * SWA_HEAD_DIM
    wq = w_qkv[:, :hd].reshape(D_MODEL, SWA_HEADS, SWA_HEAD_DIM)
    wk = w_qkv[:, hd:hd + kd].reshape(D_MODEL, SWA_KV_HEADS, 1, SWA_HEAD_DIM)
    wv = w_qkv[:, hd + kd:].reshape(D_MODEL, SWA_KV_HEADS, 1, SWA_HEAD_DIM)
    dup = lambda a: jnp.broadcast_to(a, (D_MODEL, SWA_KV_HEADS, 2, SWA_HEAD_DIM)).reshape(
        D_MODEL, -1)
    flat = lambda a: a.reshape(D_MODEL, -1)
    w = jnp.concatenate([flat(wq), flat(_swap_halves(wq)), dup(wk), dup(_swap_halves(wk)),
                         dup(wv)], axis=1)
    return w.astype(BF16)


def kernel(x, c, ctx, c_ctx, w_mod, b_mod, g_pre_mix, g_post_mix, g_pre_ffn, g_post_ffn,
           w_ffn_in, w_ffn_out, mla_w_in, mla_g_q, mla_w_uq, mla_g_kv, mla_w_ukv, mla_w_o,
           na_w_qkv, na_rpb, na_w_o, swa_w_qkv, swa_sink, swa_w_o):
    assert x.shape == (BATCH, SEQ, D_MODEL) and ctx.shape == (BATCH, CTX_LEN, D_MODEL)
    stream = jnp.concatenate([x, ctx], axis=1)
    cc = jnp.concatenate([c, c_ctx[None], jnp.zeros((8 - BATCH - 1, D_MODEL), F32)], axis=0)
    mods = _modulation(cc, w_mod, b_mod)
    mods = mods.reshape(DEPTH, 8, 6, D_MODEL)
    mods = jnp.stack([mods[:, :BATCH],
                      jnp.broadcast_to(mods[:, BATCH:BATCH + 1], (DEPTH, BATCH, 6, D_MODEL))],
                     axis=2)
    cos_t, sin_t = _rope_tables()
    row = lambda g: g.reshape(1, -1)

    for i in range(DEPTH):
        last = i == DEPTH - 1
        kind, j = i % 3, i // 3
        mod = mods[i]
        g_pre = row(g_pre_mix[i])
        if kind == 0:
            w_in_x, wq_x, wkv_x = _mla_weights(mla_w_in[j], mla_w_uq[j], mla_w_ukv[j])
            q, k, v = _mla_proj(stream, mod, g_pre, cos_t, sin_t, w_in_x, row(mla_g_q[j]),
                                row(mla_g_kv[j]), wq_x, wkv_x)
            o = _mla_attention(q, k, v)
            w_o = mla_w_o[j]
        elif kind == 1:
            qkv = _na_proj(stream, mod, g_pre, na_w_qkv[j].astype(BF16))
            o = _local_attention(qkv, 0, qkv, 1, qkv, 2, NA_HEADS * NA_HEAD_DIM,
                                 _na_table(na_rpb[j]), None, lambda p: p, "na_attn")
            w_o = na_w_o[j]
        else:
            q, k, v = _swa_proj(stream, mod, g_pre, cos_t, sin_t, _swa_weights(swa_w_qkv[j]))
            sink = jnp.broadcast_to(swa_sink[j].astype(F32)[:, None], (SWA_HEADS, LANES))
            o = _local_attention(q, 0, k, 0, v, 0, 2 * LANES, _swa_table(), sink,
                                 lambda p: p // (HEAD_PAIRS // SWA_KV_HEADS), "swa_attn")
            w_o = swa_w_o[j]
        stream = _post_ffn(stream, o, mod, row(g_post_mix[i]), row(g_pre_ffn[i]),
                           row(g_post_ffn[i]), w_o.astype(BF16), w_ffn_in[i].astype(BF16),
                           w_ffn_out[i].astype(BF16), N_LAT_TILES if last else N_TILES)
    return stream
```

```python
import functools

import jax
import jax.numpy as jnp
from jax import lax
from jax.experimental import pallas as pl
from jax.experimental.pallas import tpu as pltpu

D_MODEL = 1024
BATCH = 4
SEQ = 8192
DEPTH = 4
GRID_W = 64
CTX_LEN = 256
D_FF = 2816
NORM_EPS = 1e-6
ROPE_THETA = 10000.0
NEG_INF = -1e30

MLA_HEADS = 8
MLA_Q_LORA = 256
MLA_KV_LORA = 256
MLA_NOPE = 128
MLA_ROPE = 64
MLA_V = 128
MLA_QK_PAD = 256

NA_HEADS = 16
NA_HEAD_DIM = 64
NA_WIN_ROWS = 8
NA_WIN_COLS = 16

SWA_HEADS = 16
SWA_KV_HEADS = 2
SWA_HEAD_DIM = 64
SWA_WINDOW = 128

TOKENS = SEQ + CTX_LEN
TILE = 256
N_TILES = TOKENS // TILE
N_LAT_TILES = SEQ // TILE
LANES = 128
HEAD_PAIRS = 8
N_TABLE_VARIANTS = 4

MLA_TQ = 512
MLA_TK = 1024

VMEM_LIMIT = 56 * 1024 * 1024

LOG2E = 1.4426950408889634

BF16 = jnp.bfloat16
F32 = jnp.float32


def _cparams(n_axes, flags=None):
    return pltpu.CompilerParams(
        dimension_semantics=("parallel",) * n_axes, vmem_limit_bytes=VMEM_LIMIT, flags=flags)


def _const_spec(shape):
    nd = len(shape)
    return pl.BlockSpec(shape, lambda *_: (0,) * nd, pipeline_mode=pl.Buffered(1))


def _rms(x, g):
    return x * lax.rsqrt(jnp.mean(x * x, axis=-1, keepdims=True) + NORM_EPS) * g


def _dot(a, b):
    return jnp.dot(a, b, preferred_element_type=F32)


def _dot_nt(a, b):
    return lax.dot_general(a, b, (((1,), (1,)), ((), ())), preferred_element_type=F32)


def _mod_kernel(c_ref, w_ref, b_ref, o_ref):
    c = c_ref[...]
    a = c * jax.nn.sigmoid(c)
    o_ref[...] = _dot(a, w_ref[...]) + b_ref[...]


def _modulation(cc, w_mod, b_mod):
    tn = 1536
    return pl.pallas_call(
        _mod_kernel,
        grid=(DEPTH, 6 * D_MODEL // tn),
        in_specs=[
            pl.BlockSpec((8, D_MODEL), lambda l, n: (0, 0)),
            pl.BlockSpec((None, D_MODEL, tn), lambda l, n: (l, 0, n)),
            pl.BlockSpec((None, 1, tn), lambda l, n: (l, 0, n)),
        ],
        out_specs=pl.BlockSpec((None, 8, tn), lambda l, n: (l, 0, n)),
        out_shape=jax.ShapeDtypeStruct((DEPTH, 8, 6 * D_MODEL), F32),
        compiler_params=_cparams(2),
        name="modulation",
    )(cc, w_mod, b_mod.reshape(DEPTH, 1, 6 * D_MODEL))


def _mod_spec():
    return pl.BlockSpec((None, None, 6, D_MODEL),
                        lambda b, j: (b, (j == N_LAT_TILES).astype(jnp.int32), 0, 0))


def _x_spec():
    return pl.BlockSpec((None, TILE, D_MODEL), lambda b, j: (b, j, 0))


def _rope_spec():
    return pl.BlockSpec((TILE, LANES), lambda b, j: (j, 0))


def _pre_mix(x_ref, mod_ref, g_ref):
    shift = mod_ref[0:1, :]
    scale = mod_ref[1:2, :]
    h = _rms(x_ref[...], g_ref[...]) * (1.0 + scale) + shift
    return h.astype(BF16)


def _mla_proj_kernel(x_ref, mod_ref, g_ref, cos_ref, sin_ref, w_in_ref, gq_ref, gkv_ref,
                     wq_ref, wkv_ref, q_ref, k_ref, vt_ref, *, scale):
    h = _pre_mix(x_ref, mod_ref, g_ref)
    a = _dot(h, w_in_ref[...])
    qn = _rms(a[:, :MLA_Q_LORA], gq_ref[...]).astype(BF16)
    kvn = _rms(a[:, MLA_Q_LORA:MLA_Q_LORA + MLA_KV_LORA], gkv_ref[...]).astype(BF16)
    cos = cos_ref[...]
    sin = sin_ref[...]
    k_rope = (a[:, 512:640] * cos + a[:, 640:768] * sin).astype(BF16)
    qq = _dot(qn, wq_ref[...])
    kv = _dot(kvn, wkv_ref[...])
    hn = MLA_HEADS * LANES
    for hd in range(MLA_HEADS):
        lo, hi = hd * LANES, (hd + 1) * LANES
        base = hd * MLA_QK_PAD
        q_ref[:, base:base + LANES] = (qq[:, lo:hi] * scale).astype(BF16)
        q_rope = qq[:, hn + lo:hn + hi] * cos + qq[:, 2 * hn + lo:2 * hn + hi] * sin
        q_ref[:, base + LANES:base + 2 * LANES] = (q_rope * scale).astype(BF16)
        k_ref[:, base:base + LANES] = kv[:, lo:hi].astype(BF16)
        k_ref[:, base + LANES:base + 2 * LANES] = k_rope
    vt_ref[...] = kv[:, hn:].T.astype(BF16)


def _mla_proj(x, mod, g_pre, cos, sin, w_in, g_q, g_kv, wq, wkv):
    hq = MLA_HEADS * MLA_QK_PAD
    hv = MLA_HEADS * MLA_V
    out_spec_q = pl.BlockSpec((None, TILE, hq), lambda b, j: (b, j, 0))
    out_spec_v = pl.BlockSpec((None, hv, TILE), lambda b, j: (b, 0, j))
    scale = float((MLA_NOPE + MLA_ROPE) ** -0.5) * LOG2E
    return pl.pallas_call(
        functools.partial(_mla_proj_kernel, scale=scale),
        grid=(BATCH, N_TILES),
        in_specs=[_x_spec(), _mod_spec(), _const_spec((1, D_MODEL)), _rope_spec(), _rope_spec(),
                  _const_spec(w_in.shape), _const_spec((1, MLA_Q_LORA)),
                  _const_spec((1, MLA_KV_LORA)), _const_spec(wq.shape), _const_spec(wkv.shape)],
        out_specs=[out_spec_q, out_spec_q, out_spec_v],
        out_shape=[jax.ShapeDtypeStruct((BATCH, TOKENS, hq), BF16),
                   jax.ShapeDtypeStruct((BATCH, TOKENS, hq), BF16),
                   jax.ShapeDtypeStruct((BATCH, hv, TOKENS), BF16)],
        compiler_params=_cparams(2),
        name="mla_proj",
    )(x, mod, g_pre, cos, sin, w_in, g_q, g_kv, wq, wkv)


def _softmax_update(s, vt, carry):
    m, l, acc = carry
    m_new = jnp.maximum(m, jnp.max(s, axis=0, keepdims=True))
    alpha = jnp.exp2(m - m_new)
    p = jnp.exp2(s - m_new)
    l = alpha * l + jnp.sum(p, axis=0, keepdims=True)
    acc = alpha * acc + _dot(vt, p.astype(BF16))
    return m_new, l, acc


def _mla_attn_kernel(q_ref, k_ref, vt_ref, o_ref, *, chunks):
    q = q_ref[...]
    tq = q.shape[0]
    carry = (jnp.full((1, tq), NEG_INF, F32), jnp.zeros((1, tq), F32),
             jnp.zeros((MLA_V, tq), F32))

    def logits(chunk):
        start, size = chunk
        return _dot_nt(k_ref[start:start + size, :], q)

    s = logits(chunks[0])
    for c, (start, size) in enumerate(chunks):
        s_next = logits(chunks[c + 1]) if c + 1 < len(chunks) else None
        carry = _softmax_update(s, vt_ref[:, start:start + size], carry)
        s = s_next
    _, l, acc = carry
    o_ref[...] = (acc / l).T.astype(BF16)


def _mla_attn_ctx_kernel(q_ref, k_ref, vt_ref, latent_out_ref, o_ref):
    del latent_out_ref
    _mla_attn_kernel(q_ref, k_ref, vt_ref, o_ref, chunks=((0, CTX_LEN),))


def _mla_attention(q, k, vt):
    hv = MLA_HEADS * MLA_V
    chunks = ((SEQ, CTX_LEN),) + tuple((c * MLA_TK, MLA_TK) for c in range(SEQ // MLA_TK))
    o = pl.pallas_call(
        functools.partial(_mla_attn_kernel, chunks=chunks),
        grid=(BATCH, MLA_HEADS, SEQ // MLA_TQ),
        in_specs=[
            pl.BlockSpec((None, MLA_TQ, MLA_QK_PAD), lambda b, h, i: (b, i, h)),
            pl.BlockSpec((None, TOKENS, MLA_QK_PAD), lambda b, h, i: (b, 0, h)),
            pl.BlockSpec((None, MLA_V, TOKENS), lambda b, h, i: (b, h, 0)),
        ],
        out_specs=pl.BlockSpec((None, MLA_TQ, MLA_V), lambda b, h, i: (b, i, h)),
        out_shape=jax.ShapeDtypeStruct((BATCH, TOKENS, hv), BF16),
        compiler_params=_cparams(3),
        name="mla_attn",
    )(q, k, vt)
    ctx_blk = SEQ // CTX_LEN
    return pl.pallas_call(
        _mla_attn_ctx_kernel,
        grid=(BATCH, MLA_HEADS),
        in_specs=[
            pl.BlockSpec((None, CTX_LEN, MLA_QK_PAD), lambda b, h: (b, ctx_blk, h)),
            pl.BlockSpec((None, CTX_LEN, MLA_QK_PAD), lambda b, h: (b, ctx_blk, h)),
            pl.BlockSpec((None, MLA_V, CTX_LEN), lambda b, h: (b, h, ctx_blk)),
            pl.BlockSpec(memory_space=pl.ANY),
        ],
        out_specs=pl.BlockSpec((None, CTX_LEN, MLA_V), lambda b, h: (b, ctx_blk, h)),
        out_shape=jax.ShapeDtypeStruct((BATCH, TOKENS, hv), BF16),
        input_output_aliases={3: 0},
        compiler_params=_cparams(2),
        name="mla_attn_ctx",
    )(q, k, vt, o)


def _na_proj_kernel(x_ref, mod_ref, g_ref, w_ref, o_ref, *, scale):
    h = _pre_mix(x_ref, mod_ref, g_ref)
    a = _dot(h, w_ref[...])
    hd = NA_HEADS * NA_HEAD_DIM
    o_ref[:, :hd] = (a[:, :hd] * scale).astype(BF16)
    o_ref[:, hd:] = a[:, hd:].astype(BF16)


def _na_proj(x, mod, g_pre, w):
    n = w.shape[1]
    return pl.pallas_call(
        functools.partial(_na_proj_kernel, scale=float(NA_HEAD_DIM ** -0.5) * LOG2E),
        grid=(BATCH, N_TILES),
        in_specs=[_x_spec(), _mod_spec(), _const_spec((1, D_MODEL)), _const_spec(w.shape)],
        out_specs=pl.BlockSpec((None, TILE, n), lambda b, j: (b, j, 0)),
        out_shape=jax.ShapeDtypeStruct((BATCH, TOKENS, n), BF16),
        compiler_params=_cparams(2),
        name="na_proj",
    )(x, mod, g_pre, w)


def _swa_proj_kernel(x_ref, mod_ref, g_ref, cos_ref, sin_ref, w_ref, q_ref, k_ref, v_ref, *,
                     scale):
    h = _pre_mix(x_ref, mod_ref, g_ref)
    a = _dot(h, w_ref[...])
    cos = cos_ref[...]
    sin = sin_ref[...]
    hd = SWA_HEADS * SWA_HEAD_DIM
    for p in range(HEAD_PAIRS):
        lo, hi = p * LANES, (p + 1) * LANES
        q = a[:, lo:hi] * cos + a[:, hd + lo:hd + hi] * sin
        q_ref[:, lo:hi] = (q * scale).astype(BF16)
    kw = 2 * LANES
    for p in range(2):
        lo, hi = p * LANES, (p + 1) * LANES
        k = a[:, 2 * hd + lo:2 * hd + hi] * cos + a[:, 2 * hd + kw + lo:2 * hd + kw + hi] * sin
        k_ref[:, lo:hi] = k.astype(BF16)
    v_ref[...] = a[:, 2 * hd + 2 * kw:].astype(BF16)


def _swa_proj(x, mod, g_pre, cos, sin, w):
    hd = SWA_HEADS * SWA_HEAD_DIM
    kw = 2 * LANES
    return pl.pallas_call(
        functools.partial(_swa_proj_kernel, scale=float(SWA_HEAD_DIM ** -0.5) * LOG2E),
        grid=(BATCH, N_TILES),
        in_specs=[_x_spec(), _mod_spec(), _const_spec((1, D_MODEL)), _rope_spec(), _rope_spec(),
                  _const_spec(w.shape)],
        out_specs=[pl.BlockSpec((None, TILE, hd), lambda b, j: (b, j, 0)),
                   pl.BlockSpec((None, TILE, kw), lambda b, j: (b, j, 0)),
                   pl.BlockSpec((None, TILE, kw), lambda b, j: (b, j, 0))],
        out_shape=[jax.ShapeDtypeStruct((BATCH, TOKENS, hd), BF16),
                   jax.ShapeDtypeStruct((BATCH, TOKENS, kw), BF16),
                   jax.ShapeDtypeStruct((BATCH, TOKENS, kw), BF16)],
        compiler_params=_cparams(2),
        name="swa_proj",
    )(x, mod, g_pre, cos, sin, w)


def _local_attn_kernel(q_ref, k0_ref, k1_ref, k2_ref, kc_ref, v0_ref, v1_ref, v2_ref, vc_ref,
                       tbl_ref, sink_ref, o_ref, *, kv_slot, per_head_table, has_sink):
    lane = lax.broadcasted_iota(jnp.int32, (TILE, LANES), 1)
    low = lane < NA_HEAD_DIM
    for p in range(HEAD_PAIRS):
        qs = slice(p * LANES, (p + 1) * LANES)
        ks = slice(kv_slot(p) * LANES, (kv_slot(p) + 1) * LANES)
        qp = q_ref[:, qs]
        kn = jnp.concatenate([k0_ref[:, ks], k1_ref[:, ks], k2_ref[:, ks]], axis=0)
        vn = jnp.concatenate([v0_ref[:, ks], v1_ref[:, ks], v2_ref[:, ks]], axis=0)
        kc = kc_ref[:, ks]
        vc = vc_ref[:, ks]
        outs = []
        for half in range(2):
            hd = 2 * p + half
            qm = jnp.where(low if half == 0 else jnp.logical_not(low), qp, jnp.zeros_like(qp))
            s_n = _dot_nt(qm, kn) + tbl_ref[hd if per_head_table else 0]
            s_c = _dot_nt(qm, kc)
            m = jnp.maximum(jnp.max(s_n, axis=-1, keepdims=True),
                            jnp.max(s_c, axis=-1, keepdims=True))
            if has_sink:
                sink = sink_ref[hd:hd + 1, 0:1]
                m = jnp.maximum(m, sink)
            p_n = jnp.exp2(s_n - m)
            p_c = jnp.exp2(s_c - m)
            l = jnp.sum(p_n, axis=-1, keepdims=True) + jnp.sum(p_c, axis=-1, keepdims=True)
            if has_sink:
                l = l + jnp.exp2(sink - m)
            o = _dot(p_n.astype(BF16), vn) + _dot(p_c.astype(BF16), vc)
            outs.append(o / l)
        o_ref[:, qs] = jnp.where(low, outs[0], outs[1]).astype(BF16)


def _local_attention(q, q_col, k, k_col, v, v_col, kv_width, table, sink, kv_slot, name):
    hd = NA_HEADS * NA_HEAD_DIM
    per_head_table = table.shape[1] > 1
    lat_hi = N_LAT_TILES - 1

    def nb_spec(col, off):
        return pl.BlockSpec(
            (None, TILE, kv_width),
            lambda b, j: (b, jnp.clip(j + off, 0, lat_hi), col))

    def ctx_spec(col):
        return pl.BlockSpec((None, TILE, kv_width), lambda b, j: (b, N_LAT_TILES, col))

    def variant(j):
        return jnp.where(j == 0, 0, jnp.where(j == lat_hi, 2, jnp.where(j > lat_hi, 3, 1)))

    tbl_spec = pl.BlockSpec((None,) + table.shape[1:], lambda b, j: (variant(j), 0, 0, 0))
    return pl.pallas_call(
        functools.partial(_local_attn_kernel, kv_slot=kv_slot, per_head_table=per_head_table,
                          has_sink=sink is not None),
        grid=(BATCH, N_TILES),
        in_specs=[pl.BlockSpec((None, TILE, hd), lambda b, j: (b, j, q_col)),
                  nb_spec(k_col, -1), nb_spec(k_col, 0), nb_spec(k_col, 1), ctx_spec(k_col),
                  nb_spec(v_col, -1), nb_spec(v_col, 0), nb_spec(v_col, 1), ctx_spec(v_col),
                  tbl_spec, _const_spec((NA_HEADS, LANES))],
        out_specs=pl.BlockSpec((None, TILE, hd), lambda b, j: (b, j, 0)),
        out_shape=jax.ShapeDtypeStruct((BATCH, TOKENS, hd), BF16),
        compiler_params=_cparams(2),
        name=name,
    )(q, k, k, k, k, v, v, v, v, table,
      jnp.zeros((NA_HEADS, LANES), F32) if sink is None else sink)


def _post_ffn_kernel(x_ref, o_ref, mod_ref, gpm_ref, gpf_ref, gof_ref, wo_ref, win_ref,
                     wout_ref, y_ref, act_ref):
    gate_m = mod_ref[2:3, :]
    shift_f = mod_ref[3:4, :]
    scale_f = mod_ref[4:5, :]
    gate_f = mod_ref[5:6, :]
    mixed = _dot(o_ref[...], wo_ref[...])
    x1 = x_ref[...] + gate_m * _rms(mixed, gpm_ref[...])
    h = (_rms(x1, gpf_ref[...]) * (1.0 + scale_f) + shift_f).astype(BF16)
    chunk = 256
    for f in range(D_FF // chunk):
        lo, hi = f * chunk, (f + 1) * chunk
        gate = _dot(h, win_ref[:, lo:hi])
        up = _dot(h, win_ref[:, D_FF + lo:D_FF + hi])
        act_ref[:, lo:hi] = (gate * jax.nn.sigmoid(gate) * up).astype(BF16)
    ffn = _dot(act_ref[...], wout_ref[...])
    y_ref[...] = x1 + gate_f * _rms(ffn, gof_ref[...])


def _post_ffn(x, o, mod, g_post_mix, g_pre_ffn, g_post_ffn, wo, win, wout, n_tiles):
    vec = _const_spec((1, D_MODEL))
    return pl.pallas_call(
        _post_ffn_kernel,
        grid=(BATCH, n_tiles),
        in_specs=[_x_spec(), _x_spec(), _mod_spec(), vec, vec, vec,
                  _const_spec(wo.shape), _const_spec(win.shape), _const_spec(wout.shape)],
        out_specs=_x_spec(),
        out_shape=jax.ShapeDtypeStruct((BATCH, n_tiles * TILE, D_MODEL), F32),
        scratch_shapes=[pltpu.VMEM((TILE, D_FF), BF16)],
        compiler_params=_cparams(2),
        name="post_ffn",
    )(x, o, mod, g_post_mix, g_pre_ffn, g_post_ffn, wo, win, wout)


def _rope_tables():
    t = jnp.arange(SEQ)
    row = (t // GRID_W).astype(F32)
    col = (t % GRID_W).astype(F32)
    n_freq = MLA_ROPE // 4
    inv = ROPE_THETA ** (-jnp.arange(n_freq, dtype=F32) / n_freq)
    ang = jnp.concatenate([row[:, None] * inv, col[:, None] * inv], axis=-1)
    cos, sin = jnp.cos(ang), jnp.sin(ang)
    cos_t = jnp.concatenate([cos, cos, cos, cos], axis=-1)
    sin_t = jnp.concatenate([-sin, sin, -sin, sin], axis=-1)
    cos_t = jnp.concatenate([cos_t, jnp.ones((CTX_LEN, LANES), F32)], axis=0)
    sin_t = jnp.concatenate([sin_t, jnp.zeros((CTX_LEN, LANES), F32)], axis=0)
    return cos_t, sin_t


def _variant_tiles():
    return jnp.array([0, 1, N_LAT_TILES - 1], jnp.int32)


NA_ROWS_PER_TILE = TILE // GRID_W
NA_BIAS_ROWS = 2 * NA_WIN_ROWS - 1
NA_BIAS_COLS = 2 * NA_WIN_COLS - 1


def _na_table_kernel(u_ref, o_ref):
    rows = SEQ // GRID_W
    qc = lax.broadcasted_iota(jnp.int32, (GRID_W, LANES), 0)
    lane = lax.broadcasted_iota(jnp.int32, (GRID_W, LANES), 1)
    kc = lane % GRID_W
    col_start = jnp.clip(qc - NA_WIN_COLS // 2, 0, GRID_W - NA_WIN_COLS)
    col_ok = (kc >= col_start) & (kc < col_start + NA_WIN_COLS)
    first_half = lane < GRID_W
    neg = jnp.full((GRID_W, LANES), NEG_INF, F32)
    toeplitz = {}
    for i in range(1, NA_BIAS_ROWS):
        row = jnp.broadcast_to(u_ref[i:i + 1, :], (GRID_W, LANES))
        toeplitz[i] = pltpu.roll(row, LANES - (NA_WIN_COLS - 1), 1, stride=1,
                                 stride_axis=0) * LOG2E
    for v, tile in enumerate((0, 1, N_LAT_TILES - 1)):
        for ql in range(NA_ROWS_PER_TILE):
            qr = NA_ROWS_PER_TILE * tile + ql
            row_start = min(max(qr - NA_WIN_ROWS // 2, 0), rows - NA_WIN_ROWS)
            for kp in range(3 * NA_ROWS_PER_TILE // 2):
                kr = NA_ROWS_PER_TILE * (tile - 1) + 2 * kp
                ok0 = row_start <= kr < row_start + NA_WIN_ROWS
                ok1 = row_start <= kr + 1 < row_start + NA_WIN_ROWS
                if ok0 and ok1:
                    ok = col_ok
                elif ok0:
                    ok = col_ok & first_half
                elif ok1:
                    ok = col_ok & jnp.logical_not(first_half)
                else:
                    ok = None
                i = kr - qr + NA_WIN_ROWS
                blk = neg if ok is None else jnp.where(ok, toeplitz[i], neg)
                o_ref[v, ql * GRID_W:(ql + 1) * GRID_W, kp * LANES:(kp + 1) * LANES] = blk
    o_ref[N_TABLE_VARIANTS - 1] = jnp.full((TILE, 3 * TILE), NEG_INF, F32)


def _na_table(rpb):
    pad = jnp.full((NA_HEADS, NA_BIAS_ROWS, GRID_W - NA_BIAS_COLS), NEG_INF, F32)
    rows = jnp.concatenate([rpb.astype(F32), pad], axis=-1)
    off = jnp.full((NA_HEADS, 1, GRID_W), NEG_INF, F32)
    left = jnp.concatenate([off, rows], axis=1)
    right = jnp.concatenate([rows, off], axis=1)
    u = jnp.concatenate([left, right], axis=-1)
    return pl.pallas_call(
        _na_table_kernel,
        grid=(NA_HEADS,),
        in_specs=[pl.BlockSpec((None, NA_BIAS_ROWS + 1, LANES), lambda h: (h, 0, 0))],
        out_specs=pl.BlockSpec((N_TABLE_VARIANTS, None, TILE, 3 * TILE),
                               lambda h: (0, h, 0, 0)),
        out_shape=jax.ShapeDtypeStruct((N_TABLE_VARIANTS, NA_HEADS, TILE, 3 * TILE), F32),
        compiler_params=_cparams(1),
        name="na_table",
    )(u)


def _swa_table():
    j = _variant_tiles()[:, None, None]
    qpos = TILE * j + jnp.arange(TILE)[None, :, None]
    kpos = TILE * (j - 1) + jnp.arange(3 * TILE)[None, None, :]
    ok = (jnp.abs(kpos - qpos) <= SWA_WINDOW) & (kpos >= 0) & (kpos < SEQ)
    tbl = jnp.where(ok, 0.0, NEG_INF).astype(F32)
    tbl = jnp.concatenate([tbl, jnp.full((1, TILE, 3 * TILE), NEG_INF, F32)], axis=0)
    return tbl[:, None]


def _swap_halves(w):
    half = w.shape[-1] // 2
    return jnp.concatenate([w[..., half:], w[..., :half]], axis=-1)


def _mla_weights(w_in, w_uq, w_ukv):
    lat = MLA_Q_LORA + MLA_KV_LORA
    kpe = w_in[:, lat:]
    z = jnp.zeros((D_MODEL, LANES - MLA_ROPE), w_in.dtype)
    w_in_x = jnp.concatenate([w_in[:, :lat], kpe, z, _swap_halves(kpe), z], axis=1)
    uq = w_uq.reshape(MLA_Q_LORA, MLA_HEADS, MLA_NOPE + MLA_ROPE)
    rope = uq[:, :, MLA_NOPE:]
    zq = jnp.zeros((MLA_Q_LORA, MLA_HEADS, LANES - MLA_ROPE), w_uq.dtype)
    flat = lambda a: a.reshape(a.shape[0], -1)
    wq_x = jnp.concatenate([flat(uq[:, :, :MLA_NOPE]),
                            flat(jnp.concatenate([rope, zq], axis=-1)),
                            flat(jnp.concatenate([_swap_halves(rope), zq], axis=-1))], axis=1)
    ukv = w_ukv.reshape(MLA_KV_LORA, MLA_HEADS, MLA_NOPE + MLA_V)
    wkv_x = jnp.concatenate([flat(ukv[:, :, :MLA_NOPE]), flat(ukv[:, :, MLA_NOPE:])], axis=1)
    return w_in_x.astype(BF16), wq_x.astype(BF16), wkv_x.astype(BF16)


def _swa_weights(w_qkv):
    hd = SWA_HEADS * SWA_HEAD_DIM
    kd = SWA_KV_HEADS * SWA_HEAD_DIM
    wq = w_qkv[:, :hd].reshape(D_MODEL, SWA_HEADS, SWA_HEAD_DIM)
    wk = w_qkv[:, hd:hd + kd].reshape(D_MODEL, SWA_KV_HEADS, 1, SWA_HEAD_DIM)
    wv = w_qkv[:, hd + kd:].reshape(D_MODEL, SWA_KV_HEADS, 1, SWA_HEAD_DIM)
    dup = lambda a: jnp.broadcast_to(a, (D_MODEL, SWA_KV_HEADS, 2, SWA_HEAD_DIM)).reshape(
        D_MODEL, -1)
    flat = lambda a: a.reshape(D_MODEL, -1)
    w = jnp.concatenate([flat(wq), flat(_swap_halves(wq)), dup(wk), dup(_swap_halves(wk)),
                         dup(wv)], axis=1)
    return w.astype(BF16)


def kernel(x, c, ctx, c_ctx, w_mod, b_mod, g_pre_mix, g_post_mix, g_pre_ffn, g_post_ffn,
           w_ffn_in, w_ffn_out, mla_w_in, mla_g_q, mla_w_uq, mla_g_kv, mla_w_ukv, mla_w_o,
           na_w_qkv, na_rpb, na_w_o, swa_w_qkv, swa_sink, swa_w_o):
    assert x.shape == (BATCH, SEQ, D_MODEL) and ctx.shape == (BATCH, CTX_LEN, D_MODEL)
    stream = jnp.concatenate([x, ctx], axis=1)
    cc = jnp.concatenate([c, c_ctx[None], jnp.zeros((8 - BATCH - 1, D_MODEL), F32)], axis=0)
    mods = _modulation(cc, w_mod, b_mod)
    mods = mods.reshape(DEPTH, 8, 6, D_MODEL)
    mods = jnp.stack([mods[:, :BATCH],
                      jnp.broadcast_to(mods[:, BATCH:BATCH + 1], (DEPTH, BATCH, 6, D_MODEL))],
                     axis=2)
    cos_t, sin_t = _rope_tables()
    row = lambda g: g.reshape(1, -1)

    for i in range(DEPTH):
        last = i == DEPTH - 1
        kind, j = i % 3, i // 3
        mod = mods[i]
        g_pre = row(g_pre_mix[i])
        if kind == 0:
            w_in_x, wq_x, wkv_x = _mla_weights(mla_w_in[j], mla_w_uq[j], mla_w_ukv[j])
            q, k, v = _mla_proj(stream, mod, g_pre, cos_t, sin_t, w_in_x, row(mla_g_q[j]),
                                row(mla_g_kv[j]), wq_x, wkv_x)
            o = _mla_attention(q, k, v)
            w_o = mla_w_o[j]
        elif kind == 1:
            qkv = _na_proj(stream, mod, g_pre, na_w_qkv[j].astype(BF16))
            o = _local_attention(qkv, 0, qkv, 1, qkv, 2, NA_HEADS * NA_HEAD_DIM,
                                 _na_table(na_rpb[j]), None, lambda p: p, "na_attn")
            w_o = na_w_o[j]
        else:
            q, k, v = _swa_proj(stream, mod, g_pre, cos_t, sin_t, _swa_weights(swa_w_qkv[j]))
            sink = jnp.broadcast_to((swa_sink[j].astype(F32) * LOG2E)[:, None],
                                    (SWA_HEADS, LANES))
            o = _local_attention(q, 0, k, 0, v, 0, 2 * LANES, _swa_table(), sink,
                                 lambda p: p // (HEAD_PAIRS // SWA_KV_HEADS), "swa_attn")
            w_o = swa_w_o[j]
        stream = _post_ffn(stream, o, mod, row(g_post_mix[i]), row(g_pre_ffn[i]),
                           row(g_post_ffn[i]), w_o.astype(BF16), w_ffn_in[i].astype(BF16),
                           w_ffn_out[i].astype(BF16), N_LAT_TILES if last else N_TILES)
    return stream
```

```python
import functools

import jax
import jax.numpy as jnp
from jax import lax
from jax.experimental import pallas as pl
from jax.experimental.pallas import tpu as pltpu

D_MODEL = 1024
BATCH = 4
SEQ = 8192
DEPTH = 4
GRID_W = 64
CTX_LEN = 256
D_FF = 2816
NORM_EPS = 1e-6
ROPE_THETA = 10000.0
NEG_INF = -1e30

MLA_HEADS = 8
MLA_Q_LORA = 256
MLA_KV_LORA = 256
MLA_NOPE = 128
MLA_ROPE = 64
MLA_V = 128
MLA_QK_PAD = 256

NA_HEADS = 16
NA_HEAD_DIM = 64
NA_WIN_ROWS = 8
NA_WIN_COLS = 16

SWA_HEADS = 16
SWA_KV_HEADS = 2
SWA_HEAD_DIM = 64
SWA_WINDOW = 128

TOKENS = SEQ + CTX_LEN
TILE = 256
N_TILES = TOKENS // TILE
N_LAT_TILES = SEQ // TILE
LANES = 128
BF16_ROWS = 16
HEAD_PAIRS = 8
N_TABLE_VARIANTS = 4

MLA_TQ = 512
MLA_TK = 1024

NA_NB_TOKENS = TILE
NA_NB_BLOCKS = 3
SWA_NB_TOKENS = 128
SWA_NB_BLOCKS = 4

VMEM_LIMIT = 56 * 1024 * 1024

LOG2E = 1.4426950408889634

BF16 = jnp.bfloat16
F32 = jnp.float32


def _cparams(n_axes):
    return pltpu.CompilerParams(
        dimension_semantics=("parallel",) * n_axes, vmem_limit_bytes=VMEM_LIMIT)


def _const_spec(shape):
    nd = len(shape)
    return pl.BlockSpec(shape, lambda *_: (0,) * nd, pipeline_mode=pl.Buffered(1))


def _rms(x, g):
    return x * lax.rsqrt(jnp.mean(x * x, axis=-1, keepdims=True) + NORM_EPS) * g


def _dot(a, b):
    return jnp.dot(a, b, preferred_element_type=F32)


def _dot_nt(a, b):
    return lax.dot_general(a, b, (((1,), (1,)), ((), ())), preferred_element_type=F32)


def _ones_rows(n):
    row = lax.broadcasted_iota(jnp.int32, (BF16_ROWS, n), 0)
    return jnp.where(row == 0, 1.0, 0.0).astype(BF16)


def _mod_kernel(c_ref, w_ref, b_ref, o_ref):
    c = c_ref[...]
    a = c * jax.nn.sigmoid(c)
    o_ref[...] = _dot(a, w_ref[...]) + b_ref[...]


def _modulation(cc, w_mod, b_mod):
    tn = 1536
    return pl.pallas_call(
        _mod_kernel,
        grid=(DEPTH, 6 * D_MODEL // tn),
        in_specs=[
            pl.BlockSpec((8, D_MODEL), lambda l, n: (0, 0)),
            pl.BlockSpec((None, D_MODEL, tn), lambda l, n: (l, 0, n)),
            pl.BlockSpec((None, 1, tn), lambda l, n: (l, 0, n)),
        ],
        out_specs=pl.BlockSpec((None, 8, tn), lambda l, n: (l, 0, n)),
        out_shape=jax.ShapeDtypeStruct((DEPTH, 8, 6 * D_MODEL), F32),
        compiler_params=_cparams(2),
        name="modulation",
    )(cc, w_mod, b_mod.reshape(DEPTH, 1, 6 * D_MODEL))


def _mod_spec():
    return pl.BlockSpec((None, None, 6, D_MODEL),
                        lambda b, j: (b, (j == N_LAT_TILES).astype(jnp.int32), 0, 0))


def _x_spec():
    return pl.BlockSpec((None, TILE, D_MODEL), lambda b, j: (b, j, 0))


def _rope_spec():
    return pl.BlockSpec((TILE, LANES), lambda b, j: (j, 0))


def _pre_mix(x_ref, mod_ref, g_ref):
    shift = mod_ref[0:1, :]
    scale = mod_ref[1:2, :]
    h = _rms(x_ref[...], g_ref[...]) * (1.0 + scale) + shift
    return h.astype(BF16)


def _mla_proj_kernel(x_ref, mod_ref, g_ref, cos_ref, sin_ref, w_in_ref, gq_ref, gkv_ref,
                     wq_ref, wkv_ref, q_ref, k_ref, vt_ref, *, scale):
    h = _pre_mix(x_ref, mod_ref, g_ref)
    a = _dot(h, w_in_ref[...])
    qn = _rms(a[:, :MLA_Q_LORA], gq_ref[...]).astype(BF16)
    kvn = _rms(a[:, MLA_Q_LORA:MLA_Q_LORA + MLA_KV_LORA], gkv_ref[...]).astype(BF16)
    cos = cos_ref[...]
    sin = sin_ref[...]
    k_rope = (a[:, 512:640] * cos + a[:, 640:768] * sin).astype(BF16)
    qq = _dot(qn, wq_ref[...])
    kv = _dot(kvn, wkv_ref[...])
    hn = MLA_HEADS * LANES
    for hd in range(MLA_HEADS):
        lo, hi = hd * LANES, (hd + 1) * LANES
        base = hd * MLA_QK_PAD
        q_ref[:, base:base + LANES] = (qq[:, lo:hi] * scale).astype(BF16)
        q_rope = qq[:, hn + lo:hn + hi] * cos + qq[:, 2 * hn + lo:2 * hn + hi] * sin
        q_ref[:, base + LANES:base + 2 * LANES] = (q_rope * scale).astype(BF16)
        k_ref[:, base:base + LANES] = kv[:, lo:hi].astype(BF16)
        k_ref[:, base + LANES:base + 2 * LANES] = k_rope
    vt_ref[...] = kv[:, hn:].T.astype(BF16)


def _mla_proj(x, mod, g_pre, cos, sin, w_in, g_q, g_kv, wq, wkv):
    hq = MLA_HEADS * MLA_QK_PAD
    hv = MLA_HEADS * MLA_V
    out_spec_q = pl.BlockSpec((None, TILE, hq), lambda b, j: (b, j, 0))
    out_spec_v = pl.BlockSpec((None, hv, TILE), lambda b, j: (b, 0, j))
    scale = float((MLA_NOPE + MLA_ROPE) ** -0.5) * LOG2E
    return pl.pallas_call(
        functools.partial(_mla_proj_kernel, scale=scale),
        grid=(BATCH, N_TILES),
        in_specs=[_x_spec(), _mod_spec(), _const_spec((1, D_MODEL)), _rope_spec(), _rope_spec(),
                  _const_spec(w_in.shape), _const_spec((1, MLA_Q_LORA)),
                  _const_spec((1, MLA_KV_LORA)), _const_spec(wq.shape), _const_spec(wkv.shape)],
        out_specs=[out_spec_q, out_spec_q, out_spec_v],
        out_shape=[jax.ShapeDtypeStruct((BATCH, TOKENS, hq), BF16),
                   jax.ShapeDtypeStruct((BATCH, TOKENS, hq), BF16),
                   jax.ShapeDtypeStruct((BATCH, hv, TOKENS), BF16)],
        compiler_params=_cparams(2),
        name="mla_proj",
    )(x, mod, g_pre, cos, sin, w_in, g_q, g_kv, wq, wkv)


def _softmax_update(s, vt, carry):
    m, l, acc = carry
    m_new = jnp.maximum(m, jnp.max(s, axis=0, keepdims=True))
    alpha = jnp.exp2(m - m_new)
    p = jnp.exp2(s - m_new)
    l = alpha * l + jnp.sum(p, axis=0, keepdims=True)
    acc = alpha * acc + _dot(vt, p.astype(BF16))
    return m_new, l, acc


def _mla_attn_kernel(q_ref, k_ref, vt_ref, o_ref, *, chunks):
    q = q_ref[...]
    tq = q.shape[0]
    carry = (jnp.full((1, tq), NEG_INF, F32), jnp.zeros((1, tq), F32),
             jnp.zeros((MLA_V, tq), F32))

    def logits(chunk):
        start, size = chunk
        return _dot_nt(k_ref[start:start + size, :], q)

    s = logits(chunks[0])
    for c, (start, size) in enumerate(chunks):
        s_next = logits(chunks[c + 1]) if c + 1 < len(chunks) else None
        carry = _softmax_update(s, vt_ref[:, start:start + size], carry)
        s = s_next
    _, l, acc = carry
    o_ref[...] = (acc / l).T.astype(BF16)


def _mla_attn_ctx_kernel(q_ref, k_ref, vt_ref, latent_out_ref, o_ref):
    del latent_out_ref
    _mla_attn_kernel(q_ref, k_ref, vt_ref, o_ref, chunks=((0, CTX_LEN),))


def _mla_attention(q, k, vt):
    hv = MLA_HEADS * MLA_V
    chunks = ((SEQ, CTX_LEN),) + tuple((c * MLA_TK, MLA_TK) for c in range(SEQ // MLA_TK))
    o = pl.pallas_call(
        functools.partial(_mla_attn_kernel, chunks=chunks),
        grid=(BATCH, MLA_HEADS, SEQ // MLA_TQ),
        in_specs=[
            pl.BlockSpec((None, MLA_TQ, MLA_QK_PAD), lambda b, h, i: (b, i, h)),
            pl.BlockSpec((None, TOKENS, MLA_QK_PAD), lambda b, h, i: (b, 0, h)),
            pl.BlockSpec((None, MLA_V, TOKENS), lambda b, h, i: (b, h, 0)),
        ],
        out_specs=pl.BlockSpec((None, MLA_TQ, MLA_V), lambda b, h, i: (b, i, h)),
        out_shape=jax.ShapeDtypeStruct((BATCH, TOKENS, hv), BF16),
        compiler_params=_cparams(3),
        name="mla_attn",
    )(q, k, vt)
    ctx_blk = SEQ // CTX_LEN
    return pl.pallas_call(
        _mla_attn_ctx_kernel,
        grid=(BATCH, MLA_HEADS),
        in_specs=[
            pl.BlockSpec((None, CTX_LEN, MLA_QK_PAD), lambda b, h: (b, ctx_blk, h)),
            pl.BlockSpec((None, CTX_LEN, MLA_QK_PAD), lambda b, h: (b, ctx_blk, h)),
            pl.BlockSpec((None, MLA_V, CTX_LEN), lambda b, h: (b, h, ctx_blk)),
            pl.BlockSpec(memory_space=pl.ANY),
        ],
        out_specs=pl.BlockSpec((None, CTX_LEN, MLA_V), lambda b, h: (b, ctx_blk, h)),
        out_shape=jax.ShapeDtypeStruct((BATCH, TOKENS, hv), BF16),
        input_output_aliases={3: 0},
        compiler_params=_cparams(2),
        name="mla_attn_ctx",
    )(q, k, vt, o)


def _na_proj_kernel(x_ref, mod_ref, g_ref, w_ref, qk_ref, vt_ref, *, scale):
    h = _pre_mix(x_ref, mod_ref, g_ref)
    a = _dot(h, w_ref[...])
    hd = NA_HEADS * NA_HEAD_DIM
    qk_ref[:, :hd] = (a[:, :hd] * scale).astype(BF16)
    qk_ref[:, hd:] = a[:, hd:2 * hd].astype(BF16)
    vt_ref[...] = a[:, 2 * hd:].T.astype(BF16)


def _na_proj(x, mod, g_pre, w):
    hd = NA_HEADS * NA_HEAD_DIM
    return pl.pallas_call(
        functools.partial(_na_proj_kernel, scale=float(NA_HEAD_DIM ** -0.5) * LOG2E),
        grid=(BATCH, N_TILES),
        in_specs=[_x_spec(), _mod_spec(), _const_spec((1, D_MODEL)), _const_spec(w.shape)],
        out_specs=[pl.BlockSpec((None, TILE, 2 * hd), lambda b, j: (b, j, 0)),
                   pl.BlockSpec((None, hd, TILE), lambda b, j: (b, 0, j))],
        out_shape=[jax.ShapeDtypeStruct((BATCH, TOKENS, 2 * hd), BF16),
                   jax.ShapeDtypeStruct((BATCH, hd, TOKENS), BF16)],
        compiler_params=_cparams(2),
        name="na_proj",
    )(x, mod, g_pre, w)


def _swa_proj_kernel(x_ref, mod_ref, g_ref, cos_ref, sin_ref, w_ref, q_ref, k_ref, vt_ref, *,
                     scale):
    h = _pre_mix(x_ref, mod_ref, g_ref)
    a = _dot(h, w_ref[...])
    cos = cos_ref[...]
    sin = sin_ref[...]
    hd = SWA_HEADS * SWA_HEAD_DIM
    for p in range(HEAD_PAIRS):
        lo, hi = p * LANES, (p + 1) * LANES
        q = a[:, lo:hi] * cos + a[:, hd + lo:hd + hi] * sin
        q_ref[:, lo:hi] = (q * scale).astype(BF16)
    kw = 2 * LANES
    for p in range(2):
        lo, hi = p * LANES, (p + 1) * LANES
        k = a[:, 2 * hd + lo:2 * hd + hi] * cos + a[:, 2 * hd + kw + lo:2 * hd + kw + hi] * sin
        k_ref[:, lo:hi] = k.astype(BF16)
    vt_ref[...] = a[:, 2 * hd + 2 * kw:].T.astype(BF16)


def _swa_proj(x, mod, g_pre, cos, sin, w):
    hd = SWA_HEADS * SWA_HEAD_DIM
    kw = 2 * LANES
    return pl.pallas_call(
        functools.partial(_swa_proj_kernel, scale=float(SWA_HEAD_DIM ** -0.5) * LOG2E),
        grid=(BATCH, N_TILES),
        in_specs=[_x_spec(), _mod_spec(), _const_spec((1, D_MODEL)), _rope_spec(), _rope_spec(),
                  _const_spec(w.shape)],
        out_specs=[pl.BlockSpec((None, TILE, hd), lambda b, j: (b, j, 0)),
                   pl.BlockSpec((None, TILE, kw), lambda b, j: (b, j, 0)),
                   pl.BlockSpec((None, kw, TILE), lambda b, j: (b, 0, j))],
        out_shape=[jax.ShapeDtypeStruct((BATCH, TOKENS, hd), BF16),
                   jax.ShapeDtypeStruct((BATCH, TOKENS, kw), BF16),
                   jax.ShapeDtypeStruct((BATCH, kw, TOKENS), BF16)],
        compiler_params=_cparams(2),
        name="swa_proj",
    )(x, mod, g_pre, cos, sin, w)


def _local_attn_kernel(*refs, n_nb, kv_slot, per_head_table, has_sink):
    q_ref = refs[0]
    k_refs = refs[1:1 + n_nb]
    kc_ref = refs[1 + n_nb]
    vt_refs = refs[2 + n_nb:2 + 2 * n_nb]
    vtc_ref = refs[2 + 2 * n_nb]
    tbl_ref, sink_ref, o_ref = refs[3 + 2 * n_nb:]
    n_heads = 2 * HEAD_PAIRS
    hdim = NA_HEAD_DIM
    low = lax.broadcasted_iota(jnp.int32, (TILE, LANES), 1) < hdim

    def logits(hd):
        p, half = divmod(hd, 2)
        qp = q_ref[:, p * LANES:(p + 1) * LANES]
        qm = jnp.where(low if half == 0 else jnp.logical_not(low), qp, jnp.zeros_like(qp))
        ks = slice(kv_slot(p) * LANES, (kv_slot(p) + 1) * LANES)
        kn = jnp.concatenate([r[:, ks] for r in k_refs], axis=0)
        s_n = _dot_nt(kn, qm) + tbl_ref[hd if per_head_table else 0]
        s_c = _dot_nt(kc_ref[:, ks], qm)
        return s_n, s_c

    s = logits(0)
    pair_out = []
    for hd in range(n_heads):
        s_next = logits(hd + 1) if hd + 1 < n_heads else None
        p, half = divmod(hd, 2)
        s_n, s_c = s
        m = jnp.maximum(jnp.max(s_n, axis=0, keepdims=True), jnp.max(s_c, axis=0, keepdims=True))
        if has_sink:
            sink = sink_ref[hd:hd + 1, :]
            m = jnp.maximum(m, sink)
        p_n = jnp.exp2(s_n - m).astype(BF16)
        p_c = jnp.exp2(s_c - m).astype(BF16)
        lo = kv_slot(p) * LANES + half * hdim
        vt_n = jnp.concatenate([r[lo:lo + hdim, :] for r in vt_refs], axis=1)
        vt_n = jnp.concatenate([vt_n, _ones_rows(vt_n.shape[1])], axis=0)
        vt_c = jnp.concatenate([vtc_ref[lo:lo + hdim, :], _ones_rows(CTX_LEN)], axis=0)
        acc = _dot(vt_n, p_n) + _dot(vt_c, p_c)
        l = acc[hdim:hdim + 1]
        if has_sink:
            l = l + jnp.exp2(sink - m)
        pair_out.append(acc[:hdim] / l)
        if half == 1:
            pair = jnp.concatenate(pair_out, axis=0)
            o_ref[:, p * LANES:(p + 1) * LANES] = pair.T.astype(BF16)
            pair_out = []
        s = s_next


def _local_attention(q, q_col, k, k_col, k_width, vt, nb_tokens, n_nb, table, sink, kv_slot,
                     name):
    hd = NA_HEADS * NA_HEAD_DIM
    per_head_table = table.shape[1] > 1
    lat_hi = N_LAT_TILES - 1
    per_tile = TILE // nb_tokens
    nb_hi = SEQ // nb_tokens - 1

    def nb_index(j, n):
        return jnp.clip(j * per_tile - 1 + n, 0, nb_hi)

    def k_spec(n):
        return pl.BlockSpec((None, nb_tokens, k_width), lambda b, j: (b, nb_index(j, n), k_col))

    def vt_spec(n):
        return pl.BlockSpec((None, k_width, nb_tokens), lambda b, j: (b, 0, nb_index(j, n)))

    def variant(j):
        return jnp.where(j == 0, 0, jnp.where(j == lat_hi, 2, jnp.where(j > lat_hi, 3, 1)))

    tbl_spec = pl.BlockSpec((None,) + table.shape[1:], lambda b, j: (variant(j), 0, 0, 0))
    in_specs = ([pl.BlockSpec((None, TILE, hd), lambda b, j: (b, j, q_col))]
                + [k_spec(n) for n in range(n_nb)]
                + [pl.BlockSpec((None, CTX_LEN, k_width), lambda b, j: (b, N_LAT_TILES, k_col))]
                + [vt_spec(n) for n in range(n_nb)]
                + [pl.BlockSpec((None, k_width, CTX_LEN), lambda b, j: (b, 0, N_LAT_TILES))]
                + [tbl_spec, _const_spec((NA_HEADS, TILE))])
    sink_arr = jnp.zeros((NA_HEADS, TILE), F32) if sink is None else sink
    return pl.pallas_call(
        functools.partial(_local_attn_kernel, n_nb=n_nb, kv_slot=kv_slot,
                          per_head_table=per_head_table, has_sink=sink is not None),
        grid=(BATCH, N_TILES),
        in_specs=in_specs,
        out_specs=pl.BlockSpec((None, TILE, hd), lambda b, j: (b, j, 0)),
        out_shape=jax.ShapeDtypeStruct((BATCH, TOKENS, hd), BF16),
        compiler_params=_cparams(2),
        name=name,
    )(q, *([k] * (n_nb + 1)), *([vt] * (n_nb + 1)), table, sink_arr)


def _post_ffn_kernel(x_ref, o_ref, mod_ref, gpm_ref, gpf_ref, gof_ref, wo_ref, win_ref,
                     wout_ref, y_ref, act_ref):
    gate_m = mod_ref[2:3, :]
    shift_f = mod_ref[3:4, :]
    scale_f = mod_ref[4:5, :]
    gate_f = mod_ref[5:6, :]
    mixed = _dot(o_ref[...], wo_ref[...])
    x1 = x_ref[...] + gate_m * _rms(mixed, gpm_ref[...])
    h = (_rms(x1, gpf_ref[...]) * (1.0 + scale_f) + shift_f).astype(BF16)
    chunk = 256
    for f in range(D_FF // chunk):
        lo, hi = f * chunk, (f + 1) * chunk
        gate = _dot(h, win_ref[:, lo:hi])
        up = _dot(h, win_ref[:, D_FF + lo:D_FF + hi])
        act_ref[:, lo:hi] = (gate * jax.nn.sigmoid(gate) * up).astype(BF16)
    ffn = _dot(act_ref[...], wout_ref[...])
    y_ref[...] = x1 + gate_f * _rms(ffn, gof_ref[...])


def _post_ffn(x, o, mod, g_post_mix, g_pre_ffn, g_post_ffn, wo, win, wout, n_tiles):
    vec = _const_spec((1, D_MODEL))
    return pl.pallas_call(
        _post_ffn_kernel,
        grid=(BATCH, n_tiles),
        in_specs=[_x_spec(), _x_spec(), _mod_spec(), vec, vec, vec,
                  _const_spec(wo.shape), _const_spec(win.shape), _const_spec(wout.shape)],
        out_specs=_x_spec(),
        out_shape=jax.ShapeDtypeStruct((BATCH, n_tiles * TILE, D_MODEL), F32),
        scratch_shapes=[pltpu.VMEM((TILE, D_FF), BF16)],
        compiler_params=_cparams(2),
        name="post_ffn",
    )(x, o, mod, g_post_mix, g_pre_ffn, g_post_ffn, wo, win, wout)


def _rope_tables():
    t = jnp.arange(SEQ)
    row = (t // GRID_W).astype(F32)
    col = (t % GRID_W).astype(F32)
    n_freq = MLA_ROPE // 4
    inv = ROPE_THETA ** (-jnp.arange(n_freq, dtype=F32) / n_freq)
    ang = jnp.concatenate([row[:, None] * inv, col[:, None] * inv], axis=-1)
    cos, sin = jnp.cos(ang), jnp.sin(ang)
    cos_t = jnp.concatenate([cos, cos, cos, cos], axis=-1)
    sin_t = jnp.concatenate([-sin, sin, -sin, sin], axis=-1)
    cos_t = jnp.concatenate([cos_t, jnp.ones((CTX_LEN, LANES), F32)], axis=0)
    sin_t = jnp.concatenate([sin_t, jnp.zeros((CTX_LEN, LANES), F32)], axis=0)
    return cos_t, sin_t


NA_ROWS_PER_TILE = TILE // GRID_W
NA_BIAS_ROWS = 2 * NA_WIN_ROWS - 1
NA_BIAS_COLS = 2 * NA_WIN_COLS - 1
NA_NB_KEYS = NA_NB_BLOCKS * NA_NB_TOKENS


def _na_table_kernel(u_ref, o_ref):
    rows = SEQ // GRID_W
    kc = lax.broadcasted_iota(jnp.int32, (GRID_W, LANES), 0)
    lane = lax.broadcasted_iota(jnp.int32, (GRID_W, LANES), 1)
    qc = lane % GRID_W
    col_start = jnp.clip(qc - NA_WIN_COLS // 2, 0, GRID_W - NA_WIN_COLS)
    col_ok = (kc >= col_start) & (kc < col_start + NA_WIN_COLS)
    first_half = lane < GRID_W
    neg = jnp.full((GRID_W, LANES), NEG_INF, F32)
    toeplitz = {}
    for i in range(1, NA_BIAS_ROWS):
        row = jnp.broadcast_to(u_ref[i:i + 1, :], (GRID_W, LANES))
        toeplitz[i] = pltpu.roll(row, LANES - (NA_WIN_COLS - 1), 1, stride=1,
                                 stride_axis=0) * LOG2E
    for v, tile in enumerate((0, 1, N_LAT_TILES - 1)):
        for kl in range(NA_NB_BLOCKS * NA_ROWS_PER_TILE):
            kr = NA_ROWS_PER_TILE * (tile - 1) + kl
            for qp in range(NA_ROWS_PER_TILE // 2):
                oks = []
                for half in range(2):
                    qr = NA_ROWS_PER_TILE * tile + 2 * qp + half
                    row_start = min(max(qr - NA_WIN_ROWS // 2, 0), rows - NA_WIN_ROWS)
                    oks.append(row_start <= kr < row_start + NA_WIN_ROWS)
                if oks[0] and oks[1]:
                    ok = col_ok
                elif oks[0]:
                    ok = col_ok & first_half
                elif oks[1]:
                    ok = col_ok & jnp.logical_not(first_half)
                else:
                    ok = None
                i = kl - NA_ROWS_PER_TILE - 2 * qp + NA_WIN_ROWS - 1
                blk = neg if ok is None else jnp.where(ok, toeplitz[i], neg)
                o_ref[v, kl * GRID_W:(kl + 1) * GRID_W, qp * LANES:(qp + 1) * LANES] = blk
    o_ref[N_TABLE_VARIANTS - 1] = jnp.full((NA_NB_KEYS, TILE), NEG_INF, F32)


def _na_table(rpb):
    pad = jnp.full((NA_HEADS, NA_BIAS_ROWS, GRID_W - NA_BIAS_COLS), NEG_INF, F32)
    rev = jnp.concatenate([rpb.astype(F32)[:, :, ::-1], pad], axis=-1)
    off = jnp.full((NA_HEADS, 1, GRID_W), NEG_INF, F32)
    left = jnp.concatenate([rev, off], axis=1)
    right = jnp.concatenate([off, rev], axis=1)
    u = jnp.concatenate([left, right], axis=-1)
    return pl.pallas_call(
        _na_table_kernel,
        grid=(NA_HEADS,),
        in_specs=[pl.BlockSpec((None, NA_BIAS_ROWS + 1, LANES), lambda h: (h, 0, 0))],
        out_specs=pl.BlockSpec((N_TABLE_VARIANTS, None, NA_NB_KEYS, TILE),
                               lambda h: (0, h, 0, 0)),
        out_shape=jax.ShapeDtypeStruct((N_TABLE_VARIANTS, NA_HEADS, NA_NB_KEYS, TILE), F32),
        compiler_params=_cparams(1),
        name="na_table",
    )(u)


def _swa_table():
    j = jnp.array([0, 1, N_LAT_TILES - 1], jnp.int32)[:, None, None]
    qpos = TILE * j + jnp.arange(TILE)[None, None, :]
    kpos = TILE * j - SWA_NB_TOKENS + jnp.arange(SWA_NB_BLOCKS * SWA_NB_TOKENS)[None, :, None]
    ok = (jnp.abs(kpos - qpos) <= SWA_WINDOW) & (kpos >= 0) & (kpos < SEQ)
    tbl = jnp.where(ok, 0.0, NEG_INF).astype(F32)
    tbl = jnp.concatenate([tbl, jnp.full((1,) + tbl.shape[1:], NEG_INF, F32)], axis=0)
    return tbl[:, None]


def _swap_halves(w):
    half = w.shape[-1] // 2
    return jnp.concatenate([w[..., half:], w[..., :half]], axis=-1)


def _mla_weights(w_in, w_uq, w_ukv):
    lat = MLA_Q_LORA + MLA_KV_LORA
    kpe = w_in[:, lat:]
    z = jnp.zeros((D_MODEL, LANES - MLA_ROPE), w_in.dtype)
    w_in_x = jnp.concatenate([w_in[:, :lat], kpe, z, _swap_halves(kpe), z], axis=1)
    uq = w_uq.reshape(MLA_Q_LORA, MLA_HEADS, MLA_NOPE + MLA_ROPE)
    rope = uq[:, :, MLA_NOPE:]
    zq = jnp.zeros((MLA_Q_LORA, MLA_HEADS, LANES - MLA_ROPE), w_uq.dtype)
    flat = lambda a: a.reshape(a.shape[0], -1)
    wq_x = jnp.concatenate([flat(uq[:, :, :MLA_NOPE]),
                            flat(jnp.concatenate([rope, zq], axis=-1)),
                            flat(jnp.concatenate([_swap_halves(rope), zq], axis=-1))], axis=1)
    ukv = w_ukv.reshape(MLA_KV_LORA, MLA_HEADS, MLA_NOPE + MLA_V)
    wkv_x = jnp.concatenate([flat(ukv[:, :, :MLA_NOPE]), flat(ukv[:, :, MLA_NOPE:])], axis=1)
    return w_in_x.astype(BF16), wq_x.astype(BF16), wkv_x.astype(BF16)


def _swa_weights(w_qkv):
    hd = SWA_HEADS * SWA_HEAD_DIM
    kd = SWA_KV_HEADS * SWA_HEAD_DIM
    wq = w_qkv[:, :hd].reshape(D_MODEL, SWA_HEADS, SWA_HEAD_DIM)
    wk = w_qkv[:, hd:hd + kd].reshape(D_MODEL, SWA_KV_HEADS, 1, SWA_HEAD_DIM)
    wv = w_qkv[:, hd + kd:].reshape(D_MODEL, SWA_KV_HEADS, 1, SWA_HEAD_DIM)
    dup = lambda a: jnp.broadcast_to(a, (D_MODEL, SWA_KV_HEADS, 2, SWA_HEAD_DIM)).reshape(
        D_MODEL, -1)
    flat = lambda a: a.reshape(D_MODEL, -1)
    w = jnp.concatenate([flat(wq), flat(_swap_halves(wq)), dup(wk), dup(_swap_halves(wk)),
                         dup(wv)], axis=1)
    return w.astype(BF16)


def kernel(x, c, ctx, c_ctx, w_mod, b_mod, g_pre_mix, g_post_mix, g_pre_ffn, g_post_ffn,
           w_ffn_in, w_ffn_out, mla_w_in, mla_g_q, mla_w_uq, mla_g_kv, mla_w_ukv, mla_w_o,
           na_w_qkv, na_rpb, na_w_o, swa_w_qkv, swa_sink, swa_w_o):
    assert x.shape == (BATCH, SEQ, D_MODEL) and ctx.shape == (BATCH, CTX_LEN, D_MODEL)
    stream = jnp.concatenate([x, ctx], axis=1)
    cc = jnp.concatenate([c, c_ctx[None], jnp.zeros((8 - BATCH - 1, D_MODEL), F32)], axis=0)
    mods = _modulation(cc, w_mod, b_mod)
    mods = mods.reshape(DEPTH, 8, 6, D_MODEL)
    mods = jnp.stack([mods[:, :BATCH],
                      jnp.broadcast_to(mods[:, BATCH:BATCH + 1], (DEPTH, BATCH, 6, D_MODEL))],
                     axis=2)
    cos_t, sin_t = _rope_tables()
    row = lambda g: g.reshape(1, -1)

    for i in range(DEPTH):
        last = i == DEPTH - 1
        kind, j = i % 3, i // 3
        mod = mods[i]
        g_pre = row(g_pre_mix[i])
        if kind == 0:
            w_in_x, wq_x, wkv_x = _mla_weights(mla_w_in[j], mla_w_uq[j], mla_w_ukv[j])
            q, k, vt = _mla_proj(stream, mod, g_pre, cos_t, sin_t, w_in_x, row(mla_g_q[j]),
                                 row(mla_g_kv[j]), wq_x, wkv_x)
            o = _mla_attention(q, k, vt)
            w_o = mla_w_o[j]
        elif kind == 1:
            qk, vt = _na_proj(stream, mod, g_pre, na_w_qkv[j].astype(BF16))
            o = _local_attention(qk, 0, qk, 1, NA_HEADS * NA_HEAD_DIM, vt, NA_NB_TOKENS,
                                 NA_NB_BLOCKS, _na_table(na_rpb[j]), None, lambda p: p,
                                 "na_attn")
            w_o = na_w_o[j]
        else:
            q, k, vt = _swa_proj(stream, mod, g_pre, cos_t, sin_t, _swa_weights(swa_w_qkv[j]))
            sink = jnp.broadcast_to((swa_sink[j].astype(F32) * LOG2E)[:, None],
                                    (SWA_HEADS, TILE))
            o = _local_attention(q, 0, k, 0, 2 * LANES, vt, SWA_NB_TOKENS, SWA_NB_BLOCKS,
                                 _swa_table(), sink,
                                 lambda p: p // (HEAD_PAIRS // SWA_KV_HEADS), "swa_attn")
            w_o = swa_w_o[j]
        stream = _post_ffn(stream, o, mod, row(g_post_mix[i]), row(g_pre_ffn[i]),
                           row(g_post_ffn[i]), w_o.astype(BF16), w_ffn_in[i].astype(BF16),
                           w_ffn_out[i].astype(BF16), N_LAT_TILES if last else N_TILES)
    return stream
```

```python
import functools

import jax
import jax.numpy as jnp
from jax import lax
from jax.experimental import pallas as pl
from jax.experimental.pallas import tpu as pltpu

D_MODEL = 1024
BATCH = 4
SEQ = 8192
DEPTH = 4
GRID_W = 64
CTX_LEN = 256
D_FF = 2816
NORM_EPS = 1e-6
ROPE_THETA = 10000.0
NEG_INF = -1e30

MLA_HEADS = 8
MLA_Q_LORA = 256
MLA_KV_LORA = 256
MLA_NOPE = 128
MLA_ROPE = 64
MLA_V = 128
MLA_QK_PAD = 256

NA_HEADS = 16
NA_HEAD_DIM = 64
NA_WIN_ROWS = 8
NA_WIN_COLS = 16

SWA_HEADS = 16
SWA_KV_HEADS = 2
SWA_HEAD_DIM = 64
SWA_WINDOW = 128

TOKENS = SEQ + CTX_LEN
TILE = 256
N_TILES = TOKENS // TILE
N_LAT_TILES = SEQ // TILE
LANES = 128
BF16_ROWS = 16
HEAD_PAIRS = 8
N_TABLE_VARIANTS = 4

MLA_TQ = 512
MLA_TK = 2048
MLA_HEADROOM_LOG2 = 64.0
MLA_MIN_ROW_SUM = 2.0 ** -40
MLA_BOUND_SLACK = 1.02

NA_NB_TOKENS = TILE
NA_NB_BLOCKS = 3
SWA_NB_TOKENS = 128
SWA_NB_BLOCKS = 4

VMEM_LIMIT = 56 * 1024 * 1024

LOG2E = 1.4426950408889634

BF16 = jnp.bfloat16
F32 = jnp.float32


def _cparams(n_axes):
    return pltpu.CompilerParams(
        dimension_semantics=("parallel",) * n_axes, vmem_limit_bytes=VMEM_LIMIT)


def _const_spec(shape):
    nd = len(shape)
    return pl.BlockSpec(shape, lambda *_: (0,) * nd, pipeline_mode=pl.Buffered(1))


def _rms(x, g):
    return x * lax.rsqrt(jnp.mean(x * x, axis=-1, keepdims=True) + NORM_EPS) * g


def _dot(a, b):
    return jnp.dot(a, b, preferred_element_type=F32)


def _dot_nt(a, b):
    return lax.dot_general(a, b, (((1,), (1,)), ((), ())), preferred_element_type=F32)


def _ones_rows(n):
    row = lax.broadcasted_iota(jnp.int32, (BF16_ROWS, n), 0)
    return jnp.where(row == 0, 1.0, 0.0).astype(BF16)


def _mod_kernel(c_ref, w_ref, b_ref, o_ref):
    c = c_ref[...]
    a = c * jax.nn.sigmoid(c)
    o_ref[...] = _dot(a, w_ref[...]) + b_ref[...]


def _modulation(cc, w_mod, b_mod):
    tn = 1536
    return pl.pallas_call(
        _mod_kernel,
        grid=(DEPTH, 6 * D_MODEL // tn),
        in_specs=[
            pl.BlockSpec((8, D_MODEL), lambda l, n: (0, 0)),
            pl.BlockSpec((None, D_MODEL, tn), lambda l, n: (l, 0, n)),
            pl.BlockSpec((None, 1, tn), lambda l, n: (l, 0, n)),
        ],
        out_specs=pl.BlockSpec((None, 8, tn), lambda l, n: (l, 0, n)),
        out_shape=jax.ShapeDtypeStruct((DEPTH, 8, 6 * D_MODEL), F32),
        compiler_params=_cparams(2),
        name="modulation",
    )(cc, w_mod, b_mod.reshape(DEPTH, 1, 6 * D_MODEL))


def _mod_spec():
    return pl.BlockSpec((None, None, 6, D_MODEL),
                        lambda b, j: (b, (j == N_LAT_TILES).astype(jnp.int32), 0, 0))


def _x_spec():
    return pl.BlockSpec((None, TILE, D_MODEL), lambda b, j: (b, j, 0))


def _rope_spec():
    return pl.BlockSpec((TILE, LANES), lambda b, j: (j, 0))


def _pre_mix(x_ref, mod_ref, g_ref):
    shift = mod_ref[0:1, :]
    scale = mod_ref[1:2, :]
    h = _rms(x_ref[...], g_ref[...]) * (1.0 + scale) + shift
    return h.astype(BF16)


def _mla_proj_kernel(x_ref, mod_ref, g_ref, cos_ref, sin_ref, w_in_ref, gq_ref, gkv_ref,
                     wq_ref, wkv_ref, q_ref, k_ref, vt_ref, ksq_ref, *, scale):
    h = _pre_mix(x_ref, mod_ref, g_ref)
    a = _dot(h, w_in_ref[...])
    qn = _rms(a[:, :MLA_Q_LORA], gq_ref[...]).astype(BF16)
    kvn = _rms(a[:, MLA_Q_LORA:MLA_Q_LORA + MLA_KV_LORA], gkv_ref[...]).astype(BF16)
    cos = cos_ref[...]
    sin = sin_ref[...]
    k_rope = (a[:, 512:640] * cos + a[:, 640:768] * sin).astype(BF16)
    qq = _dot(qn, wq_ref[...])
    kv = _dot(kvn, wkv_ref[...])
    hn = MLA_HEADS * LANES
    k_rope_f = k_rope.astype(F32)
    rope_sq = jnp.sum(k_rope_f * k_rope_f, axis=-1, keepdims=True)
    for hd in range(MLA_HEADS):
        lo, hi = hd * LANES, (hd + 1) * LANES
        base = hd * MLA_QK_PAD
        q_ref[:, base:base + LANES] = (qq[:, lo:hi] * scale).astype(BF16)
        q_rope = qq[:, hn + lo:hn + hi] * cos + qq[:, 2 * hn + lo:2 * hn + hi] * sin
        q_ref[:, base + LANES:base + 2 * LANES] = (q_rope * scale).astype(BF16)
        k_nope = kv[:, lo:hi].astype(BF16)
        k_ref[:, base:base + LANES] = k_nope
        k_ref[:, base + LANES:base + 2 * LANES] = k_rope
        k_nope_f = k_nope.astype(F32)
        k_sq = jnp.sum(k_nope_f * k_nope_f, axis=-1, keepdims=True) + rope_sq
        ksq_ref[hd:hd + 1, :] = jnp.broadcast_to(jnp.max(k_sq, axis=0, keepdims=True),
                                                 (1, LANES))
    vt_ref[...] = kv[:, hn:].T.astype(BF16)


def _mla_proj(x, mod, g_pre, cos, sin, w_in, g_q, g_kv, wq, wkv):
    hq = MLA_HEADS * MLA_QK_PAD
    hv = MLA_HEADS * MLA_V
    out_spec_q = pl.BlockSpec((None, TILE, hq), lambda b, j: (b, j, 0))
    out_spec_v = pl.BlockSpec((None, hv, TILE), lambda b, j: (b, 0, j))
    scale = float((MLA_NOPE + MLA_ROPE) ** -0.5) * LOG2E
    return pl.pallas_call(
        functools.partial(_mla_proj_kernel, scale=scale),
        grid=(BATCH, N_TILES),
        in_specs=[_x_spec(), _mod_spec(), _const_spec((1, D_MODEL)), _rope_spec(), _rope_spec(),
                  _const_spec(w_in.shape), _const_spec((1, MLA_Q_LORA)),
                  _const_spec((1, MLA_KV_LORA)), _const_spec(wq.shape), _const_spec(wkv.shape)],
        out_specs=[out_spec_q, out_spec_q, out_spec_v,
                   pl.BlockSpec((None, None, MLA_HEADS, LANES), lambda b, j: (b, j, 0, 0))],
        out_shape=[jax.ShapeDtypeStruct((BATCH, TOKENS, hq), BF16),
                   jax.ShapeDtypeStruct((BATCH, TOKENS, hq), BF16),
                   jax.ShapeDtypeStruct((BATCH, hv, TOKENS), BF16),
                   jax.ShapeDtypeStruct((BATCH, N_TILES, MLA_HEADS, LANES), F32)],
        compiler_params=_cparams(2),
        name="mla_proj",
    )(x, mod, g_pre, cos, sin, w_in, g_q, g_kv, wq, wkv)


def _softmax_update(s, vt, carry):
    m, l, acc = carry
    m_new = jnp.maximum(m, jnp.max(s, axis=0, keepdims=True))
    alpha = jnp.exp2(m - m_new)
    p = jnp.exp2(s - m_new)
    l = alpha * l + jnp.sum(p, axis=0, keepdims=True)
    acc = alpha * acc + _dot(vt, p.astype(BF16))
    return m_new, l, acc


def _mla_attn_kernel(q_ref, k_ref, vt_ref, ksq_ref, o_ref, *, chunks):
    q = q_ref[...]
    tq = q.shape[0]

    def logits(chunk):
        start, size = chunk
        return _dot_nt(k_ref[start:start + size, :], q)

    def single_pass(r):
        l = jnp.zeros((1, tq), F32)
        acc = jnp.zeros((MLA_V, tq), F32)
        s = logits(chunks[0])
        for c, (start, size) in enumerate(chunks):
            s_next = logits(chunks[c + 1]) if c + 1 < len(chunks) else None
            p = jnp.exp2(s - r)
            l = l + jnp.sum(p, axis=0, keepdims=True)
            acc = acc + _dot(vt_ref[:, start:start + size], p.astype(BF16))
            s = s_next
        return l, acc

    def online():
        carry = (jnp.full((1, tq), NEG_INF, F32), jnp.zeros((1, tq), F32),
                 jnp.zeros((MLA_V, tq), F32))
        s = logits(chunks[0])
        for c, (start, size) in enumerate(chunks):
            s_next = logits(chunks[c + 1]) if c + 1 < len(chunks) else None
            carry = _softmax_update(s, vt_ref[:, start:start + size], carry)
            s = s_next
        return carry[1], carry[2]

    if ksq_ref is None:
        l, acc = online()
    else:
        qf = q.astype(F32)
        ones = jnp.ones((8, q.shape[1]), BF16)
        q_sq = _dot_nt(ones, (qf * qf).astype(BF16))[0:1]
        k_sq = jnp.max(ksq_ref[...], axis=0, keepdims=True)[:, 0:1]
        bound = jnp.sqrt(q_sq * k_sq) * MLA_BOUND_SLACK
        l, acc = single_pass(bound - MLA_HEADROOM_LOG2)
        l, acc = lax.cond(jnp.min(l) >= MLA_MIN_ROW_SUM, lambda: (l, acc), online)
    o_ref[...] = (acc / l).T.astype(BF16)


def _mla_attn_ctx_kernel(q_ref, k_ref, vt_ref, latent_out_ref, o_ref):
    del latent_out_ref
    _mla_attn_kernel(q_ref, k_ref, vt_ref, None, o_ref, chunks=((0, CTX_LEN),))


def _mla_attention(q, k, vt, ksq):
    hv = MLA_HEADS * MLA_V
    chunks = ((SEQ, CTX_LEN),) + tuple((c * MLA_TK, MLA_TK) for c in range(SEQ // MLA_TK))
    o = pl.pallas_call(
        functools.partial(_mla_attn_kernel, chunks=chunks),
        grid=(BATCH, MLA_HEADS, SEQ // MLA_TQ),
        in_specs=[
            pl.BlockSpec((None, MLA_TQ, MLA_QK_PAD), lambda b, h, i: (b, i, h)),
            pl.BlockSpec((None, TOKENS, MLA_QK_PAD), lambda b, h, i: (b, 0, h)),
            pl.BlockSpec((None, MLA_V, TOKENS), lambda b, h, i: (b, h, 0)),
            pl.BlockSpec((None, None, N_TILES, LANES), lambda b, h, i: (b, h, 0, 0)),
        ],
        out_specs=pl.BlockSpec((None, MLA_TQ, MLA_V), lambda b, h, i: (b, i, h)),
        out_shape=jax.ShapeDtypeStruct((BATCH, TOKENS, hv), BF16),
        compiler_params=_cparams(3),
        name="mla_attn",
    )(q, k, vt, ksq)
    ctx_blk = SEQ // CTX_LEN
    return pl.pallas_call(
        _mla_attn_ctx_kernel,
        grid=(BATCH, MLA_HEADS),
        in_specs=[
            pl.BlockSpec((None, CTX_LEN, MLA_QK_PAD), lambda b, h: (b, ctx_blk, h)),
            pl.BlockSpec((None, CTX_LEN, MLA_QK_PAD), lambda b, h: (b, ctx_blk, h)),
            pl.BlockSpec((None, MLA_V, CTX_LEN), lambda b, h: (b, h, ctx_blk)),
            pl.BlockSpec(memory_space=pl.ANY),
        ],
        out_specs=pl.BlockSpec((None, CTX_LEN, MLA_V), lambda b, h: (b, ctx_blk, h)),
        out_shape=jax.ShapeDtypeStruct((BATCH, TOKENS, hv), BF16),
        input_output_aliases={3: 0},
        compiler_params=_cparams(2),
        name="mla_attn_ctx",
    )(q, k, vt, o)


def _na_proj_kernel(x_ref, mod_ref, g_ref, w_ref, qk_ref, vt_ref, *, scale):
    h = _pre_mix(x_ref, mod_ref, g_ref)
    a = _dot(h, w_ref[...])
    hd = NA_HEADS * NA_HEAD_DIM
    qk_ref[:, :hd] = (a[:, :hd] * scale).astype(BF16)
    qk_ref[:, hd:] = a[:, hd:2 * hd].astype(BF16)
    vt_ref[...] = a[:, 2 * hd:].T.astype(BF16)


def _na_proj(x, mod, g_pre, w):
    hd = NA_HEADS * NA_HEAD_DIM
    return pl.pallas_call(
        functools.partial(_na_proj_kernel, scale=float(NA_HEAD_DIM ** -0.5) * LOG2E),
        grid=(BATCH, N_TILES),
        in_specs=[_x_spec(), _mod_spec(), _const_spec((1, D_MODEL)), _const_spec(w.shape)],
        out_specs=[pl.BlockSpec((None, TILE, 2 * hd), lambda b, j: (b, j, 0)),
                   pl.BlockSpec((None, hd, TILE), lambda b, j: (b, 0, j))],
        out_shape=[jax.ShapeDtypeStruct((BATCH, TOKENS, 2 * hd), BF16),
                   jax.ShapeDtypeStruct((BATCH, hd, TOKENS), BF16)],
        compiler_params=_cparams(2),
        name="na_proj",
    )(x, mod, g_pre, w)


def _swa_proj_kernel(x_ref, mod_ref, g_ref, cos_ref, sin_ref, w_ref, q_ref, k_ref, vt_ref, *,
                     scale):
    h = _pre_mix(x_ref, mod_ref, g_ref)
    a = _dot(h, w_ref[...])
    cos = cos_ref[...]
    sin = sin_ref[...]
    hd = SWA_HEADS * SWA_HEAD_DIM
    for p in range(HEAD_PAIRS):
        lo, hi = p * LANES, (p + 1) * LANES
        q = a[:, lo:hi] * cos + a[:, hd + lo:hd + hi] * sin
        q_ref[:, lo:hi] = (q * scale).astype(BF16)
    kw = 2 * LANES
    for p in range(2):
        lo, hi = p * LANES, (p + 1) * LANES
        k = a[:, 2 * hd + lo:2 * hd + hi] * cos + a[:, 2 * hd + kw + lo:2 * hd + kw + hi] * sin
        k_ref[:, lo:hi] = k.astype(BF16)
    vt_ref[...] = a[:, 2 * hd + 2 * kw:].T.astype(BF16)


def _swa_proj(x, mod, g_pre, cos, sin, w):
    hd = SWA_HEADS * SWA_HEAD_DIM
    kw = 2 * LANES
    return pl.pallas_call(
        functools.partial(_swa_proj_kernel, scale=float(SWA_HEAD_DIM ** -0.5) * LOG2E),
        grid=(BATCH, N_TILES),
        in_specs=[_x_spec(), _mod_spec(), _const_spec((1, D_MODEL)), _rope_spec(), _rope_spec(),
                  _const_spec(w.shape)],
        out_specs=[pl.BlockSpec((None, TILE, hd), lambda b, j: (b, j, 0)),
                   pl.BlockSpec((None, TILE, kw), lambda b, j: (b, j, 0)),
                   pl.BlockSpec((None, kw, TILE), lambda b, j: (b, 0, j))],
        out_shape=[jax.ShapeDtypeStruct((BATCH, TOKENS, hd), BF16),
                   jax.ShapeDtypeStruct((BATCH, TOKENS, kw), BF16),
                   jax.ShapeDtypeStruct((BATCH, kw, TOKENS), BF16)],
        compiler_params=_cparams(2),
        name="swa_proj",
    )(x, mod, g_pre, cos, sin, w)


def _local_attn_kernel(*refs, n_nb, kv_slot, per_head_table, has_sink):
    q_ref = refs[0]
    k_refs = refs[1:1 + n_nb]
    kc_ref = refs[1 + n_nb]
    vt_refs = refs[2 + n_nb:2 + 2 * n_nb]
    vtc_ref = refs[2 + 2 * n_nb]
    tbl_ref, sink_ref, o_ref = refs[3 + 2 * n_nb:]
    n_heads = 2 * HEAD_PAIRS
    hdim = NA_HEAD_DIM
    low = lax.broadcasted_iota(jnp.int32, (TILE, LANES), 1) < hdim

    def logits(hd):
        p, half = divmod(hd, 2)
        qp = q_ref[:, p * LANES:(p + 1) * LANES]
        qm = jnp.where(low if half == 0 else jnp.logical_not(low), qp, jnp.zeros_like(qp))
        ks = slice(kv_slot(p) * LANES, (kv_slot(p) + 1) * LANES)
        kn = jnp.concatenate([r[:, ks] for r in k_refs], axis=0)
        s_n = _dot_nt(kn, qm) + tbl_ref[hd if per_head_table else 0]
        s_c = _dot_nt(kc_ref[:, ks], qm)
        return s_n, s_c

    s = logits(0)
    pair_out = []
    for hd in range(n_heads):
        s_next = logits(hd + 1) if hd + 1 < n_heads else None
        p, half = divmod(hd, 2)
        s_n, s_c = s
        m = jnp.maximum(jnp.max(s_n, axis=0, keepdims=True), jnp.max(s_c, axis=0, keepdims=True))
        if has_sink:
            sink = sink_ref[hd:hd + 1, :]
            m = jnp.maximum(m, sink)
        p_n = jnp.exp2(s_n - m).astype(BF16)
        p_c = jnp.exp2(s_c - m).astype(BF16)
        lo = kv_slot(p) * LANES + half * hdim
        vt_n = jnp.concatenate([r[lo:lo + hdim, :] for r in vt_refs], axis=1)
        vt_n = jnp.concatenate([vt_n, _ones_rows(vt_n.shape[1])], axis=0)
        vt_c = jnp.concatenate([vtc_ref[lo:lo + hdim, :], _ones_rows(CTX_LEN)], axis=0)
        acc = _dot(vt_n, p_n) + _dot(vt_c, p_c)
        l = acc[hdim:hdim + 1]
        if has_sink:
            l = l + jnp.exp2(sink - m)
        pair_out.append(acc[:hdim] / l)
        if half == 1:
            pair = jnp.concatenate(pair_out, axis=0)
            o_ref[:, p * LANES:(p + 1) * LANES] = pair.T.astype(BF16)
            pair_out = []
        s = s_next


def _local_attention(q, q_col, k, k_col, k_width, vt, nb_tokens, n_nb, table, sink, kv_slot,
                     name):
    hd = NA_HEADS * NA_HEAD_DIM
    per_head_table = table.shape[1] > 1
    lat_hi = N_LAT_TILES - 1
    per_tile = TILE // nb_tokens
    nb_hi = SEQ // nb_tokens - 1

    def nb_index(j, n):
        return jnp.clip(j * per_tile - 1 + n, 0, nb_hi)

    def k_spec(n):
        return pl.BlockSpec((None, nb_tokens, k_width), lambda b, j: (b, nb_index(j, n), k_col))

    def vt_spec(n):
        return pl.BlockSpec((None, k_width, nb_tokens), lambda b, j: (b, 0, nb_index(j, n)))

    def variant(j):
        return jnp.where(j == 0, 0, jnp.where(j == lat_hi, 2, jnp.where(j > lat_hi, 3, 1)))

    tbl_spec = pl.BlockSpec((None,) + table.shape[1:], lambda b, j: (variant(j), 0, 0, 0))
    in_specs = ([pl.BlockSpec((None, TILE, hd), lambda b, j: (b, j, q_col))]
                + [k_spec(n) for n in range(n_nb)]
                + [pl.BlockSpec((None, CTX_LEN, k_width), lambda b, j: (b, N_LAT_TILES, k_col))]
                + [vt_spec(n) for n in range(n_nb)]
                + [pl.BlockSpec((None, k_width, CTX_LEN), lambda b, j: (b, 0, N_LAT_TILES))]
                + [tbl_spec, _const_spec((NA_HEADS, TILE))])
    sink_arr = jnp.zeros((NA_HEADS, TILE), F32) if sink is None else sink
    return pl.pallas_call(
        functools.partial(_local_attn_kernel, n_nb=n_nb, kv_slot=kv_slot,
                          per_head_table=per_head_table, has_sink=sink is not None),
        grid=(BATCH, N_TILES),
        in_specs=in_specs,
        out_specs=pl.BlockSpec((None, TILE, hd), lambda b, j: (b, j, 0)),
        out_shape=jax.ShapeDtypeStruct((BATCH, TOKENS, hd), BF16),
        compiler_params=_cparams(2),
        name=name,
    )(q, *([k] * (n_nb + 1)), *([vt] * (n_nb + 1)), table, sink_arr)


def _post_ffn_kernel(x_ref, o_ref, mod_ref, gpm_ref, gpf_ref, gof_ref, wo_ref, win_ref,
                     wout_ref, y_ref, act_ref):
    gate_m = mod_ref[2:3, :]
    shift_f = mod_ref[3:4, :]
    scale_f = mod_ref[4:5, :]
    gate_f = mod_ref[5:6, :]
    mixed = _dot(o_ref[...], wo_ref[...])
    x1 = x_ref[...] + gate_m * _rms(mixed, gpm_ref[...])
    h = (_rms(x1, gpf_ref[...]) * (1.0 + scale_f) + shift_f).astype(BF16)
    chunk = 256
    for f in range(D_FF // chunk):
        lo, hi = f * chunk, (f + 1) * chunk
        gate = _dot(h, win_ref[:, lo:hi])
        up = _dot(h, win_ref[:, D_FF + lo:D_FF + hi])
        act_ref[:, lo:hi] = (gate * jax.nn.sigmoid(gate) * up).astype(BF16)
    ffn = _dot(act_ref[...], wout_ref[...])
    y_ref[...] = x1 + gate_f * _rms(ffn, gof_ref[...])


def _post_ffn(x, o, mod, g_post_mix, g_pre_ffn, g_post_ffn, wo, win, wout, n_tiles):
    vec = _const_spec((1, D_MODEL))
    return pl.pallas_call(
        _post_ffn_kernel,
        grid=(BATCH, n_tiles),
        in_specs=[_x_spec(), _x_spec(), _mod_spec(), vec, vec, vec,
                  _const_spec(wo.shape), _const_spec(win.shape), _const_spec(wout.shape)],
        out_specs=_x_spec(),
        out_shape=jax.ShapeDtypeStruct((BATCH, n_tiles * TILE, D_MODEL), F32),
        scratch_shapes=[pltpu.VMEM((TILE, D_FF), BF16)],
        compiler_params=_cparams(2),
        name="post_ffn",
    )(x, o, mod, g_post_mix, g_pre_ffn, g_post_ffn, wo, win, wout)


def _rope_tables():
    t = jnp.arange(SEQ)
    row = (t // GRID_W).astype(F32)
    col = (t % GRID_W).astype(F32)
    n_freq = MLA_ROPE // 4
    inv = ROPE_THETA ** (-jnp.arange(n_freq, dtype=F32) / n_freq)
    ang = jnp.concatenate([row[:, None] * inv, col[:, None] * inv], axis=-1)
    cos, sin = jnp.cos(ang), jnp.sin(ang)
    cos_t = jnp.concatenate([cos, cos, cos, cos], axis=-1)
    sin_t = jnp.concatenate([-sin, sin, -sin, sin], axis=-1)
    cos_t = jnp.concatenate([cos_t, jnp.ones((CTX_LEN, LANES), F32)], axis=0)
    sin_t = jnp.concatenate([sin_t, jnp.zeros((CTX_LEN, LANES), F32)], axis=0)
    return cos_t, sin_t


NA_ROWS_PER_TILE = TILE // GRID_W
NA_BIAS_ROWS = 2 * NA_WIN_ROWS - 1
NA_BIAS_COLS = 2 * NA_WIN_COLS - 1
NA_NB_KEYS = NA_NB_BLOCKS * NA_NB_TOKENS


def _na_table_kernel(u_ref, o_ref):
    rows = SEQ // GRID_W
    kc = lax.broadcasted_iota(jnp.int32, (GRID_W, LANES), 0)
    lane = lax.broadcasted_iota(jnp.int32, (GRID_W, LANES), 1)
    qc = lane % GRID_W
    col_start = jnp.clip(qc - NA_WIN_COLS // 2, 0, GRID_W - NA_WIN_COLS)
    col_ok = (kc >= col_start) & (kc < col_start + NA_WIN_COLS)
    first_half = lane < GRID_W
    neg = jnp.full((GRID_W, LANES), NEG_INF, F32)
    toeplitz = {}
    for i in range(1, NA_BIAS_ROWS):
        row = jnp.broadcast_to(u_ref[i:i + 1, :], (GRID_W, LANES))
        toeplitz[i] = pltpu.roll(row, LANES - (NA_WIN_COLS - 1), 1, stride=1,
                                 stride_axis=0) * LOG2E
    for v, tile in enumerate((0, 1, N_LAT_TILES - 1)):
        for kl in range(NA_NB_BLOCKS * NA_ROWS_PER_TILE):
            kr = NA_ROWS_PER_TILE * (tile - 1) + kl
            for qp in range(NA_ROWS_PER_TILE // 2):
                oks = []
                for half in range(2):
                    qr = NA_ROWS_PER_TILE * tile + 2 * qp + half
                    row_start = min(max(qr - NA_WIN_ROWS // 2, 0), rows - NA_WIN_ROWS)
                    oks.append(row_start <= kr < row_start + NA_WIN_ROWS)
                if oks[0] and oks[1]:
                    ok = col_ok
                elif oks[0]:
                    ok = col_ok & first_half
                elif oks[1]:
                    ok = col_ok & jnp.logical_not(first_half)
                else:
                    ok = None
                i = kl - NA_ROWS_PER_TILE - 2 * qp + NA_WIN_ROWS - 1
                blk = neg if ok is None else jnp.where(ok, toeplitz[i], neg)
                o_ref[v, kl * GRID_W:(kl + 1) * GRID_W, qp * LANES:(qp + 1) * LANES] = blk
    o_ref[N_TABLE_VARIANTS - 1] = jnp.full((NA_NB_KEYS, TILE), NEG_INF, F32)


def _na_table(rpb):
    pad = jnp.full((NA_HEADS, NA_BIAS_ROWS, GRID_W - NA_BIAS_COLS), NEG_INF, F32)
    rev = jnp.concatenate([rpb.astype(F32)[:, :, ::-1], pad], axis=-1)
    off = jnp.full((NA_HEADS, 1, GRID_W), NEG_INF, F32)
    left = jnp.concatenate([rev, off], axis=1)
    right = jnp.concatenate([off, rev], axis=1)
    u = jnp.concatenate([left, right], axis=-1)
    return pl.pallas_call(
        _na_table_kernel,
        grid=(NA_HEADS,),
        in_specs=[pl.BlockSpec((None, NA_BIAS_ROWS + 1, LANES), lambda h: (h, 0, 0))],
        out_specs=pl.BlockSpec((N_TABLE_VARIANTS, None, NA_NB_KEYS, TILE),
                               lambda h: (0, h, 0, 0)),
        out_shape=jax.ShapeDtypeStruct((N_TABLE_VARIANTS, NA_HEADS, NA_NB_KEYS, TILE), F32),
        compiler_params=_cparams(1),
        name="na_table",
    )(u)


def _swa_table():
    j = jnp.array([0, 1, N_LAT_TILES - 1], jnp.int32)[:, None, None]
    qpos = TILE * j + jnp.arange(TILE)[None, None, :]
    kpos = TILE * j - SWA_NB_TOKENS + jnp.arange(SWA_NB_BLOCKS * SWA_NB_TOKENS)[None, :, None]
    ok = (jnp.abs(kpos - qpos) <= SWA_WINDOW) & (kpos >= 0) & (kpos < SEQ)
    tbl = jnp.where(ok, 0.0, NEG_INF).astype(F32)
    tbl = jnp.concatenate([tbl, jnp.full((1,) + tbl.shape[1:], NEG_INF, F32)], axis=0)
    return tbl[:, None]


def _swap_halves(w):
    half = w.shape[-1] // 2
    return jnp.concatenate([w[..., half:], w[..., :half]], axis=-1)


def _mla_weights(w_in, w_uq, w_ukv):
    lat = MLA_Q_LORA + MLA_KV_LORA
    kpe = w_in[:, lat:]
    z = jnp.zeros((D_MODEL, LANES - MLA_ROPE), w_in.dtype)
    w_in_x = jnp.concatenate([w_in[:, :lat], kpe, z, _swap_halves(kpe), z], axis=1)
    uq = w_uq.reshape(MLA_Q_LORA, MLA_HEADS, MLA_NOPE + MLA_ROPE)
    rope = uq[:, :, MLA_NOPE:]
    zq = jnp.zeros((MLA_Q_LORA, MLA_HEADS, LANES - MLA_ROPE), w_uq.dtype)
    flat = lambda a: a.reshape(a.shape[0], -1)
    wq_x = jnp.concatenate([flat(uq[:, :, :MLA_NOPE]),
                            flat(jnp.concatenate([rope, zq], axis=-1)),
                            flat(jnp.concatenate([_swap_halves(rope), zq], axis=-1))], axis=1)
    ukv = w_ukv.reshape(MLA_KV_LORA, MLA_HEADS, MLA_NOPE + MLA_V)
    wkv_x = jnp.concatenate([flat(ukv[:, :, :MLA_NOPE]), flat(ukv[:, :, MLA_NOPE:])], axis=1)
    return w_in_x.astype(BF16), wq_x.astype(BF16), wkv_x.astype(BF16)


def _swa_weights(w_qkv):
    hd = SWA_HEADS * SWA_HEAD_DIM
    kd = SWA_KV_HEADS * SWA_HEAD_DIM
    wq = w_qkv[:, :hd].reshape(D_MODEL, SWA_HEADS, SWA_HEAD_DIM)
    wk = w_qkv[:, hd:hd + kd].reshape(D_MODEL, SWA_KV_HEADS, 1, SWA_HEAD_DIM)
    wv = w_qkv[:, hd + kd:].reshape(D_MODEL, SWA_KV_HEADS, 1, SWA_HEAD_DIM)
    dup = lambda a: jnp.broadcast_to(a, (D_MODEL, SWA_KV_HEADS, 2, SWA_HEAD_DIM)).reshape(
        D_MODEL, -1)
    flat = lambda a: a.reshape(D_MODEL, -1)
    w = jnp.concatenate([flat(wq), flat(_swap_halves(wq)), dup(wk), dup(_swap_halves(wk)),
                         dup(wv)], axis=1)
    return w.astype(BF16)


def kernel(x, c, ctx, c_ctx, w_mod, b_mod, g_pre_mix, g_post_mix, g_pre_ffn, g_post_ffn,
           w_ffn_in, w_ffn_out, mla_w_in, mla_g_q, mla_w_uq, mla_g_kv, mla_w_ukv, mla_w_o,
           na_w_qkv, na_rpb, na_w_o, swa_w_qkv, swa_sink, swa_w_o):
    assert x.shape == (BATCH, SEQ, D_MODEL) and ctx.shape == (BATCH, CTX_LEN, D_MODEL)
    stream = jnp.concatenate([x, ctx], axis=1)
    cc = jnp.concatenate([c, c_ctx[None], jnp.zeros((8 - BATCH - 1, D_MODEL), F32)], axis=0)
    mods = _modulation(cc, w_mod, b_mod)
    mods = mods.reshape(DEPTH, 8, 6, D_MODEL)
    mods = jnp.stack([mods[:, :BATCH],
                      jnp.broadcast_to(mods[:, BATCH:BATCH + 1], (DEPTH, BATCH, 6, D_MODEL))],
                     axis=2)
    cos_t, sin_t = _rope_tables()
    row = lambda g: g.reshape(1, -1)

    for i in range(DEPTH):
        last = i == DEPTH - 1
        kind, j = i % 3, i // 3
        mod = mods[i]
        g_pre = row(g_pre_mix[i])
        if kind == 0:
            w_in_x, wq_x, wkv_x = _mla_weights(mla_w_in[j], mla_w_uq[j], mla_w_ukv[j])
            q, k, vt, ksq = _mla_proj(stream, mod, g_pre, cos_t, sin_t, w_in_x,
                                      row(mla_g_q[j]), row(mla_g_kv[j]), wq_x, wkv_x)
            o = _mla_attention(q, k, vt, jnp.swapaxes(ksq, 1, 2))
            w_o = mla_w_o[j]
        elif kind == 1:
            qk, vt = _na_proj(stream, mod, g_pre, na_w_qkv[j].astype(BF16))
            o = _local_attention(qk, 0, qk, 1, NA_HEADS * NA_HEAD_DIM, vt, NA_NB_TOKENS,
                                 NA_NB_BLOCKS, _na_table(na_rpb[j]), None, lambda p: p,
                                 "na_attn")
            w_o = na_w_o[j]
        else:
            q, k, vt = _swa_proj(stream, mod, g_pre, cos_t, sin_t, _swa_weights(swa_w_qkv[j]))
            sink = jnp.broadcast_to((swa_sink[j].astype(F32) * LOG2E)[:, None],
                                    (SWA_HEADS, TILE))
            o = _local_attention(q, 0, k, 0, 2 * LANES, vt, SWA_NB_TOKENS, SWA_NB_BLOCKS,
                                 _swa_table(), sink,
                                 lambda p: p // (HEAD_PAIRS // SWA_KV_HEADS), "swa_attn")
            w_o = swa_w_o[j]
        stream = _post_ffn(stream, o, mod, row(g_post_mix[i]), row(g_pre_ffn[i]),
                           row(g_post_ffn[i]), w_o.astype(BF16), w_ffn_in[i].astype(BF16),
                           w_ffn_out[i].astype(BF16), N_LAT_TILES if last else N_TILES)
    return stream
```

```python
import functools

import jax
import jax.numpy as jnp
from jax import lax
from jax.experimental import pallas as pl
from jax.experimental.pallas import tpu as pltpu

D_MODEL = 1024
BATCH = 4
SEQ = 8192
DEPTH = 4
GRID_W = 64
CTX_LEN = 256
D_FF = 2816
NORM_EPS = 1e-6
ROPE_THETA = 10000.0
NEG_INF = -1e30

MLA_HEADS = 8
MLA_Q_LORA = 256
MLA_KV_LORA = 256
MLA_NOPE = 128
MLA_ROPE = 64
MLA_V = 128
MLA_QK_PAD = 256

NA_HEADS = 16
NA_HEAD_DIM = 64
NA_WIN_ROWS = 8
NA_WIN_COLS = 16

SWA_HEADS = 16
SWA_KV_HEADS = 2
SWA_HEAD_DIM = 64
SWA_WINDOW = 128

TOKENS = SEQ + CTX_LEN
TILE = 256
N_TILES = TOKENS // TILE
N_LAT_TILES = SEQ // TILE
POST_TILES_PER_STEP = 2
TOKENS_PAD = -(-N_TILES // POST_TILES_PER_STEP) * POST_TILES_PER_STEP * TILE
LANES = 128
BF16_ROWS = 16
HEAD_PAIRS = 8
N_TABLE_VARIANTS = 4

MLA_TQ = 512
MLA_TK = 2048
SOFTMAX_HEADROOM_LOG2 = 64.0
SOFTMAX_MIN_ROW_SUM = 2.0 ** -40
SOFTMAX_BOUND_SLACK = 1.02

NA_NB_TOKENS = TILE
NA_NB_BLOCKS = 3
SWA_NB_TOKENS = 128
SWA_NB_BLOCKS = 4

VMEM_LIMIT = 56 * 1024 * 1024

LOG2E = 1.4426950408889634

BF16 = jnp.bfloat16
F32 = jnp.float32


def _cparams(n_axes):
    return pltpu.CompilerParams(
        dimension_semantics=("parallel",) * n_axes, vmem_limit_bytes=VMEM_LIMIT)


def _const_spec(shape):
    nd = len(shape)
    return pl.BlockSpec(shape, lambda *_: (0,) * nd, pipeline_mode=pl.Buffered(1))


def _rms(x, g):
    return x * lax.rsqrt(jnp.mean(x * x, axis=-1, keepdims=True) + NORM_EPS) * g


def _dot(a, b):
    return jnp.dot(a, b, preferred_element_type=F32)


def _dot_nt(a, b):
    return lax.dot_general(a, b, (((1,), (1,)), ((), ())), preferred_element_type=F32)


def _ones_rows(n):
    row = lax.broadcasted_iota(jnp.int32, (BF16_ROWS, n), 0)
    return jnp.where(row == 0, 1.0, 0.0).astype(BF16)


def _mod_kernel(c_ref, w_ref, b_ref, o_ref):
    c = c_ref[...]
    a = c * jax.nn.sigmoid(c)
    o_ref[...] = _dot(a, w_ref[...]) + b_ref[...]


def _modulation(cc, w_mod, b_mod):
    tn = 1536
    return pl.pallas_call(
        _mod_kernel,
        grid=(DEPTH, 6 * D_MODEL // tn),
        in_specs=[
            pl.BlockSpec((8, D_MODEL), lambda l, n: (0, 0)),
            pl.BlockSpec((None, D_MODEL, tn), lambda l, n: (l, 0, n)),
            pl.BlockSpec((None, 1, tn), lambda l, n: (l, 0, n)),
        ],
        out_specs=pl.BlockSpec((None, 8, tn), lambda l, n: (l, 0, n)),
        out_shape=jax.ShapeDtypeStruct((DEPTH, 8, 6 * D_MODEL), F32),
        compiler_params=_cparams(2),
        name="modulation",
    )(cc, w_mod, b_mod.reshape(DEPTH, 1, 6 * D_MODEL))


def _mod_spec():
    return pl.BlockSpec((None, None, 6, D_MODEL),
                        lambda b, j: (b, (j == N_LAT_TILES).astype(jnp.int32), 0, 0))


def _x_spec():
    return pl.BlockSpec((None, TILE, D_MODEL), lambda b, j: (b, j, 0))


def _rope_spec():
    return pl.BlockSpec((TILE, LANES), lambda b, j: (j, 0))


def _pre_mix(x_ref, mod_ref, g_ref):
    shift = mod_ref[0:1, :]
    scale = mod_ref[1:2, :]
    h = _rms(x_ref[...], g_ref[...]) * (1.0 + scale) + shift
    return h.astype(BF16)


def _mla_proj_kernel(x_ref, mod_ref, g_ref, cos_ref, sin_ref, w_in_ref, gq_ref, gkv_ref,
                     wq_ref, wkv_ref, q_ref, k_ref, vt_ref, ksq_ref, *, scale):
    h = _pre_mix(x_ref, mod_ref, g_ref)
    a = _dot(h, w_in_ref[...])
    qn = _rms(a[:, :MLA_Q_LORA], gq_ref[...]).astype(BF16)
    kvn = _rms(a[:, MLA_Q_LORA:MLA_Q_LORA + MLA_KV_LORA], gkv_ref[...]).astype(BF16)
    cos = cos_ref[...]
    sin = sin_ref[...]
    k_rope = (a[:, 512:640] * cos + a[:, 640:768] * sin).astype(BF16)
    qq = _dot(qn, wq_ref[...])
    kv = _dot(kvn, wkv_ref[...])
    hn = MLA_HEADS * LANES
    k_rope_f = k_rope.astype(F32)
    rope_sq = jnp.sum(k_rope_f * k_rope_f, axis=-1, keepdims=True)
    for hd in range(MLA_HEADS):
        lo, hi = hd * LANES, (hd + 1) * LANES
        base = hd * MLA_QK_PAD
        q_ref[:, base:base + LANES] = (qq[:, lo:hi] * scale).astype(BF16)
        q_rope = qq[:, hn + lo:hn + hi] * cos + qq[:, 2 * hn + lo:2 * hn + hi] * sin
        q_ref[:, base + LANES:base + 2 * LANES] = (q_rope * scale).astype(BF16)
        k_nope = kv[:, lo:hi].astype(BF16)
        k_ref[:, base:base + LANES] = k_nope
        k_ref[:, base + LANES:base + 2 * LANES] = k_rope
        k_nope_f = k_nope.astype(F32)
        k_sq = jnp.sum(k_nope_f * k_nope_f, axis=-1, keepdims=True) + rope_sq
        ksq_ref[hd:hd + 1, :] = jnp.broadcast_to(jnp.max(k_sq, axis=0, keepdims=True),
                                                 (1, LANES))
    vt_ref[...] = kv[:, hn:].T.astype(BF16)


def _mla_proj(x, mod, g_pre, cos, sin, w_in, g_q, g_kv, wq, wkv):
    hq = MLA_HEADS * MLA_QK_PAD
    hv = MLA_HEADS * MLA_V
    out_spec_q = pl.BlockSpec((None, TILE, hq), lambda b, j: (b, j, 0))
    out_spec_v = pl.BlockSpec((None, hv, TILE), lambda b, j: (b, 0, j))
    scale = float((MLA_NOPE + MLA_ROPE) ** -0.5) * LOG2E
    return pl.pallas_call(
        functools.partial(_mla_proj_kernel, scale=scale),
        grid=(BATCH, N_TILES),
        in_specs=[_x_spec(), _mod_spec(), _const_spec((1, D_MODEL)), _rope_spec(), _rope_spec(),
                  _const_spec(w_in.shape), _const_spec((1, MLA_Q_LORA)),
                  _const_spec((1, MLA_KV_LORA)), _const_spec(wq.shape), _const_spec(wkv.shape)],
        out_specs=[out_spec_q, out_spec_q, out_spec_v,
                   pl.BlockSpec((None, None, MLA_HEADS, LANES), lambda b, j: (b, j, 0, 0))],
        out_shape=[jax.ShapeDtypeStruct((BATCH, TOKENS, hq), BF16),
                   jax.ShapeDtypeStruct((BATCH, TOKENS, hq), BF16),
                   jax.ShapeDtypeStruct((BATCH, hv, TOKENS), BF16),
                   jax.ShapeDtypeStruct((BATCH, N_TILES, MLA_HEADS, LANES), F32)],
        compiler_params=_cparams(2),
        name="mla_proj",
    )(x, mod, g_pre, cos, sin, w_in, g_q, g_kv, wq, wkv)


def _softmax_update(s, vt, carry):
    m, l, acc = carry
    m_new = jnp.maximum(m, jnp.max(s, axis=0, keepdims=True))
    alpha = jnp.exp2(m - m_new)
    p = jnp.exp2(s - m_new)
    l = alpha * l + jnp.sum(p, axis=0, keepdims=True)
    acc = alpha * acc + _dot(vt, p.astype(BF16))
    return m_new, l, acc


def _mla_attn_kernel(q_ref, k_ref, vt_ref, ksq_ref, o_ref, *, chunks):
    q = q_ref[...]
    tq = q.shape[0]

    def logits(chunk):
        start, size = chunk
        return _dot_nt(k_ref[start:start + size, :], q)

    def single_pass(r):
        l = jnp.zeros((1, tq), F32)
        acc = jnp.zeros((MLA_V, tq), F32)
        s = logits(chunks[0])
        for c, (start, size) in enumerate(chunks):
            s_next = logits(chunks[c + 1]) if c + 1 < len(chunks) else None
            p = jnp.exp2(s - r)
            l = l + jnp.sum(p, axis=0, keepdims=True)
            acc = acc + _dot(vt_ref[:, start:start + size], p.astype(BF16))
            s = s_next
        return l, acc

    def online():
        carry = (jnp.full((1, tq), NEG_INF, F32), jnp.zeros((1, tq), F32),
                 jnp.zeros((MLA_V, tq), F32))
        s = logits(chunks[0])
        for c, (start, size) in enumerate(chunks):
            s_next = logits(chunks[c + 1]) if c + 1 < len(chunks) else None
            carry = _softmax_update(s, vt_ref[:, start:start + size], carry)
            s = s_next
        return carry[1], carry[2]

    if ksq_ref is None:
        l, acc = online()
    else:
        qf = q.astype(F32)
        ones = jnp.ones((8, q.shape[1]), BF16)
        q_sq = _dot_nt(ones, (qf * qf).astype(BF16))[0:1]
        k_sq = jnp.max(ksq_ref[...], axis=0, keepdims=True)[:, 0:1]
        bound = jnp.sqrt(q_sq * k_sq) * SOFTMAX_BOUND_SLACK
        l, acc = single_pass(bound - SOFTMAX_HEADROOM_LOG2)
        l, acc = lax.cond(jnp.min(l) >= SOFTMAX_MIN_ROW_SUM, lambda: (l, acc), online)
    o_ref[...] = (acc / l).T.astype(BF16)


def _mla_attn_ctx_kernel(q_ref, k_ref, vt_ref, latent_out_ref, o_ref):
    del latent_out_ref
    _mla_attn_kernel(q_ref, k_ref, vt_ref, None, o_ref, chunks=((0, CTX_LEN),))


def _mla_attention(q, k, vt, ksq):
    hv = MLA_HEADS * MLA_V
    chunks = ((SEQ, CTX_LEN),) + tuple((c * MLA_TK, MLA_TK) for c in range(SEQ // MLA_TK))
    o = pl.pallas_call(
        functools.partial(_mla_attn_kernel, chunks=chunks),
        grid=(BATCH, MLA_HEADS, SEQ // MLA_TQ),
        in_specs=[
            pl.BlockSpec((None, MLA_TQ, MLA_QK_PAD), lambda b, h, i: (b, i, h)),
            pl.BlockSpec((None, TOKENS, MLA_QK_PAD), lambda b, h, i: (b, 0, h)),
            pl.BlockSpec((None, MLA_V, TOKENS), lambda b, h, i: (b, h, 0)),
            pl.BlockSpec((None, None, N_TILES, LANES), lambda b, h, i: (b, h, 0, 0)),
        ],
        out_specs=pl.BlockSpec((None, MLA_TQ, MLA_V), lambda b, h, i: (b, i, h)),
        out_shape=jax.ShapeDtypeStruct((BATCH, TOKENS, hv), BF16),
        compiler_params=_cparams(3),
        name="mla_attn",
    )(q, k, vt, ksq)
    ctx_blk = SEQ // CTX_LEN
    return pl.pallas_call(
        _mla_attn_ctx_kernel,
        grid=(BATCH, MLA_HEADS),
        in_specs=[
            pl.BlockSpec((None, CTX_LEN, MLA_QK_PAD), lambda b, h: (b, ctx_blk, h)),
            pl.BlockSpec((None, CTX_LEN, MLA_QK_PAD), lambda b, h: (b, ctx_blk, h)),
            pl.BlockSpec((None, MLA_V, CTX_LEN), lambda b, h: (b, h, ctx_blk)),
            pl.BlockSpec(memory_space=pl.ANY),
        ],
        out_specs=pl.BlockSpec((None, CTX_LEN, MLA_V), lambda b, h: (b, ctx_blk, h)),
        out_shape=jax.ShapeDtypeStruct((BATCH, TOKENS, hv), BF16),
        input_output_aliases={3: 0},
        compiler_params=_cparams(2),
        name="mla_attn_ctx",
    )(q, k, vt, o)


def _group_sq_norms(xb):
    w = xb.shape[1]
    xf = xb.astype(F32)
    group = lax.broadcasted_iota(jnp.int32, (w, LANES), 0) // NA_HEAD_DIM
    col = lax.broadcasted_iota(jnp.int32, (w, LANES), 1)
    indicator = jnp.where(group == col, 1.0, 0.0).astype(BF16)
    return _dot((xf * xf).astype(BF16), indicator)


def _store_norms(qb, kb, qsq_ref, ksq_ref):
    qsq_ref[...] = _group_sq_norms(qb).T[:NA_HEADS, :]
    k_max = jnp.max(_group_sq_norms(kb), axis=0, keepdims=True)
    ksq_ref[...] = jnp.broadcast_to(k_max, (8, LANES))


def _norm_specs():
    return ([pl.BlockSpec((None, NA_HEADS, TILE), lambda b, j: (b, 0, j)),
             pl.BlockSpec((None, None, 8, LANES), lambda b, j: (b, j, 0, 0))],
            [jax.ShapeDtypeStruct((BATCH, NA_HEADS, TOKENS), F32),
             jax.ShapeDtypeStruct((BATCH, N_TILES, 8, LANES), F32)])


def _na_proj_kernel(x_ref, mod_ref, g_ref, w_ref, qk_ref, vt_ref, qsq_ref, ksq_ref, *, scale):
    h = _pre_mix(x_ref, mod_ref, g_ref)
    a = _dot(h, w_ref[...])
    hd = NA_HEADS * NA_HEAD_DIM
    qb = (a[:, :hd] * scale).astype(BF16)
    kb = a[:, hd:2 * hd].astype(BF16)
    qk_ref[:, :hd] = qb
    qk_ref[:, hd:] = kb
    vt_ref[...] = a[:, 2 * hd:].T.astype(BF16)
    _store_norms(qb, kb, qsq_ref, ksq_ref)


def _na_proj(x, mod, g_pre, w):
    hd = NA_HEADS * NA_HEAD_DIM
    norm_specs, norm_shapes = _norm_specs()
    return pl.pallas_call(
        functools.partial(_na_proj_kernel, scale=float(NA_HEAD_DIM ** -0.5) * LOG2E),
        grid=(BATCH, N_TILES),
        in_specs=[_x_spec(), _mod_spec(), _const_spec((1, D_MODEL)), _const_spec(w.shape)],
        out_specs=[pl.BlockSpec((None, TILE, 2 * hd), lambda b, j: (b, j, 0)),
                   pl.BlockSpec((None, hd, TILE), lambda b, j: (b, 0, j))] + norm_specs,
        out_shape=[jax.ShapeDtypeStruct((BATCH, TOKENS, 2 * hd), BF16),
                   jax.ShapeDtypeStruct((BATCH, hd, TOKENS), BF16)] + norm_shapes,
        compiler_params=_cparams(2),
        name="na_proj",
    )(x, mod, g_pre, w)


def _swa_proj_kernel(x_ref, mod_ref, g_ref, cos_ref, sin_ref, w_ref, q_ref, k_ref, vt_ref,
                     qsq_ref, ksq_ref, *, scale):
    h = _pre_mix(x_ref, mod_ref, g_ref)
    a = _dot(h, w_ref[...])
    cos = cos_ref[...]
    sin = sin_ref[...]
    hd = SWA_HEADS * SWA_HEAD_DIM
    for p in range(HEAD_PAIRS):
        lo, hi = p * LANES, (p + 1) * LANES
        q = a[:, lo:hi] * cos + a[:, hd + lo:hd + hi] * sin
        q_ref[:, lo:hi] = (q * scale).astype(BF16)
    kw = 2 * LANES
    for p in range(2):
        lo, hi = p * LANES, (p + 1) * LANES
        k = a[:, 2 * hd + lo:2 * hd + hi] * cos + a[:, 2 * hd + kw + lo:2 * hd + kw + hi] * sin
        k_ref[:, lo:hi] = k.astype(BF16)
    vt_ref[...] = a[:, 2 * hd + 2 * kw:].T.astype(BF16)
    _store_norms(q_ref[...], k_ref[...], qsq_ref, ksq_ref)


def _swa_proj(x, mod, g_pre, cos, sin, w):
    hd = SWA_HEADS * SWA_HEAD_DIM
    kw = 2 * LANES
    norm_specs, norm_shapes = _norm_specs()
    return pl.pallas_call(
        functools.partial(_swa_proj_kernel, scale=float(SWA_HEAD_DIM ** -0.5) * LOG2E),
        grid=(BATCH, N_TILES),
        in_specs=[_x_spec(), _mod_spec(), _const_spec((1, D_MODEL)), _rope_spec(), _rope_spec(),
                  _const_spec(w.shape)],
        out_specs=[pl.BlockSpec((None, TILE, hd), lambda b, j: (b, j, 0)),
                   pl.BlockSpec((None, TILE, kw), lambda b, j: (b, j, 0)),
                   pl.BlockSpec((None, kw, TILE), lambda b, j: (b, 0, j))] + norm_specs,
        out_shape=[jax.ShapeDtypeStruct((BATCH, TOKENS, hd), BF16),
                   jax.ShapeDtypeStruct((BATCH, TOKENS, kw), BF16),
                   jax.ShapeDtypeStruct((BATCH, kw, TOKENS), BF16)] + norm_shapes,
        compiler_params=_cparams(2),
        name="swa_proj",
    )(x, mod, g_pre, cos, sin, w)


def _local_attn_kernel(*refs, n_nb, kv_slot, k_norm_lane, per_head_table, has_sink):
    q_ref = refs[0]
    k_refs = refs[1:1 + n_nb]
    kc_ref = refs[1 + n_nb]
    vt_refs = refs[2 + n_nb:2 + 2 * n_nb]
    vtc_ref = refs[2 + 2 * n_nb]
    tbl_ref, head_ref, qsq_ref, ksq_ref, o_ref = refs[3 + 2 * n_nb:]
    n_heads = 2 * HEAD_PAIRS
    hdim = NA_HEAD_DIM
    low = lax.broadcasted_iota(jnp.int32, (TILE, LANES), 1) < hdim

    def logits(hd):
        p, half = divmod(hd, 2)
        qp = q_ref[:, p * LANES:(p + 1) * LANES]
        qm = jnp.where(low if half == 0 else jnp.logical_not(low), qp, jnp.zeros_like(qp))
        ks = slice(kv_slot(p) * LANES, (kv_slot(p) + 1) * LANES)
        kn = jnp.concatenate([r[:, ks] for r in k_refs], axis=0)
        s_n = _dot_nt(kn, qm) + tbl_ref[hd if per_head_table else 0]
        s_c = _dot_nt(kc_ref[:, ks], qm)
        return s_n, s_c

    def attend(reference):
        s = logits(0)
        outs, pair_out, l_min = [], [], None
        for hd in range(n_heads):
            s_next = logits(hd + 1) if hd + 1 < n_heads else None
            p, half = divmod(hd, 2)
            s_n, s_c = s
            r = reference(hd, s_n, s_c)
            p_n = jnp.exp2(s_n - r).astype(BF16)
            p_c = jnp.exp2(s_c - r).astype(BF16)
            lo = kv_slot(p) * LANES + half * hdim
            vt_n = jnp.concatenate([v[lo:lo + hdim, :] for v in vt_refs], axis=1)
            vt_n = jnp.concatenate([vt_n, _ones_rows(vt_n.shape[1])], axis=0)
            vt_c = jnp.concatenate([vtc_ref[lo:lo + hdim, :], _ones_rows(CTX_LEN)], axis=0)
            acc = _dot(vt_n, p_n) + _dot(vt_c, p_c)
            l = acc[hdim:hdim + 1]
            if has_sink:
                l = l + jnp.exp2(head_ref[hd:hd + 1, :] - r)
            l_min = l if l_min is None else jnp.minimum(l_min, l)
            pair_out.append(acc[:hdim] / l)
            if half == 1:
                outs.append(jnp.concatenate(pair_out, axis=0).T.astype(BF16))
                pair_out = []
            s = s_next
        return tuple(outs), l_min

    def row_max(hd, s_n, s_c):
        m = jnp.maximum(jnp.max(s_n, axis=0, keepdims=True), jnp.max(s_c, axis=0, keepdims=True))
        return jnp.maximum(m, head_ref[hd:hd + 1, :]) if has_sink else m

    k_sq = jnp.max(ksq_ref[...], axis=0)[0:1, :]

    def bound(hd, s_n, s_c):
        lane = k_norm_lane(hd)
        qk = jnp.sqrt(qsq_ref[hd:hd + 1, :] * k_sq[:, lane:lane + 1]) * SOFTMAX_BOUND_SLACK
        extra = head_ref[hd:hd + 1, :]
        top = jnp.maximum(qk, extra) if has_sink else qk + extra
        return top - SOFTMAX_HEADROOM_LOG2

    outs, l_min = attend(bound)
    outs = lax.cond(jnp.min(l_min) >= SOFTMAX_MIN_ROW_SUM, lambda: outs,
                    lambda: attend(row_max)[0])
    for p, pair in enumerate(outs):
        o_ref[:, p * LANES:(p + 1) * LANES] = pair


def _local_attention(q, q_col, k, k_col, k_width, vt, qsq, ksq, nb_tokens, n_nb, table,
                     head_rows, has_sink, kv_slot, k_norm_lane, name):
    hd = NA_HEADS * NA_HEAD_DIM
    per_head_table = table.shape[1] > 1
    lat_hi = N_LAT_TILES - 1
    per_tile = TILE // nb_tokens
    nb_hi = SEQ // nb_tokens - 1

    def nb_index(j, n):
        return jnp.clip(j * per_tile - 1 + n, 0, nb_hi)

    def k_spec(n):
        return pl.BlockSpec((None, nb_tokens, k_width), lambda b, j: (b, nb_index(j, n), k_col))

    def vt_spec(n):
        return pl.BlockSpec((None, k_width, nb_tokens), lambda b, j: (b, 0, nb_index(j, n)))

    def variant(j):
        return jnp.where(j == 0, 0, jnp.where(j == lat_hi, 2, jnp.where(j > lat_hi, 3, 1)))

    tbl_spec = pl.BlockSpec((None,) + table.shape[1:], lambda b, j: (variant(j), 0, 0, 0))
    in_specs = ([pl.BlockSpec((None, TILE, hd), lambda b, j: (b, j, q_col))]
                + [k_spec(n) for n in range(n_nb)]
                + [pl.BlockSpec((None, CTX_LEN, k_width), lambda b, j: (b, N_LAT_TILES, k_col))]
                + [vt_spec(n) for n in range(n_nb)]
                + [pl.BlockSpec((None, k_width, CTX_LEN), lambda b, j: (b, 0, N_LAT_TILES))]
                + [tbl_spec, _const_spec((NA_HEADS, TILE)),
                   pl.BlockSpec((None, NA_HEADS, TILE), lambda b, j: (b, 0, j)),
                   pl.BlockSpec((None, N_TILES, 8, LANES), lambda b, j: (b, 0, 0, 0))])
    return pl.pallas_call(
        functools.partial(_local_attn_kernel, n_nb=n_nb, kv_slot=kv_slot,
                          k_norm_lane=k_norm_lane, per_head_table=per_head_table,
                          has_sink=has_sink),
        grid=(BATCH, N_TILES),
        in_specs=in_specs,
        out_specs=pl.BlockSpec((None, TILE, hd), lambda b, j: (b, j, 0)),
        out_shape=jax.ShapeDtypeStruct((BATCH, TOKENS, hd), BF16),
        compiler_params=_cparams(2),
        name=name,
    )(q, *([k] * (n_nb + 1)), *([vt] * (n_nb + 1)), table, head_rows, qsq, ksq)


def _post_ffn_kernel(x_ref, oa_ref, ob_ref, mod_ref, gpm_ref, gpf_ref, gof_ref, wo_ref,
                     win_ref, wout_ref, y_ref, act_ref):
    gate_m = mod_ref[2:3, :]
    shift_f = mod_ref[3:4, :]
    scale_f = mod_ref[4:5, :]
    gate_f = mod_ref[5:6, :]
    rows = [slice(i * TILE, (i + 1) * TILE) for i in range(POST_TILES_PER_STEP)]
    mixed = [_dot(o_ref[...], wo_ref[...]) for o_ref in (oa_ref, ob_ref)]
    x1 = [x_ref[r, :] + gate_m * _rms(mx, gpm_ref[...]) for r, mx in zip(rows, mixed)]
    h = [(_rms(xi, gpf_ref[...]) * (1.0 + scale_f) + shift_f).astype(BF16) for xi in x1]
    chunk = 256
    for i, r in enumerate(rows):
        for f in range(D_FF // chunk):
            lo, hi = f * chunk, (f + 1) * chunk
            gate = _dot(h[i], win_ref[:, lo:hi])
            up = _dot(h[i], win_ref[:, D_FF + lo:D_FF + hi])
            act_ref[i, :, lo:hi] = (gate * jax.nn.sigmoid(gate) * up).astype(BF16)
        ffn = _dot(act_ref[i], wout_ref[...])
        y_ref[r, :] = x1[i] + gate_f * _rms(ffn, gof_ref[...])


def _post_ffn(x, o, mod, g_post_mix, g_pre_ffn, g_post_ffn, wo, win, wout, last):
    vec = _const_spec((1, D_MODEL))
    rows = POST_TILES_PER_STEP * TILE
    n_steps = SEQ // rows if last else TOKENS_PAD // rows
    ctx_step = N_LAT_TILES // POST_TILES_PER_STEP
    x_spec = pl.BlockSpec((None, rows, D_MODEL), lambda b, j: (b, j, 0))

    def o_spec(i):
        return pl.BlockSpec(
            (None, TILE, D_MODEL),
            lambda b, j: (b, jnp.minimum(POST_TILES_PER_STEP * j + i, N_LAT_TILES), 0))

    mod_spec = pl.BlockSpec((None, None, 6, D_MODEL),
                            lambda b, j: (b, (j == ctx_step).astype(jnp.int32), 0, 0))
    return pl.pallas_call(
        _post_ffn_kernel,
        grid=(BATCH, n_steps),
        in_specs=[x_spec, o_spec(0), o_spec(1), mod_spec, vec, vec, vec,
                  _const_spec(wo.shape), _const_spec(win.shape), _const_spec(wout.shape)],
        out_specs=x_spec,
        out_shape=jax.ShapeDtypeStruct((BATCH, n_steps * rows, D_MODEL), F32),
        scratch_shapes=[pltpu.VMEM((POST_TILES_PER_STEP, TILE, D_FF), BF16)],
        compiler_params=_cparams(2),
        name="post_ffn",
    )(x, o, o, mod, g_post_mix, g_pre_ffn, g_post_ffn, wo, win, wout)


def _rope_tables():
    t = jnp.arange(SEQ)
    row = (t // GRID_W).astype(F32)
    col = (t % GRID_W).astype(F32)
    n_freq = MLA_ROPE // 4
    inv = ROPE_THETA ** (-jnp.arange(n_freq, dtype=F32) / n_freq)
    ang = jnp.concatenate([row[:, None] * inv, col[:, None] * inv], axis=-1)
    cos, sin = jnp.cos(ang), jnp.sin(ang)
    cos_t = jnp.concatenate([cos, cos, cos, cos], axis=-1)
    sin_t = jnp.concatenate([-sin, sin, -sin, sin], axis=-1)
    cos_t = jnp.concatenate([cos_t, jnp.ones((CTX_LEN, LANES), F32)], axis=0)
    sin_t = jnp.concatenate([sin_t, jnp.zeros((CTX_LEN, LANES), F32)], axis=0)
    return cos_t, sin_t


NA_ROWS_PER_TILE = TILE // GRID_W
NA_BIAS_ROWS = 2 * NA_WIN_ROWS - 1
NA_BIAS_COLS = 2 * NA_WIN_COLS - 1
NA_NB_KEYS = NA_NB_BLOCKS * NA_NB_TOKENS


def _na_table_kernel(u_ref, o_ref):
    rows = SEQ // GRID_W
    kc = lax.broadcasted_iota(jnp.int32, (GRID_W, LANES), 0)
    lane = lax.broadcasted_iota(jnp.int32, (GRID_W, LANES), 1)
    qc = lane % GRID_W
    col_start = jnp.clip(qc - NA_WIN_COLS // 2, 0, GRID_W - NA_WIN_COLS)
    col_ok = (kc >= col_start) & (kc < col_start + NA_WIN_COLS)
    first_half = lane < GRID_W
    neg = jnp.full((GRID_W, LANES), NEG_INF, F32)
    toeplitz = {}
    for i in range(1, NA_BIAS_ROWS):
        row = jnp.broadcast_to(u_ref[i:i + 1, :], (GRID_W, LANES))
        toeplitz[i] = pltpu.roll(row, LANES - (NA_WIN_COLS - 1), 1, stride=1,
                                 stride_axis=0) * LOG2E
    for v, tile in enumerate((0, 1, N_LAT_TILES - 1)):
        for kl in range(NA_NB_BLOCKS * NA_ROWS_PER_TILE):
            kr = NA_ROWS_PER_TILE * (tile - 1) + kl
            for qp in range(NA_ROWS_PER_TILE // 2):
                oks = []
                for half in range(2):
                    qr = NA_ROWS_PER_TILE * tile + 2 * qp + half
                    row_start = min(max(qr - NA_WIN_ROWS // 2, 0), rows - NA_WIN_ROWS)
                    oks.append(row_start <= kr < row_start + NA_WIN_ROWS)
                if oks[0] and oks[1]:
                    ok = col_ok
                elif oks[0]:
                    ok = col_ok & first_half
                elif oks[1]:
                    ok = col_ok & jnp.logical_not(first_half)
                else:
                    ok = None
                i = kl - NA_ROWS_PER_TILE - 2 * qp + NA_WIN_ROWS - 1
                blk = neg if ok is None else jnp.where(ok, toeplitz[i], neg)
                o_ref[v, kl * GRID_W:(kl + 1) * GRID_W, qp * LANES:(qp + 1) * LANES] = blk
    o_ref[N_TABLE_VARIANTS - 1] = jnp.full((NA_NB_KEYS, TILE), NEG_INF, F32)


def _na_table(rpb):
    pad = jnp.full((NA_HEADS, NA_BIAS_ROWS, GRID_W - NA_BIAS_COLS), NEG_INF, F32)
    rev = jnp.concatenate([rpb.astype(F32)[:, :, ::-1], pad], axis=-1)
    off = jnp.full((NA_HEADS, 1, GRID_W), NEG_INF, F32)
    left = jnp.concatenate([rev, off], axis=1)
    right = jnp.concatenate([off, rev], axis=1)
    u = jnp.concatenate([left, right], axis=-1)
    return pl.pallas_call(
        _na_table_kernel,
        grid=(NA_HEADS,),
        in_specs=[pl.BlockSpec((None, NA_BIAS_ROWS + 1, LANES), lambda h: (h, 0, 0))],
        out_specs=pl.BlockSpec((N_TABLE_VARIANTS, None, NA_NB_KEYS, TILE),
                               lambda h: (0, h, 0, 0)),
        out_shape=jax.ShapeDtypeStruct((N_TABLE_VARIANTS, NA_HEADS, NA_NB_KEYS, TILE), F32),
        compiler_params=_cparams(1),
        name="na_table",
    )(u)


def _swa_table():
    j = jnp.array([0, 1, N_LAT_TILES - 1], jnp.int32)[:, None, None]
    qpos = TILE * j + jnp.arange(TILE)[None, None, :]
    kpos = TILE * j - SWA_NB_TOKENS + jnp.arange(SWA_NB_BLOCKS * SWA_NB_TOKENS)[None, :, None]
    ok = (jnp.abs(kpos - qpos) <= SWA_WINDOW) & (kpos >= 0) & (kpos < SEQ)
    tbl = jnp.where(ok, 0.0, NEG_INF).astype(F32)
    tbl = jnp.concatenate([tbl, jnp.full((1,) + tbl.shape[1:], NEG_INF, F32)], axis=0)
    return tbl[:, None]


def _swap_halves(w):
    half = w.shape[-1] // 2
    return jnp.concatenate([w[..., half:], w[..., :half]], axis=-1)


def _mla_weights(w_in, w_uq, w_ukv):
    lat = MLA_Q_LORA + MLA_KV_LORA
    kpe = w_in[:, lat:]
    z = jnp.zeros((D_MODEL, LANES - MLA_ROPE), w_in.dtype)
    w_in_x = jnp.concatenate([w_in[:, :lat], kpe, z, _swap_halves(kpe), z], axis=1)
    uq = w_uq.reshape(MLA_Q_LORA, MLA_HEADS, MLA_NOPE + MLA_ROPE)
    rope = uq[:, :, MLA_NOPE:]
    zq = jnp.zeros((MLA_Q_LORA, MLA_HEADS, LANES - MLA_ROPE), w_uq.dtype)
    flat = lambda a: a.reshape(a.shape[0], -1)
    wq_x = jnp.concatenate([flat(uq[:, :, :MLA_NOPE]),
                            flat(jnp.concatenate([rope, zq], axis=-1)),
                            flat(jnp.concatenate([_swap_halves(rope), zq], axis=-1))], axis=1)
    ukv = w_ukv.reshape(MLA_KV_LORA, MLA_HEADS, MLA_NOPE + MLA_V)
    wkv_x = jnp.concatenate([flat(ukv[:, :, :MLA_NOPE]), flat(ukv[:, :, MLA_NOPE:])], axis=1)
    return w_in_x.astype(BF16), wq_x.astype(BF16), wkv_x.astype(BF16)


def _swa_weights(w_qkv):
    hd = SWA_HEADS * SWA_HEAD_DIM
    kd = SWA_KV_HEADS * SWA_HEAD_DIM
    wq = w_qkv[:, :hd].reshape(D_MODEL, SWA_HEADS, SWA_HEAD_DIM)
    wk = w_qkv[:, hd:hd + kd].reshape(D_MODEL, SWA_KV_HEADS, 1, SWA_HEAD_DIM)
    wv = w_qkv[:, hd + kd:].reshape(D_MODEL, SWA_KV_HEADS, 1, SWA_HEAD_DIM)
    dup = lambda a: jnp.broadcast_to(a, (D_MODEL, SWA_KV_HEADS, 2, SWA_HEAD_DIM)).reshape(
        D_MODEL, -1)
    flat = lambda a: a.reshape(D_MODEL, -1)
    w = jnp.concatenate([flat(wq), flat(_swap_halves(wq)), dup(wk), dup(_swap_halves(wk)),
                         dup(wv)], axis=1)
    return w.astype(BF16)


def kernel(x, c, ctx, c_ctx, w_mod, b_mod, g_pre_mix, g_post_mix, g_pre_ffn, g_post_ffn,
           w_ffn_in, w_ffn_out, mla_w_in, mla_g_q, mla_w_uq, mla_g_kv, mla_w_ukv, mla_w_o,
           na_w_qkv, na_rpb, na_w_o, swa_w_qkv, swa_sink, swa_w_o):
    assert x.shape == (BATCH, SEQ, D_MODEL) and ctx.shape == (BATCH, CTX_LEN, D_MODEL)
    pad = jnp.zeros((BATCH, TOKENS_PAD - TOKENS, D_MODEL), F32)
    stream = jnp.concatenate([x, ctx, pad], axis=1)
    cc = jnp.concatenate([c, c_ctx[None], jnp.zeros((8 - BATCH - 1, D_MODEL), F32)], axis=0)
    mods = _modulation(cc, w_mod, b_mod)
    mods = mods.reshape(DEPTH, 8, 6, D_MODEL)
    mods = jnp.stack([mods[:, :BATCH],
                      jnp.broadcast_to(mods[:, BATCH:BATCH + 1], (DEPTH, BATCH, 6, D_MODEL))],
                     axis=2)
    cos_t, sin_t = _rope_tables()
    row = lambda g: g.reshape(1, -1)

    for i in range(DEPTH):
        last = i == DEPTH - 1
        kind, j = i % 3, i // 3
        mod = mods[i]
        g_pre = row(g_pre_mix[i])
        if kind == 0:
            w_in_x, wq_x, wkv_x = _mla_weights(mla_w_in[j], mla_w_uq[j], mla_w_ukv[j])
            q, k, vt, ksq = _mla_proj(stream, mod, g_pre, cos_t, sin_t, w_in_x,
                                      row(mla_g_q[j]), row(mla_g_kv[j]), wq_x, wkv_x)
            o = _mla_attention(q, k, vt, jnp.swapaxes(ksq, 1, 2))
            w_o = mla_w_o[j]
        elif kind == 1:
            qk, vt, qsq, ksq = _na_proj(stream, mod, g_pre, na_w_qkv[j].astype(BF16))
            bias_top = jnp.maximum(jnp.max(na_rpb[j].astype(F32), axis=(1, 2)), 0.0) * LOG2E
            o = _local_attention(qk, 0, qk, 1, NA_HEADS * NA_HEAD_DIM, vt, qsq, ksq,
                                 NA_NB_TOKENS, NA_NB_BLOCKS, _na_table(na_rpb[j]),
                                 jnp.broadcast_to(bias_top[:, None], (NA_HEADS, TILE)), False,
                                 lambda p: p, lambda hd: hd, "na_attn")
            w_o = na_w_o[j]
        else:
            q, k, vt, qsq, ksq = _swa_proj(stream, mod, g_pre, cos_t, sin_t,
                                           _swa_weights(swa_w_qkv[j]))
            sink = jnp.broadcast_to((swa_sink[j].astype(F32) * LOG2E)[:, None],
                                    (SWA_HEADS, TILE))
            heads_per_kv = SWA_HEADS // SWA_KV_HEADS
            o = _local_attention(q, 0, k, 0, 2 * LANES, vt, qsq, ksq, SWA_NB_TOKENS,
                                 SWA_NB_BLOCKS, _swa_table(), sink, True,
                                 lambda p: 2 * p // heads_per_kv,
                                 lambda hd: 2 * (hd // heads_per_kv), "swa_attn")
            w_o = swa_w_o[j]
        stream = _post_ffn(stream, o, mod, row(g_post_mix[i]), row(g_pre_ffn[i]),
                           row(g_post_ffn[i]), w_o.astype(BF16), w_ffn_in[i].astype(BF16),
                           w_ffn_out[i].astype(BF16), last)
    return stream
```

```python
import functools

import jax
import jax.numpy as jnp
import numpy as np
from jax import lax
from jax.experimental import pallas as pl
from jax.experimental.pallas import tpu as pltpu

D_MODEL = 1024
BATCH = 4
SEQ = 8192
DEPTH = 4
GRID_W = 64
CTX_LEN = 256
D_FF = 2816
NORM_EPS = 1e-6
ROPE_THETA = 10000.0
NEG_INF = -1e30

MLA_HEADS = 8
MLA_Q_LORA = 256
MLA_KV_LORA = 256
MLA_NOPE = 128
MLA_ROPE = 64
MLA_V = 128
MLA_QK_PAD = 256

NA_HEADS = 16
NA_HEAD_DIM = 64
NA_WIN_ROWS = 8
NA_WIN_COLS = 16

SWA_HEADS = 16
SWA_KV_HEADS = 2
SWA_HEAD_DIM = 64
SWA_WINDOW = 128

TOKENS = SEQ + CTX_LEN
TILE = 256
N_TILES = TOKENS // TILE
N_LAT_TILES = SEQ // TILE
POST_TILES_PER_STEP = 2
TOKENS_PAD = -(-N_TILES // POST_TILES_PER_STEP) * POST_TILES_PER_STEP * TILE
LANES = 128
BF16_ROWS = 16
HEAD_PAIRS = 8
N_TABLE_VARIANTS = 4

MLA_TQ = 1024
MLA_TK = 2048
SOFTMAX_HEADROOM_LOG2 = 64.0
SOFTMAX_MIN_ROW_SUM = 2.0 ** -40
SOFTMAX_BOUND_SLACK = 1.02

NA_NB_TOKENS = TILE
NA_NB_BLOCKS = 3
SWA_NB_TOKENS = 128
SWA_NB_BLOCKS = 4

VMEM_LIMIT = 56 * 1024 * 1024

LOG2E = 1.4426950408889634

BF16 = jnp.bfloat16
F32 = jnp.float32


def _cparams(n_axes):
    return pltpu.CompilerParams(
        dimension_semantics=("parallel",) * n_axes, vmem_limit_bytes=VMEM_LIMIT)


def _const_spec(shape):
    nd = len(shape)
    return pl.BlockSpec(shape, lambda *_: (0,) * nd, pipeline_mode=pl.Buffered(1))


def _rms(x, g):
    return x * lax.rsqrt(jnp.mean(x * x, axis=-1, keepdims=True) + NORM_EPS) * g


def _dot(a, b):
    return jnp.dot(a, b, preferred_element_type=F32)


def _dot_nt(a, b):
    return lax.dot_general(a, b, (((1,), (1,)), ((), ())), preferred_element_type=F32)


def _ones_rows(n):
    row = lax.broadcasted_iota(jnp.int32, (BF16_ROWS, n), 0)
    return jnp.where(row == 0, 1.0, 0.0).astype(BF16)


def _mod_kernel(c_ref, w_ref, b_ref, o_ref):
    c = c_ref[...]
    a = c * jax.nn.sigmoid(c)
    o_ref[...] = _dot(a, w_ref[...]) + b_ref[...]


def _modulation(cc, w_mod, b_mod):
    tn = 1536
    return pl.pallas_call(
        _mod_kernel,
        grid=(DEPTH, 6 * D_MODEL // tn),
        in_specs=[
            pl.BlockSpec((8, D_MODEL), lambda l, n: (0, 0)),
            pl.BlockSpec((None, D_MODEL, tn), lambda l, n: (l, 0, n)),
            pl.BlockSpec((None, 1, tn), lambda l, n: (l, 0, n)),
        ],
        out_specs=pl.BlockSpec((None, 8, tn), lambda l, n: (l, 0, n)),
        out_shape=jax.ShapeDtypeStruct((DEPTH, 8, 6 * D_MODEL), F32),
        compiler_params=_cparams(2),
        name="modulation",
    )(cc, w_mod, b_mod.reshape(DEPTH, 1, 6 * D_MODEL))


def _mod_spec():
    return pl.BlockSpec((None, None, 6, D_MODEL),
                        lambda b, j: (b, (j == N_LAT_TILES).astype(jnp.int32), 0, 0))


def _x_spec():
    return pl.BlockSpec((None, TILE, D_MODEL), lambda b, j: (b, j, 0))


def _rope_spec():
    return pl.BlockSpec((TILE, LANES), lambda b, j: (j, 0))


def _pre_mix(x_ref, mod_ref, g_ref):
    shift = mod_ref[0:1, :]
    scale = mod_ref[1:2, :]
    h = _rms(x_ref[...], g_ref[...]) * (1.0 + scale) + shift
    return h.astype(BF16)


def _mla_proj_kernel(x_ref, mod_ref, g_ref, cos_ref, sin_ref, w_in_ref, gq_ref, gkv_ref,
                     wq_ref, wkv_ref, q_ref, k_ref, vt_ref, ksq_ref, *, scale):
    h = _pre_mix(x_ref, mod_ref, g_ref)
    a = _dot(h, w_in_ref[...])
    qn = _rms(a[:, :MLA_Q_LORA], gq_ref[...]).astype(BF16)
    kvn = _rms(a[:, MLA_Q_LORA:MLA_Q_LORA + MLA_KV_LORA], gkv_ref[...]).astype(BF16)
    cos = cos_ref[...]
    sin = sin_ref[...]
    k_rope = (a[:, 512:640] * cos + a[:, 640:768] * sin).astype(BF16)
    qq = _dot(qn, wq_ref[...])
    kv = _dot(kvn, wkv_ref[...])
    hn = MLA_HEADS * LANES
    k_rope_f = k_rope.astype(F32)
    rope_sq = jnp.sum(k_rope_f * k_rope_f, axis=-1, keepdims=True)
    for hd in range(MLA_HEADS):
        lo, hi = hd * LANES, (hd + 1) * LANES
        base = hd * MLA_QK_PAD
        q_ref[:, base:base + LANES] = (qq[:, lo:hi] * scale).astype(BF16)
        q_rope = qq[:, hn + lo:hn + hi] * cos + qq[:, 2 * hn + lo:2 * hn + hi] * sin
        q_ref[:, base + LANES:base + 2 * LANES] = (q_rope * scale).astype(BF16)
        k_nope = kv[:, lo:hi].astype(BF16)
        k_ref[:, base:base + LANES] = k_nope
        k_ref[:, base + LANES:base + 2 * LANES] = k_rope
        k_nope_f = k_nope.astype(F32)
        k_sq = jnp.sum(k_nope_f * k_nope_f, axis=-1, keepdims=True) + rope_sq
        ksq_ref[hd:hd + 1, :] = jnp.broadcast_to(jnp.max(k_sq, axis=0, keepdims=True),
                                                 (1, LANES))
    vt_ref[...] = kv[:, hn:].T.astype(BF16)


def _mla_proj(x, mod, g_pre, cos, sin, w_in, g_q, g_kv, wq, wkv):
    hq = MLA_HEADS * MLA_QK_PAD
    hv = MLA_HEADS * MLA_V
    out_spec_q = pl.BlockSpec((None, TILE, hq), lambda b, j: (b, j, 0))
    out_spec_v = pl.BlockSpec((None, hv, TILE), lambda b, j: (b, 0, j))
    scale = float((MLA_NOPE + MLA_ROPE) ** -0.5) * LOG2E
    return pl.pallas_call(
        functools.partial(_mla_proj_kernel, scale=scale),
        grid=(BATCH, N_TILES),
        in_specs=[_x_spec(), _mod_spec(), _const_spec((1, D_MODEL)), _rope_spec(), _rope_spec(),
                  _const_spec(w_in.shape), _const_spec((1, MLA_Q_LORA)),
                  _const_spec((1, MLA_KV_LORA)), _const_spec(wq.shape), _const_spec(wkv.shape)],
        out_specs=[out_spec_q, out_spec_q, out_spec_v,
                   pl.BlockSpec((None, None, MLA_HEADS, LANES), lambda b, j: (b, j, 0, 0))],
        out_shape=[jax.ShapeDtypeStruct((BATCH, TOKENS, hq), BF16),
                   jax.ShapeDtypeStruct((BATCH, TOKENS, hq), BF16),
                   jax.ShapeDtypeStruct((BATCH, hv, TOKENS), BF16),
                   jax.ShapeDtypeStruct((BATCH, N_TILES, MLA_HEADS, LANES), F32)],
        compiler_params=_cparams(2),
        name="mla_proj",
    )(x, mod, g_pre, cos, sin, w_in, g_q, g_kv, wq, wkv)


def _softmax_update(s, vt, carry):
    m, l, acc = carry
    m_new = jnp.maximum(m, jnp.max(s, axis=0, keepdims=True))
    alpha = jnp.exp2(m - m_new)
    p = jnp.exp2(s - m_new)
    l = alpha * l + jnp.sum(p, axis=0, keepdims=True)
    acc = alpha * acc + _dot(vt, p.astype(BF16))
    return m_new, l, acc


def _mla_attn_kernel(q_ref, k_ref, vt_ref, ksq_ref, o_ref, *, chunks):
    q = q_ref[...]
    tq = q.shape[0]

    def logits(chunk):
        start, size = chunk
        return _dot_nt(k_ref[start:start + size, :], q)

    def single_pass(r):
        l = jnp.zeros((1, tq), F32)
        acc = jnp.zeros((MLA_V, tq), F32)
        s = logits(chunks[0])
        for c, (start, size) in enumerate(chunks):
            s_next = logits(chunks[c + 1]) if c + 1 < len(chunks) else None
            p = jnp.exp2(s - r)
            l = l + jnp.sum(p, axis=0, keepdims=True)
            acc = acc + _dot(vt_ref[:, start:start + size], p.astype(BF16))
            s = s_next
        return l, acc

    def online():
        carry = (jnp.full((1, tq), NEG_INF, F32), jnp.zeros((1, tq), F32),
                 jnp.zeros((MLA_V, tq), F32))
        s = logits(chunks[0])
        for c, (start, size) in enumerate(chunks):
            s_next = logits(chunks[c + 1]) if c + 1 < len(chunks) else None
            carry = _softmax_update(s, vt_ref[:, start:start + size], carry)
            s = s_next
        return carry[1], carry[2]

    if ksq_ref is None:
        l, acc = online()
    else:
        qf = q.astype(F32)
        ones = jnp.ones((8, q.shape[1]), BF16)
        q_sq = _dot_nt(ones, (qf * qf).astype(BF16))[0:1]
        k_sq = jnp.max(ksq_ref[...], axis=0, keepdims=True)[:, 0:1]
        bound = jnp.sqrt(q_sq * k_sq) * SOFTMAX_BOUND_SLACK
        l, acc = single_pass(bound - SOFTMAX_HEADROOM_LOG2)
        l, acc = lax.cond(jnp.min(l) >= SOFTMAX_MIN_ROW_SUM, lambda: (l, acc), online)
    o_ref[...] = (acc / l).T.astype(BF16)


def _mla_attn_ctx_kernel(q_ref, k_ref, vt_ref, latent_out_ref, o_ref):
    del latent_out_ref
    _mla_attn_kernel(q_ref, k_ref, vt_ref, None, o_ref, chunks=((0, CTX_LEN),))


def _mla_attention(q, k, vt, ksq):
    hv = MLA_HEADS * MLA_V
    chunks = ((SEQ, CTX_LEN),) + tuple((c * MLA_TK, MLA_TK) for c in range(SEQ // MLA_TK))
    o = pl.pallas_call(
        functools.partial(_mla_attn_kernel, chunks=chunks),
        grid=(BATCH, MLA_HEADS, SEQ // MLA_TQ),
        in_specs=[
            pl.BlockSpec((None, MLA_TQ, MLA_QK_PAD), lambda b, h, i: (b, i, h)),
            pl.BlockSpec((None, TOKENS, MLA_QK_PAD), lambda b, h, i: (b, 0, h)),
            pl.BlockSpec((None, MLA_V, TOKENS), lambda b, h, i: (b, h, 0)),
            pl.BlockSpec((None, None, N_TILES, LANES), lambda b, h, i: (b, h, 0, 0)),
        ],
        out_specs=pl.BlockSpec((None, MLA_TQ, MLA_V), lambda b, h, i: (b, i, h)),
        out_shape=jax.ShapeDtypeStruct((BATCH, TOKENS, hv), BF16),
        compiler_params=_cparams(3),
        name="mla_attn",
    )(q, k, vt, ksq)
    ctx_blk = SEQ // CTX_LEN
    return pl.pallas_call(
        _mla_attn_ctx_kernel,
        grid=(BATCH, MLA_HEADS),
        in_specs=[
            pl.BlockSpec((None, CTX_LEN, MLA_QK_PAD), lambda b, h: (b, ctx_blk, h)),
            pl.BlockSpec((None, CTX_LEN, MLA_QK_PAD), lambda b, h: (b, ctx_blk, h)),
            pl.BlockSpec((None, MLA_V, CTX_LEN), lambda b, h: (b, h, ctx_blk)),
            pl.BlockSpec(memory_space=pl.ANY),
        ],
        out_specs=pl.BlockSpec((None, CTX_LEN, MLA_V), lambda b, h: (b, ctx_blk, h)),
        out_shape=jax.ShapeDtypeStruct((BATCH, TOKENS, hv), BF16),
        input_output_aliases={3: 0},
        compiler_params=_cparams(2),
        name="mla_attn_ctx",
    )(q, k, vt, o)


def _group_sq_norms(xb):
    w = xb.shape[1]
    xf = xb.astype(F32)
    group = lax.broadcasted_iota(jnp.int32, (w, LANES), 0) // NA_HEAD_DIM
    col = lax.broadcasted_iota(jnp.int32, (w, LANES), 1)
    indicator = jnp.where(group == col, 1.0, 0.0).astype(BF16)
    return _dot((xf * xf).astype(BF16), indicator)


def _store_norms(qb, kb, qsq_ref, ksq_ref):
    qsq_ref[...] = _group_sq_norms(qb).T[:NA_HEADS, :]
    k_max = jnp.max(_group_sq_norms(kb), axis=0, keepdims=True)
    ksq_ref[...] = jnp.broadcast_to(k_max, (8, LANES))


def _norm_specs():
    return ([pl.BlockSpec((None, NA_HEADS, TILE), lambda b, j: (b, 0, j)),
             pl.BlockSpec((None, None, 8, LANES), lambda b, j: (b, j, 0, 0))],
            [jax.ShapeDtypeStruct((BATCH, NA_HEADS, TOKENS), F32),
             jax.ShapeDtypeStruct((BATCH, N_TILES, 8, LANES), F32)])


def _na_proj_kernel(x_ref, mod_ref, g_ref, w_ref, qk_ref, vt_ref, qsq_ref, ksq_ref, *, scale):
    h = _pre_mix(x_ref, mod_ref, g_ref)
    a = _dot(h, w_ref[...])
    hd = NA_HEADS * NA_HEAD_DIM
    qb = (a[:, :hd] * scale).astype(BF16)
    kb = a[:, hd:2 * hd].astype(BF16)
    qk_ref[:, :hd] = qb
    qk_ref[:, hd:] = kb
    vt_ref[...] = a[:, 2 * hd:].T.astype(BF16)
    _store_norms(qb, kb, qsq_ref, ksq_ref)


def _na_proj(x, mod, g_pre, w):
    hd = NA_HEADS * NA_HEAD_DIM
    norm_specs, norm_shapes = _norm_specs()
    return pl.pallas_call(
        functools.partial(_na_proj_kernel, scale=float(NA_HEAD_DIM ** -0.5) * LOG2E),
        grid=(BATCH, N_TILES),
        in_specs=[_x_spec(), _mod_spec(), _const_spec((1, D_MODEL)), _const_spec(w.shape)],
        out_specs=[pl.BlockSpec((None, TILE, 2 * hd), lambda b, j: (b, j, 0)),
                   pl.BlockSpec((None, hd, TILE), lambda b, j: (b, 0, j))] + norm_specs,
        out_shape=[jax.ShapeDtypeStruct((BATCH, TOKENS, 2 * hd), BF16),
                   jax.ShapeDtypeStruct((BATCH, hd, TOKENS), BF16)] + norm_shapes,
        compiler_params=_cparams(2),
        name="na_proj",
    )(x, mod, g_pre, w)


def _swa_proj_kernel(x_ref, mod_ref, g_ref, cos_ref, sin_ref, w_ref, q_ref, k_ref, vt_ref,
                     qsq_ref, ksq_ref, *, scale):
    h = _pre_mix(x_ref, mod_ref, g_ref)
    a = _dot(h, w_ref[...])
    cos = cos_ref[...]
    sin = sin_ref[...]
    hd = SWA_HEADS * SWA_HEAD_DIM
    for p in range(HEAD_PAIRS):
        lo, hi = p * LANES, (p + 1) * LANES
        q = a[:, lo:hi] * cos + a[:, hd + lo:hd + hi] * sin
        q_ref[:, lo:hi] = (q * scale).astype(BF16)
    kw = 2 * LANES
    for p in range(2):
        lo, hi = p * LANES, (p + 1) * LANES
        k = a[:, 2 * hd + lo:2 * hd + hi] * cos + a[:, 2 * hd + kw + lo:2 * hd + kw + hi] * sin
        k_ref[:, lo:hi] = k.astype(BF16)
    vt_ref[...] = a[:, 2 * hd + 2 * kw:].T.astype(BF16)
    _store_norms(q_ref[...], k_ref[...], qsq_ref, ksq_ref)


def _swa_proj(x, mod, g_pre, cos, sin, w):
    hd = SWA_HEADS * SWA_HEAD_DIM
    kw = 2 * LANES
    norm_specs, norm_shapes = _norm_specs()
    return pl.pallas_call(
        functools.partial(_swa_proj_kernel, scale=float(SWA_HEAD_DIM ** -0.5) * LOG2E),
        grid=(BATCH, N_TILES),
        in_specs=[_x_spec(), _mod_spec(), _const_spec((1, D_MODEL)), _rope_spec(), _rope_spec(),
                  _const_spec(w.shape)],
        out_specs=[pl.BlockSpec((None, TILE, hd), lambda b, j: (b, j, 0)),
                   pl.BlockSpec((None, TILE, kw), lambda b, j: (b, j, 0)),
                   pl.BlockSpec((None, kw, TILE), lambda b, j: (b, 0, j))] + norm_specs,
        out_shape=[jax.ShapeDtypeStruct((BATCH, TOKENS, hd), BF16),
                   jax.ShapeDtypeStruct((BATCH, TOKENS, kw), BF16),
                   jax.ShapeDtypeStruct((BATCH, kw, TOKENS), BF16)] + norm_shapes,
        compiler_params=_cparams(2),
        name="swa_proj",
    )(x, mod, g_pre, cos, sin, w)


def _local_attn_kernel(*refs, n_nb, kv_slot, k_norm_lane, per_head_table, has_sink):
    q_ref = refs[0]
    k_refs = refs[1:1 + n_nb]
    kc_ref = refs[1 + n_nb]
    vt_refs = refs[2 + n_nb:2 + 2 * n_nb]
    vtc_ref = refs[2 + 2 * n_nb]
    tbl_ref, head_ref, qsq_ref, ksq_ref, o_ref = refs[3 + 2 * n_nb:]
    hdim = NA_HEAD_DIM
    low = lax.broadcasted_iota(jnp.int32, (TILE, LANES), 1) < hdim

    def pair_logits(p):
        qp = q_ref[:, p * LANES:(p + 1) * LANES]
        zero = jnp.zeros_like(qp)
        q2 = jnp.concatenate([jnp.where(low, qp, zero), jnp.where(low, zero, qp)], axis=0)
        ks = slice(kv_slot(p) * LANES, (kv_slot(p) + 1) * LANES)
        kn = jnp.concatenate([r[:, ks] for r in k_refs], axis=0)
        return _dot_nt(kn, q2), _dot_nt(kc_ref[:, ks], q2)

    def attend(reference):
        s = pair_logits(0)
        outs, l_min = [], None
        for p in range(HEAD_PAIRS):
            s_next = pair_logits(p + 1) if p + 1 < HEAD_PAIRS else None
            pair_out = []
            for half in range(2):
                hd = 2 * p + half
                cols = slice(half * TILE, (half + 1) * TILE)
                s_n = s[0][:, cols] + tbl_ref[hd if per_head_table else 0]
                s_c = s[1][:, cols]
                r = reference(hd, s_n, s_c)
                p_n = jnp.exp2(s_n - r).astype(BF16)
                p_c = jnp.exp2(s_c - r).astype(BF16)
                lo = kv_slot(p) * LANES + half * hdim
                vt_n = jnp.concatenate([v[lo:lo + hdim, :] for v in vt_refs], axis=1)
                vt_n = jnp.concatenate([vt_n, _ones_rows(vt_n.shape[1])], axis=0)
                vt_c = jnp.concatenate([vtc_ref[lo:lo + hdim, :], _ones_rows(CTX_LEN)], axis=0)
                acc = _dot(vt_n, p_n) + _dot(vt_c, p_c)
                l = acc[hdim:hdim + 1]
                if has_sink:
                    l = l + jnp.exp2(head_ref[hd:hd + 1, :] - r)
                l_min = l if l_min is None else jnp.minimum(l_min, l)
                pair_out.append(acc[:hdim] / l)
            outs.append(jnp.concatenate(pair_out, axis=0).T.astype(BF16))
            s = s_next
        return tuple(outs), l_min

    def row_max(hd, s_n, s_c):
        m = jnp.maximum(jnp.max(s_n, axis=0, keepdims=True), jnp.max(s_c, axis=0, keepdims=True))
        return jnp.maximum(m, head_ref[hd:hd + 1, :]) if has_sink else m

    k_sq = jnp.max(ksq_ref[...], axis=0)[0:1, :]

    def bound(hd, s_n, s_c):
        lane = k_norm_lane(hd)
        qk = jnp.sqrt(qsq_ref[hd:hd + 1, :] * k_sq[:, lane:lane + 1]) * SOFTMAX_BOUND_SLACK
        extra = head_ref[hd:hd + 1, :]
        top = jnp.maximum(qk, extra) if has_sink else qk + extra
        return top - SOFTMAX_HEADROOM_LOG2

    outs, l_min = attend(bound)
    outs = lax.cond(jnp.min(l_min) >= SOFTMAX_MIN_ROW_SUM, lambda: outs,
                    lambda: attend(row_max)[0])
    for p, pair in enumerate(outs):
        o_ref[:, p * LANES:(p + 1) * LANES] = pair


def _local_attention(q, q_col, k, k_col, k_width, vt, qsq, ksq, nb_tokens, n_nb, table,
                     head_rows, has_sink, kv_slot, k_norm_lane, name):
    hd = NA_HEADS * NA_HEAD_DIM
    per_head_table = table.shape[1] > 1
    lat_hi = N_LAT_TILES - 1
    per_tile = TILE // nb_tokens
    nb_hi = SEQ // nb_tokens - 1

    def nb_index(j, n):
        return jnp.clip(j * per_tile - 1 + n, 0, nb_hi)

    def k_spec(n):
        return pl.BlockSpec((None, nb_tokens, k_width), lambda b, j: (b, nb_index(j, n), k_col))

    def vt_spec(n):
        return pl.BlockSpec((None, k_width, nb_tokens), lambda b, j: (b, 0, nb_index(j, n)))

    def variant(j):
        return jnp.where(j == 0, 0, jnp.where(j == lat_hi, 2, jnp.where(j > lat_hi, 3, 1)))

    tbl_spec = pl.BlockSpec((None,) + table.shape[1:], lambda b, j: (variant(j), 0, 0, 0))
    in_specs = ([pl.BlockSpec((None, TILE, hd), lambda b, j: (b, j, q_col))]
                + [k_spec(n) for n in range(n_nb)]
                + [pl.BlockSpec((None, CTX_LEN, k_width), lambda b, j: (b, N_LAT_TILES, k_col))]
                + [vt_spec(n) for n in range(n_nb)]
                + [pl.BlockSpec((None, k_width, CTX_LEN), lambda b, j: (b, 0, N_LAT_TILES))]
                + [tbl_spec, _const_spec((NA_HEADS, TILE)),
                   pl.BlockSpec((None, NA_HEADS, TILE), lambda b, j: (b, 0, j)),
                   pl.BlockSpec((None, N_TILES, 8, LANES), lambda b, j: (b, 0, 0, 0))])
    return pl.pallas_call(
        functools.partial(_local_attn_kernel, n_nb=n_nb, kv_slot=kv_slot,
                          k_norm_lane=k_norm_lane, per_head_table=per_head_table,
                          has_sink=has_sink),
        grid=(BATCH, N_TILES),
        in_specs=in_specs,
        out_specs=pl.BlockSpec((None, TILE, hd), lambda b, j: (b, j, 0)),
        out_shape=jax.ShapeDtypeStruct((BATCH, TOKENS, hd), BF16),
        compiler_params=_cparams(2),
        name=name,
    )(q, *([k] * (n_nb + 1)), *([vt] * (n_nb + 1)), table, head_rows, qsq, ksq)


def _post_ffn_kernel(x_ref, oa_ref, ob_ref, mod_ref, gpm_ref, gpf_ref, gof_ref, wo_ref,
                     win_ref, wout_ref, y_ref, act_ref):
    gate_m = mod_ref[2:3, :]
    shift_f = mod_ref[3:4, :]
    scale_f = mod_ref[4:5, :]
    gate_f = mod_ref[5:6, :]
    rows = [slice(i * TILE, (i + 1) * TILE) for i in range(POST_TILES_PER_STEP)]
    mixed = [_dot(o_ref[...], wo_ref[...]) for o_ref in (oa_ref, ob_ref)]
    x1 = [x_ref[r, :] + gate_m * _rms(mx, gpm_ref[...]) for r, mx in zip(rows, mixed)]
    h = [(_rms(xi, gpf_ref[...]) * (1.0 + scale_f) + shift_f).astype(BF16) for xi in x1]
    chunk = 256
    for i, r in enumerate(rows):
        for f in range(D_FF // chunk):
            lo, hi = f * chunk, (f + 1) * chunk
            gate = _dot(h[i], win_ref[:, lo:hi])
            up = _dot(h[i], win_ref[:, D_FF + lo:D_FF + hi])
            act_ref[i, :, lo:hi] = (gate * jax.nn.sigmoid(gate) * up).astype(BF16)
        ffn = _dot(act_ref[i], wout_ref[...])
        y_ref[r, :] = x1[i] + gate_f * _rms(ffn, gof_ref[...])


def _post_ffn(x, o, mod, g_post_mix, g_pre_ffn, g_post_ffn, wo, win, wout, last):
    vec = _const_spec((1, D_MODEL))
    rows = POST_TILES_PER_STEP * TILE
    n_steps = SEQ // rows if last else TOKENS_PAD // rows
    ctx_step = N_LAT_TILES // POST_TILES_PER_STEP
    x_spec = pl.BlockSpec((None, rows, D_MODEL), lambda b, j: (b, j, 0))

    def o_spec(i):
        return pl.BlockSpec(
            (None, TILE, D_MODEL),
            lambda b, j: (b, jnp.minimum(POST_TILES_PER_STEP * j + i, N_LAT_TILES), 0))

    mod_spec = pl.BlockSpec((None, None, 6, D_MODEL),
                            lambda b, j: (b, (j == ctx_step).astype(jnp.int32), 0, 0))
    return pl.pallas_call(
        _post_ffn_kernel,
        grid=(BATCH, n_steps),
        in_specs=[x_spec, o_spec(0), o_spec(1), mod_spec, vec, vec, vec,
                  _const_spec(wo.shape), _const_spec(win.shape), _const_spec(wout.shape)],
        out_specs=x_spec,
        out_shape=jax.ShapeDtypeStruct((BATCH, n_steps * rows, D_MODEL), F32),
        scratch_shapes=[pltpu.VMEM((POST_TILES_PER_STEP, TILE, D_FF), BF16)],
        compiler_params=_cparams(2),
        name="post_ffn",
    )(x, o, o, mod, g_post_mix, g_pre_ffn, g_post_ffn, wo, win, wout)


def _rope_tables():
    t = np.arange(SEQ)
    row = (t // GRID_W).astype(np.float32)
    col = (t % GRID_W).astype(np.float32)
    n_freq = MLA_ROPE // 4
    inv = np.float32(ROPE_THETA) ** (-(np.arange(n_freq, dtype=np.float32) / np.float32(n_freq)))
    ang = np.concatenate([row[:, None] * inv, col[:, None] * inv], axis=-1)
    cos = np.cos(ang.astype(np.float64)).astype(np.float32)
    sin = np.sin(ang.astype(np.float64)).astype(np.float32)
    cos_t = np.concatenate([cos, cos, cos, cos], axis=-1)
    sin_t = np.concatenate([-sin, sin, -sin, sin], axis=-1)
    cos_t = np.concatenate([cos_t, np.ones((CTX_LEN, LANES), np.float32)], axis=0)
    sin_t = np.concatenate([sin_t, np.zeros((CTX_LEN, LANES), np.float32)], axis=0)
    return jnp.asarray(cos_t), jnp.asarray(sin_t)


NA_ROWS_PER_TILE = TILE // GRID_W
NA_BIAS_ROWS = 2 * NA_WIN_ROWS - 1
NA_BIAS_COLS = 2 * NA_WIN_COLS - 1
NA_NB_KEYS = NA_NB_BLOCKS * NA_NB_TOKENS


def _na_table_kernel(u_ref, o_ref):
    rows = SEQ // GRID_W
    kc = lax.broadcasted_iota(jnp.int32, (GRID_W, LANES), 0)
    lane = lax.broadcasted_iota(jnp.int32, (GRID_W, LANES), 1)
    qc = lane % GRID_W
    col_start = jnp.clip(qc - NA_WIN_COLS // 2, 0, GRID_W - NA_WIN_COLS)
    col_ok = (kc >= col_start) & (kc < col_start + NA_WIN_COLS)
    first_half = lane < GRID_W
    neg = jnp.full((GRID_W, LANES), NEG_INF, F32)
    toeplitz = {}
    for i in range(1, NA_BIAS_ROWS):
        row = jnp.broadcast_to(u_ref[i:i + 1, :], (GRID_W, LANES))
        toeplitz[i] = pltpu.roll(row, LANES - (NA_WIN_COLS - 1), 1, stride=1,
                                 stride_axis=0) * LOG2E
    for v, tile in enumerate((0, 1, N_LAT_TILES - 1)):
        for kl in range(NA_NB_BLOCKS * NA_ROWS_PER_TILE):
            kr = NA_ROWS_PER_TILE * (tile - 1) + kl
            for qp in range(NA_ROWS_PER_TILE // 2):
                oks = []
                for half in range(2):
                    qr = NA_ROWS_PER_TILE * tile + 2 * qp + half
                    row_start = min(max(qr - NA_WIN_ROWS // 2, 0), rows - NA_WIN_ROWS)
                    oks.append(row_start <= kr < row_start + NA_WIN_ROWS)
                if oks[0] and oks[1]:
                    ok = col_ok
                elif oks[0]:
                    ok = col_ok & first_half
                elif oks[1]:
                    ok = col_ok & jnp.logical_not(first_half)
                else:
                    ok = None
                i = kl - NA_ROWS_PER_TILE - 2 * qp + NA_WIN_ROWS - 1
                blk = neg if ok is None else jnp.where(ok, toeplitz[i], neg)
                o_ref[v, kl * GRID_W:(kl + 1) * GRID_W, qp * LANES:(qp + 1) * LANES] = blk
    o_ref[N_TABLE_VARIANTS - 1] = jnp.full((NA_NB_KEYS, TILE), NEG_INF, F32)


def _na_table(rpb):
    pad = jnp.full((NA_HEADS, NA_BIAS_ROWS, GRID_W - NA_BIAS_COLS), NEG_INF, F32)
    rev = jnp.concatenate([rpb.astype(F32)[:, :, ::-1], pad], axis=-1)
    off = jnp.full((NA_HEADS, 1, GRID_W), NEG_INF, F32)
    left = jnp.concatenate([rev, off], axis=1)
    right = jnp.concatenate([off, rev], axis=1)
    u = jnp.concatenate([left, right], axis=-1)
    return pl.pallas_call(
        _na_table_kernel,
        grid=(NA_HEADS,),
        in_specs=[pl.BlockSpec((None, NA_BIAS_ROWS + 1, LANES), lambda h: (h, 0, 0))],
        out_specs=pl.BlockSpec((N_TABLE_VARIANTS, None, NA_NB_KEYS, TILE),
                               lambda h: (0, h, 0, 0)),
        out_shape=jax.ShapeDtypeStruct((N_TABLE_VARIANTS, NA_HEADS, NA_NB_KEYS, TILE), F32),
        compiler_params=_cparams(1),
        name="na_table",
    )(u)


def _swa_table():
    j = np.array([0, 1, N_LAT_TILES - 1])[:, None, None]
    qpos = TILE * j + np.arange(TILE)[None, None, :]
    kpos = TILE * j - SWA_NB_TOKENS + np.arange(SWA_NB_BLOCKS * SWA_NB_TOKENS)[None, :, None]
    ok = (np.abs(kpos - qpos) <= SWA_WINDOW) & (kpos >= 0) & (kpos < SEQ)
    tbl = np.where(ok, 0.0, NEG_INF).astype(np.float32)
    tbl = np.concatenate([tbl, np.full((1,) + tbl.shape[1:], NEG_INF, np.float32)], axis=0)
    return jnp.asarray(tbl[:, None])


def _swap_halves(w):
    half = w.shape[-1] // 2
    return jnp.concatenate([w[..., half:], w[..., :half]], axis=-1)


def _mla_weights(w_in, w_uq, w_ukv):
    lat = MLA_Q_LORA + MLA_KV_LORA
    kpe = w_in[:, lat:]
    z = jnp.zeros((D_MODEL, LANES - MLA_ROPE), w_in.dtype)
    w_in_x = jnp.concatenate([w_in[:, :lat], kpe, z, _swap_halves(kpe), z], axis=1)
    uq = w_uq.reshape(MLA_Q_LORA, MLA_HEADS, MLA_NOPE + MLA_ROPE)
    rope = uq[:, :, MLA_NOPE:]
    zq = jnp.zeros((MLA_Q_LORA, MLA_HEADS, LANES - MLA_ROPE), w_uq.dtype)
    flat = lambda a: a.reshape(a.shape[0], -1)
    wq_x = jnp.concatenate([flat(uq[:, :, :MLA_NOPE]),
                            flat(jnp.concatenate([rope, zq], axis=-1)),
                            flat(jnp.concatenate([_swap_halves(rope), zq], axis=-1))], axis=1)
    ukv = w_ukv.reshape(MLA_KV_LORA, MLA_HEADS, MLA_NOPE + MLA_V)
    wkv_x = jnp.concatenate([flat(ukv[:, :, :MLA_NOPE]), flat(ukv[:, :, MLA_NOPE:])], axis=1)
    return w_in_x.astype(BF16), wq_x.astype(BF16), wkv_x.astype(BF16)


def _swa_weights(w_qkv):
    hd = SWA_HEADS * SWA_HEAD_DIM
    kd = SWA_KV_HEADS * SWA_HEAD_DIM
    wq = w_qkv[:, :hd].reshape(D_MODEL, SWA_HEADS, SWA_HEAD_DIM)
    wk = w_qkv[:, hd:hd + kd].reshape(D_MODEL, SWA_KV_HEADS, 1, SWA_HEAD_DIM)
    wv = w_qkv[:, hd + kd:].reshape(D_MODEL, SWA_KV_HEADS, 1, SWA_HEAD_DIM)
    dup = lambda a: jnp.broadcast_to(a, (D_MODEL, SWA_KV_HEADS, 2, SWA_HEAD_DIM)).reshape(
        D_MODEL, -1)
    flat = lambda a: a.reshape(D_MODEL, -1)
    w = jnp.concatenate([flat(wq), flat(_swap_halves(wq)), dup(wk), dup(_swap_halves(wk)),
                         dup(wv)], axis=1)
    return w.astype(BF16)


def kernel(x, c, ctx, c_ctx, w_mod, b_mod, g_pre_mix, g_post_mix, g_pre_ffn, g_post_ffn,
           w_ffn_in, w_ffn_out, mla_w_in, mla_g_q, mla_w_uq, mla_g_kv, mla_w_ukv, mla_w_o,
           na_w_qkv, na_rpb, na_w_o, swa_w_qkv, swa_sink, swa_w_o):
    assert x.shape == (BATCH, SEQ, D_MODEL) and ctx.shape == (BATCH, CTX_LEN, D_MODEL)
    pad = jnp.zeros((BATCH, TOKENS_PAD - TOKENS, D_MODEL), F32)
    stream = jnp.concatenate([x, ctx, pad], axis=1)
    cc = jnp.concatenate([c, c_ctx[None], jnp.zeros((8 - BATCH - 1, D_MODEL), F32)], axis=0)
    mods = _modulation(cc, w_mod, b_mod)
    mods = mods.reshape(DEPTH, 8, 6, D_MODEL)
    mods = jnp.stack([mods[:, :BATCH],
                      jnp.broadcast_to(mods[:, BATCH:BATCH + 1], (DEPTH, BATCH, 6, D_MODEL))],
                     axis=2)
    cos_t, sin_t = _rope_tables()
    row = lambda g: g.reshape(1, -1)

    for i in range(DEPTH):
        last = i == DEPTH - 1
        kind, j = i % 3, i // 3
        mod = mods[i]
        g_pre = row(g_pre_mix[i])
        if kind == 0:
            w_in_x, wq_x, wkv_x = _mla_weights(mla_w_in[j], mla_w_uq[j], mla_w_ukv[j])
            q, k, vt, ksq = _mla_proj(stream, mod, g_pre, cos_t, sin_t, w_in_x,
                                      row(mla_g_q[j]), row(mla_g_kv[j]), wq_x, wkv_x)
            o = _mla_attention(q, k, vt, jnp.swapaxes(ksq, 1, 2))
            w_o = mla_w_o[j]
        elif kind == 1:
            qk, vt, qsq, ksq = _na_proj(stream, mod, g_pre, na_w_qkv[j].astype(BF16))
            bias_top = jnp.maximum(jnp.max(na_rpb[j].astype(F32), axis=(1, 2)), 0.0) * LOG2E
            o = _local_attention(qk, 0, qk, 1, NA_HEADS * NA_HEAD_DIM, vt, qsq, ksq,
                                 NA_NB_TOKENS, NA_NB_BLOCKS, _na_table(na_rpb[j]),
                                 jnp.broadcast_to(bias_top[:, None], (NA_HEADS, TILE)), False,
                                 lambda p: p, lambda hd: hd, "na_attn")
            w_o = na_w_o[j]
        else:
            q, k, vt, qsq, ksq = _swa_proj(stream, mod, g_pre, cos_t, sin_t,
                                           _swa_weights(swa_w_qkv[j]))
            sink = jnp.broadcast_to((swa_sink[j].astype(F32) * LOG2E)[:, None],
                                    (SWA_HEADS, TILE))
            heads_per_kv = SWA_HEADS // SWA_KV_HEADS
            o = _local_attention(q, 0, k, 0, 2 * LANES, vt, qsq, ksq, SWA_NB_TOKENS,
                                 SWA_NB_BLOCKS, _swa_table(), sink, True,
                                 lambda p: 2 * p // heads_per_kv,
                                 lambda hd: 2 * (hd // heads_per_kv), "swa_attn")
            w_o = swa_w_o[j]
        stream = _post_ffn(stream, o, mod, row(g_post_mix[i]), row(g_pre_ffn[i]),
                           row(g_post_ffn[i]), w_o.astype(BF16), w_ffn_in[i].astype(BF16),
                           w_ffn_out[i].astype(BF16), last)
    return stream
```

```python
import functools

import jax
import jax.numpy as jnp
import numpy as np
from jax import lax
from jax.experimental import pallas as pl
from jax.experimental.pallas import tpu as pltpu

D_MODEL = 1024
BATCH = 4
SEQ = 8192
DEPTH = 4
GRID_W = 64
CTX_LEN = 256
D_FF = 2816
NORM_EPS = 1e-6
ROPE_THETA = 10000.0
NEG_INF = -1e30

MLA_HEADS = 8
MLA_Q_LORA = 256
MLA_KV_LORA = 256
MLA_NOPE = 128
MLA_ROPE = 64
MLA_V = 128
MLA_QK_PAD = 256

NA_HEADS = 16
NA_HEAD_DIM = 64
NA_WIN_ROWS = 8
NA_WIN_COLS = 16

SWA_HEADS = 16
SWA_KV_HEADS = 2
SWA_HEAD_DIM = 64
SWA_WINDOW = 128

TOKENS = SEQ + CTX_LEN
TILE = 256
N_TILES = TOKENS // TILE
N_LAT_TILES = SEQ // TILE
POST_TILES_PER_STEP = 2
STEP_ROWS = POST_TILES_PER_STEP * TILE
N_STEPS = -(-N_TILES // POST_TILES_PER_STEP)
CTX_STEP = N_LAT_TILES // POST_TILES_PER_STEP
TOKENS_PAD = N_STEPS * STEP_ROWS
LANES = 128
BF16_ROWS = 16
HEAD_PAIRS = 8
N_TABLE_VARIANTS = 4

MLA_TQ = 1024
MLA_TK = 2048
SOFTMAX_HEADROOM_LOG2 = 64.0
SOFTMAX_MIN_ROW_SUM = 2.0 ** -40
SOFTMAX_BOUND_SLACK = 1.02

NA_Q_ROWS = TILE
NA_NB_TOKENS = TILE
NA_NB_BLOCKS = NA_Q_ROWS // NA_NB_TOKENS + 2
SWA_Q_ROWS = TILE
SWA_NB_TOKENS = SWA_WINDOW
SWA_NB_BLOCKS = SWA_Q_ROWS // SWA_NB_TOKENS + 2

VMEM_LIMIT = 56 * 1024 * 1024

LOG2E = 1.4426950408889634

BF16 = jnp.bfloat16
F32 = jnp.float32


def _cparams(n_axes):
    return pltpu.CompilerParams(
        dimension_semantics=("parallel",) * n_axes, vmem_limit_bytes=VMEM_LIMIT)


def _const_spec(shape):
    nd = len(shape)
    return pl.BlockSpec(shape, lambda *_: (0,) * nd, pipeline_mode=pl.Buffered(1))


def _rms(x, g):
    return x * lax.rsqrt(jnp.mean(x * x, axis=-1, keepdims=True) + NORM_EPS) * g


def _dot(a, b):
    return jnp.dot(a, b, preferred_element_type=F32)


def _dot_nt(a, b):
    return lax.dot_general(a, b, (((1,), (1,)), ((), ())), preferred_element_type=F32)


def _ones_rows(n):
    row = lax.broadcasted_iota(jnp.int32, (BF16_ROWS, n), 0)
    return jnp.where(row == 0, 1.0, 0.0).astype(BF16)


def _mod_kernel(c_ref, w_ref, b_ref, o_ref):
    c = c_ref[...]
    a = c * jax.nn.sigmoid(c)
    o_ref[...] = _dot(a, w_ref[...]) + b_ref[...]


def _modulation(cc, w_mod, b_mod):
    tn = 1536
    return pl.pallas_call(
        _mod_kernel,
        grid=(DEPTH, 6 * D_MODEL // tn),
        in_specs=[
            pl.BlockSpec((8, D_MODEL), lambda l, n: (0, 0)),
            pl.BlockSpec((None, D_MODEL, tn), lambda l, n: (l, 0, n)),
            pl.BlockSpec((None, 1, tn), lambda l, n: (l, 0, n)),
        ],
        out_specs=pl.BlockSpec((None, 8, tn), lambda l, n: (l, 0, n)),
        out_shape=jax.ShapeDtypeStruct((DEPTH, 8, 6 * D_MODEL), F32),
        compiler_params=_cparams(2),
        name="modulation",
    )(cc, w_mod, b_mod.reshape(DEPTH, 1, 6 * D_MODEL))


def _mod_spec():
    return pl.BlockSpec((None, None, 6, D_MODEL),
                        lambda b, j: (b, (j == CTX_STEP).astype(jnp.int32), 0, 0))


def _x_spec():
    return pl.BlockSpec((None, STEP_ROWS, D_MODEL), lambda b, j: (b, j, 0))


def _rope_spec():
    return pl.BlockSpec((STEP_ROWS, LANES), lambda b, j: (j, 0))


def _stream_operands(stream):
    if not isinstance(stream, tuple):
        return [stream], [_x_spec()]
    return list(stream), [
        pl.BlockSpec((None, STEP_ROWS, D_MODEL),
                     lambda b, j: (b, jnp.minimum(j, CTX_STEP - 1), 0)),
        pl.BlockSpec((None, CTX_LEN, D_MODEL), lambda b, j: (b, 0, 0))]


def _stream_rows(refs):
    if len(refs) == 1:
        return refs[0][...]
    x_ref, ctx_ref = refs
    ctx_rows = jnp.concatenate([ctx_ref[...]] * (STEP_ROWS // CTX_LEN), axis=0)
    return jnp.where(pl.program_id(1) == CTX_STEP, ctx_rows, x_ref[...])


def _pre_mix(x, mod_ref, g_ref):
    shift = mod_ref[0:1, :]
    scale = mod_ref[1:2, :]
    h = _rms(x, g_ref[...]) * (1.0 + scale) + shift
    return h.astype(BF16)


def _mla_proj_kernel(*refs, n_stream, scale):
    stream_refs = refs[:n_stream]
    (mod_ref, g_ref, cos_ref, sin_ref, w_in_ref, gq_ref, gkv_ref, wq_ref, wkv_ref,
     q_ref, k_ref, vt_ref, ksq_ref) = refs[n_stream:]
    h = _pre_mix(_stream_rows(stream_refs), mod_ref, g_ref)
    a = _dot(h, w_in_ref[...])
    qn = _rms(a[:, :MLA_Q_LORA], gq_ref[...]).astype(BF16)
    kvn = _rms(a[:, MLA_Q_LORA:MLA_Q_LORA + MLA_KV_LORA], gkv_ref[...]).astype(BF16)
    cos = cos_ref[...]
    sin = sin_ref[...]
    k_rope = (a[:, 512:640] * cos + a[:, 640:768] * sin).astype(BF16)
    qq = _dot(qn, wq_ref[...])
    kv = _dot(kvn, wkv_ref[...])
    hn = MLA_HEADS * LANES
    k_rope_f = k_rope.astype(F32)
    rope_sq = jnp.sum(k_rope_f * k_rope_f, axis=-1, keepdims=True)
    for hd in range(MLA_HEADS):
        lo, hi = hd * LANES, (hd + 1) * LANES
        base = hd * MLA_QK_PAD
        q_ref[:, base:base + LANES] = (qq[:, lo:hi] * scale).astype(BF16)
        q_rope = qq[:, hn + lo:hn + hi] * cos + qq[:, 2 * hn + lo:2 * hn + hi] * sin
        q_ref[:, base + LANES:base + 2 * LANES] = (q_rope * scale).astype(BF16)
        k_nope = kv[:, lo:hi].astype(BF16)
        k_ref[:, base:base + LANES] = k_nope
        k_ref[:, base + LANES:base + 2 * LANES] = k_rope
        k_nope_f = k_nope.astype(F32)
        k_sq = jnp.sum(k_nope_f * k_nope_f, axis=-1, keepdims=True) + rope_sq
        ksq_ref[hd:hd + 1, :] = jnp.broadcast_to(jnp.max(k_sq, axis=0, keepdims=True),
                                                 (1, LANES))
    vt_ref[...] = kv[:, hn:].T.astype(BF16)


def _mla_proj(stream, mod, g_pre, cos, sin, w_in, g_q, g_kv, wq, wkv):
    hq = MLA_HEADS * MLA_QK_PAD
    hv = MLA_HEADS * MLA_V
    out_spec_q = pl.BlockSpec((None, STEP_ROWS, hq), lambda b, j: (b, j, 0))
    out_spec_v = pl.BlockSpec((None, hv, STEP_ROWS), lambda b, j: (b, 0, j))
    scale = float((MLA_NOPE + MLA_ROPE) ** -0.5) * LOG2E
    stream_arrays, stream_specs = _stream_operands(stream)
    return pl.pallas_call(
        functools.partial(_mla_proj_kernel, n_stream=len(stream_arrays), scale=scale),
        grid=(BATCH, N_STEPS),
        in_specs=stream_specs + [
            _mod_spec(), _const_spec((1, D_MODEL)), _rope_spec(), _rope_spec(),
            _const_spec(w_in.shape), _const_spec((1, MLA_Q_LORA)),
            _const_spec((1, MLA_KV_LORA)), _const_spec(wq.shape), _const_spec(wkv.shape)],
        out_specs=[out_spec_q, out_spec_q, out_spec_v,
                   pl.BlockSpec((None, None, MLA_HEADS, LANES), lambda b, j: (b, j, 0, 0))],
        out_shape=[jax.ShapeDtypeStruct((BATCH, TOKENS_PAD, hq), BF16),
                   jax.ShapeDtypeStruct((BATCH, TOKENS_PAD, hq), BF16),
                   jax.ShapeDtypeStruct((BATCH, hv, TOKENS_PAD), BF16),
                   jax.ShapeDtypeStruct((BATCH, N_STEPS, MLA_HEADS, LANES), F32)],
        compiler_params=_cparams(2),
        name="mla_proj",
    )(*stream_arrays, mod, g_pre, cos, sin, w_in, g_q, g_kv, wq, wkv)


def _softmax_update(s, vt, carry):
    m, l, acc = carry
    m_new = jnp.maximum(m, jnp.max(s, axis=0, keepdims=True))
    alpha = jnp.exp2(m - m_new)
    p = jnp.exp2(s - m_new)
    l = alpha * l + jnp.sum(p, axis=0, keepdims=True)
    acc = alpha * acc + _dot(vt, p.astype(BF16))
    return m_new, l, acc


def _mla_attn_kernel(q_ref, k_ref, vt_ref, ksq_ref, o_ref, *, chunks):
    q = q_ref[...]
    tq = q.shape[0]

    def logits(chunk):
        start, size = chunk
        return _dot_nt(k_ref[start:start + size, :], q)

    def single_pass(r):
        l = jnp.zeros((1, tq), F32)
        acc = jnp.zeros((MLA_V, tq), F32)
        s = logits(chunks[0])
        for c, (start, size) in enumerate(chunks):
            s_next = logits(chunks[c + 1]) if c + 1 < len(chunks) else None
            p = jnp.exp2(s - r)
            l = l + jnp.sum(p, axis=0, keepdims=True)
            acc = acc + _dot(vt_ref[:, start:start + size], p.astype(BF16))
            s = s_next
        return l, acc

    def online():
        carry = (jnp.full((1, tq), NEG_INF, F32), jnp.zeros((1, tq), F32),
                 jnp.zeros((MLA_V, tq), F32))
        s = logits(chunks[0])
        for c, (start, size) in enumerate(chunks):
            s_next = logits(chunks[c + 1]) if c + 1 < len(chunks) else None
            carry = _softmax_update(s, vt_ref[:, start:start + size], carry)
            s = s_next
        return carry[1], carry[2]

    if ksq_ref is None:
        l, acc = online()
    else:
        qf = q.astype(F32)
        ones = jnp.ones((8, q.shape[1]), BF16)
        q_sq = _dot_nt(ones, (qf * qf).astype(BF16))[0:1]
        k_sq = jnp.max(ksq_ref[...], axis=0, keepdims=True)[:, 0:1]
        bound = jnp.sqrt(q_sq * k_sq) * SOFTMAX_BOUND_SLACK
        l, acc = single_pass(bound - SOFTMAX_HEADROOM_LOG2)
        l, acc = lax.cond(jnp.min(l) >= SOFTMAX_MIN_ROW_SUM, lambda: (l, acc), online)
    o_ref[...] = (acc / l).T.astype(BF16)


def _mla_attn_ctx_kernel(q_ref, k_ref, vt_ref, latent_out_ref, o_ref):
    del latent_out_ref
    _mla_attn_kernel(q_ref, k_ref, vt_ref, None, o_ref, chunks=((0, CTX_LEN),))


def _mla_attention(q, k, vt, ksq):
    hv = MLA_HEADS * MLA_V
    chunks = ((SEQ, CTX_LEN),) + tuple((c * MLA_TK, MLA_TK) for c in range(SEQ // MLA_TK))
    o = pl.pallas_call(
        functools.partial(_mla_attn_kernel, chunks=chunks),
        grid=(BATCH, MLA_HEADS, SEQ // MLA_TQ),
        in_specs=[
            pl.BlockSpec((None, MLA_TQ, MLA_QK_PAD), lambda b, h, i: (b, i, h)),
            pl.BlockSpec((None, TOKENS, MLA_QK_PAD), lambda b, h, i: (b, 0, h)),
            pl.BlockSpec((None, MLA_V, TOKENS), lambda b, h, i: (b, h, 0)),
            pl.BlockSpec((None, None, N_STEPS, LANES), lambda b, h, i: (b, h, 0, 0)),
        ],
        out_specs=pl.BlockSpec((None, MLA_TQ, MLA_V), lambda b, h, i: (b, i, h)),
        out_shape=jax.ShapeDtypeStruct((BATCH, TOKENS, hv), BF16),
        compiler_params=_cparams(3),
        name="mla_attn",
    )(q, k, vt, ksq)
    ctx_blk = SEQ // CTX_LEN
    return pl.pallas_call(
        _mla_attn_ctx_kernel,
        grid=(BATCH, MLA_HEADS),
        in_specs=[
            pl.BlockSpec((None, CTX_LEN, MLA_QK_PAD), lambda b, h: (b, ctx_blk, h)),
            pl.BlockSpec((None, CTX_LEN, MLA_QK_PAD), lambda b, h: (b, ctx_blk, h)),
            pl.BlockSpec((None, MLA_V, CTX_LEN), lambda b, h: (b, h, ctx_blk)),
            pl.BlockSpec(memory_space=pl.ANY),
        ],
        out_specs=pl.BlockSpec((None, CTX_LEN, MLA_V), lambda b, h: (b, ctx_blk, h)),
        out_shape=jax.ShapeDtypeStruct((BATCH, TOKENS, hv), BF16),
        input_output_aliases={3: 0},
        compiler_params=_cparams(2),
        name="mla_attn_ctx",
    )(q, k, vt, o)


def _group_sq_norms(xb):
    w = xb.shape[1]
    xf = xb.astype(F32)
    group = lax.broadcasted_iota(jnp.int32, (w, LANES), 0) // NA_HEAD_DIM
    col = lax.broadcasted_iota(jnp.int32, (w, LANES), 1)
    indicator = jnp.where(group == col, 1.0, 0.0).astype(BF16)
    return _dot((xf * xf).astype(BF16), indicator)


def _store_norms(qb, kb, qsq_ref, ksq_ref):
    qsq_ref[...] = _group_sq_norms(qb).T[:NA_HEADS, :]
    k_max = jnp.max(_group_sq_norms(kb), axis=0, keepdims=True)
    ksq_ref[...] = jnp.broadcast_to(k_max, (8, LANES))


def _norm_specs():
    return ([pl.BlockSpec((None, NA_HEADS, STEP_ROWS), lambda b, j: (b, 0, j)),
             pl.BlockSpec((None, None, 8, LANES), lambda b, j: (b, j, 0, 0))],
            [jax.ShapeDtypeStruct((BATCH, NA_HEADS, TOKENS_PAD), F32),
             jax.ShapeDtypeStruct((BATCH, N_STEPS, 8, LANES), F32)])


def _na_proj_kernel(x_ref, mod_ref, g_ref, w_ref, qk_ref, vt_ref, qsq_ref, ksq_ref, *, scale):
    h = _pre_mix(x_ref[...], mod_ref, g_ref)
    a = _dot(h, w_ref[...])
    hd = NA_HEADS * NA_HEAD_DIM
    qb = (a[:, :hd] * scale).astype(BF16)
    kb = a[:, hd:2 * hd].astype(BF16)
    qk_ref[:, :hd] = qb
    qk_ref[:, hd:] = kb
    vt_ref[...] = a[:, 2 * hd:].T.astype(BF16)
    _store_norms(qb, kb, qsq_ref, ksq_ref)


def _na_proj(x, mod, g_pre, w):
    hd = NA_HEADS * NA_HEAD_DIM
    norm_specs, norm_shapes = _norm_specs()
    return pl.pallas_call(
        functools.partial(_na_proj_kernel, scale=float(NA_HEAD_DIM ** -0.5) * LOG2E),
        grid=(BATCH, N_STEPS),
        in_specs=[_x_spec(), _mod_spec(), _const_spec((1, D_MODEL)), _const_spec(w.shape)],
        out_specs=[pl.BlockSpec((None, STEP_ROWS, 2 * hd), lambda b, j: (b, j, 0)),
                   pl.BlockSpec((None, hd, STEP_ROWS), lambda b, j: (b, 0, j))] + norm_specs,
        out_shape=[jax.ShapeDtypeStruct((BATCH, TOKENS_PAD, 2 * hd), BF16),
                   jax.ShapeDtypeStruct((BATCH, hd, TOKENS_PAD), BF16)] + norm_shapes,
        compiler_params=_cparams(2),
        name="na_proj",
    )(x, mod, g_pre, w)


def _swa_proj_kernel(x_ref, mod_ref, g_ref, cos_ref, sin_ref, w_ref, q_ref, k_ref, vt_ref,
                     qsq_ref, ksq_ref, *, scale):
    h = _pre_mix(x_ref[...], mod_ref, g_ref)
    a = _dot(h, w_ref[...])
    cos = cos_ref[...]
    sin = sin_ref[...]
    hd = SWA_HEADS * SWA_HEAD_DIM
    for p in range(HEAD_PAIRS):
        lo, hi = p * LANES, (p + 1) * LANES
        q = a[:, lo:hi] * cos + a[:, hd + lo:hd + hi] * sin
        q_ref[:, lo:hi] = (q * scale).astype(BF16)
    kw = 2 * LANES
    for p in range(2):
        lo, hi = p * LANES, (p + 1) * LANES
        k = a[:, 2 * hd + lo:2 * hd + hi] * cos + a[:, 2 * hd + kw + lo:2 * hd + kw + hi] * sin
        k_ref[:, lo:hi] = k.astype(BF16)
    vt_ref[...] = a[:, 2 * hd + 2 * kw:].T.astype(BF16)
    _store_norms(q_ref[...], k_ref[...], qsq_ref, ksq_ref)


def _swa_proj(x, mod, g_pre, cos, sin, w):
    hd = SWA_HEADS * SWA_HEAD_DIM
    kw = 2 * LANES
    norm_specs, norm_shapes = _norm_specs()
    return pl.pallas_call(
        functools.partial(_swa_proj_kernel, scale=float(SWA_HEAD_DIM ** -0.5) * LOG2E),
        grid=(BATCH, N_STEPS),
        in_specs=[_x_spec(), _mod_spec(), _const_spec((1, D_MODEL)), _rope_spec(), _rope_spec(),
                  _const_spec(w.shape)],
        out_specs=[pl.BlockSpec((None, STEP_ROWS, hd), lambda b, j: (b, j, 0)),
                   pl.BlockSpec((None, STEP_ROWS, kw), lambda b, j: (b, j, 0)),
                   pl.BlockSpec((None, kw, STEP_ROWS), lambda b, j: (b, 0, j))] + norm_specs,
        out_shape=[jax.ShapeDtypeStruct((BATCH, TOKENS_PAD, hd), BF16),
                   jax.ShapeDtypeStruct((BATCH, TOKENS_PAD, kw), BF16),
                   jax.ShapeDtypeStruct((BATCH, kw, TOKENS_PAD), BF16)] + norm_shapes,
        compiler_params=_cparams(2),
        name="swa_proj",
    )(x, mod, g_pre, cos, sin, w)


def _local_attn_kernel(*refs, n_nb, kv_slot, k_norm_lane, per_head_table, has_sink):
    q_ref = refs[0]
    k_refs = refs[1:1 + n_nb]
    kc_ref = refs[1 + n_nb]
    vt_refs = refs[2 + n_nb:2 + 2 * n_nb]
    vtc_ref = refs[2 + 2 * n_nb]
    tbl_ref, head_ref, qsq_ref, ksq_ref, o_ref = refs[3 + 2 * n_nb:]
    hdim = NA_HEAD_DIM
    tq = q_ref.shape[0]
    low = lax.broadcasted_iota(jnp.int32, (tq, LANES), 1) < hdim

    def pair_logits(p):
        qp = q_ref[:, p * LANES:(p + 1) * LANES]
        zero = jnp.zeros_like(qp)
        q2 = jnp.concatenate([jnp.where(low, qp, zero), jnp.where(low, zero, qp)], axis=0)
        ks = slice(kv_slot(p) * LANES, (kv_slot(p) + 1) * LANES)
        kn = jnp.concatenate([r[:, ks] for r in k_refs], axis=0)
        return _dot_nt(kn, q2), _dot_nt(kc_ref[:, ks], q2)

    def attend(reference):
        s = pair_logits(0)
        outs, l_min = [], None
        for p in range(HEAD_PAIRS):
            s_next = pair_logits(p + 1) if p + 1 < HEAD_PAIRS else None
            pair_out = []
            for half in range(2):
                hd = 2 * p + half
                cols = slice(half * tq, (half + 1) * tq)
                s_n = s[0][:, cols] + tbl_ref[hd if per_head_table else 0]
                s_c = s[1][:, cols]
                r = reference(hd, s_n, s_c)
                p_n = jnp.exp2(s_n - r).astype(BF16)
                p_c = jnp.exp2(s_c - r).astype(BF16)
                lo = kv_slot(p) * LANES + half * hdim
                vt_n = jnp.concatenate([v[lo:lo + hdim, :] for v in vt_refs], axis=1)
                vt_n = jnp.concatenate([vt_n, _ones_rows(vt_n.shape[1])], axis=0)
                vt_c = jnp.concatenate([vtc_ref[lo:lo + hdim, :], _ones_rows(CTX_LEN)], axis=0)
                acc = _dot(vt_n, p_n) + _dot(vt_c, p_c)
                l = acc[hdim:hdim + 1]
                if has_sink:
                    l = l + jnp.exp2(head_ref[hd:hd + 1, :] - r)
                l_min = l if l_min is None else jnp.minimum(l_min, l)
                pair_out.append(acc[:hdim] / l)
            outs.append(jnp.concatenate(pair_out, axis=0).T.astype(BF16))
            s = s_next
        return tuple(outs), l_min

    def row_max(hd, s_n, s_c):
        m = jnp.maximum(jnp.max(s_n, axis=0, keepdims=True), jnp.max(s_c, axis=0, keepdims=True))
        return jnp.maximum(m, head_ref[hd:hd + 1, :]) if has_sink else m

    k_sq = jnp.max(ksq_ref[...], axis=0)[0:1, :]

    def bound(hd, s_n, s_c):
        lane = k_norm_lane(hd)
        qk = jnp.sqrt(qsq_ref[hd:hd + 1, :] * k_sq[:, lane:lane + 1]) * SOFTMAX_BOUND_SLACK
        extra = head_ref[hd:hd + 1, :]
        top = jnp.maximum(qk, extra) if has_sink else qk + extra
        return top - SOFTMAX_HEADROOM_LOG2

    outs, l_min = attend(bound)
    outs = lax.cond(jnp.min(l_min) >= SOFTMAX_MIN_ROW_SUM, lambda: outs,
                    lambda: attend(row_max)[0])
    for p, pair in enumerate(outs):
        o_ref[:, p * LANES:(p + 1) * LANES] = pair


def _local_attention(q, q_col, k, k_col, k_width, vt, qsq, ksq, q_rows, nb_tokens, n_nb, table,
                     head_rows, has_sink, kv_slot, k_norm_lane, name):
    hd = NA_HEADS * NA_HEAD_DIM
    per_head_table = table.shape[1] > 1
    n_steps = -(-TOKENS // q_rows)
    lat_hi = SEQ // q_rows - 1
    per_tile = q_rows // nb_tokens
    nb_hi = SEQ // nb_tokens - 1

    def nb_index(j, n):
        return jnp.clip(j * per_tile - 1 + n, 0, nb_hi)

    def k_spec(n):
        return pl.BlockSpec((None, nb_tokens, k_width), lambda b, j: (b, nb_index(j, n), k_col))

    def vt_spec(n):
        return pl.BlockSpec((None, k_width, nb_tokens), lambda b, j: (b, 0, nb_index(j, n)))

    def variant(j):
        return jnp.where(j == 0, 0, jnp.where(j == lat_hi, 2, jnp.where(j > lat_hi, 3, 1)))

    tbl_spec = pl.BlockSpec((None,) + table.shape[1:], lambda b, j: (variant(j), 0, 0, 0))
    in_specs = ([pl.BlockSpec((None, q_rows, hd), lambda b, j: (b, j, q_col))]
                + [k_spec(n) for n in range(n_nb)]
                + [pl.BlockSpec((None, CTX_LEN, k_width), lambda b, j: (b, N_LAT_TILES, k_col))]
                + [vt_spec(n) for n in range(n_nb)]
                + [pl.BlockSpec((None, k_width, CTX_LEN), lambda b, j: (b, 0, N_LAT_TILES))]
                + [tbl_spec, _const_spec((NA_HEADS, q_rows)),
                   pl.BlockSpec((None, NA_HEADS, q_rows), lambda b, j: (b, 0, j)),
                   pl.BlockSpec((None, N_STEPS, 8, LANES), lambda b, j: (b, 0, 0, 0))])
    return pl.pallas_call(
        functools.partial(_local_attn_kernel, n_nb=n_nb, kv_slot=kv_slot,
                          k_norm_lane=k_norm_lane, per_head_table=per_head_table,
                          has_sink=has_sink),
        grid=(BATCH, n_steps),
        in_specs=in_specs,
        out_specs=pl.BlockSpec((None, q_rows, hd), lambda b, j: (b, j, 0)),
        out_shape=jax.ShapeDtypeStruct((BATCH, n_steps * q_rows, hd), BF16),
        compiler_params=_cparams(2),
        name=name,
    )(q, *([k] * (n_nb + 1)), *([vt] * (n_nb + 1)), table, head_rows, qsq, ksq)


def _post_ffn_kernel(*refs, n_stream):
    (oa_ref, ob_ref, mod_ref, gpm_ref, gpf_ref, gof_ref, wo_ref, win_ref, wout_ref,
     y_ref, act_ref) = refs[n_stream:]
    x = _stream_rows(refs[:n_stream])
    gate_m = mod_ref[2:3, :]
    shift_f = mod_ref[3:4, :]
    scale_f = mod_ref[4:5, :]
    gate_f = mod_ref[5:6, :]
    rows = [slice(i * TILE, (i + 1) * TILE) for i in range(POST_TILES_PER_STEP)]
    mixed = [_dot(o_ref[...], wo_ref[...]) for o_ref in (oa_ref, ob_ref)]
    x1 = [x[r, :] + gate_m * _rms(mx, gpm_ref[...]) for r, mx in zip(rows, mixed)]
    h = [(_rms(xi, gpf_ref[...]) * (1.0 + scale_f) + shift_f).astype(BF16) for xi in x1]
    chunk = 256
    for i, r in enumerate(rows):
        for f in range(D_FF // chunk):
            lo, hi = f * chunk, (f + 1) * chunk
            gate = _dot(h[i], win_ref[:, lo:hi])
            up = _dot(h[i], win_ref[:, D_FF + lo:D_FF + hi])
            act_ref[i, :, lo:hi] = (gate * jax.nn.sigmoid(gate) * up).astype(BF16)
        ffn = _dot(act_ref[i], wout_ref[...])
        y_ref[r, :] = x1[i] + gate_f * _rms(ffn, gof_ref[...])


def _post_ffn(stream, o, mod, g_post_mix, g_pre_ffn, g_post_ffn, wo, win, wout, last):
    vec = _const_spec((1, D_MODEL))
    n_steps = SEQ // STEP_ROWS if last else N_STEPS
    stream_arrays, stream_specs = _stream_operands(stream)

    def o_spec(i):
        return pl.BlockSpec(
            (None, TILE, D_MODEL),
            lambda b, j: (b, jnp.minimum(POST_TILES_PER_STEP * j + i, N_LAT_TILES), 0))

    return pl.pallas_call(
        functools.partial(_post_ffn_kernel, n_stream=len(stream_arrays)),
        grid=(BATCH, n_steps),
        in_specs=stream_specs + [
            o_spec(0), o_spec(1), _mod_spec(), vec, vec, vec,
            _const_spec(wo.shape), _const_spec(win.shape), _const_spec(wout.shape)],
        out_specs=_x_spec(),
        out_shape=jax.ShapeDtypeStruct((BATCH, n_steps * STEP_ROWS, D_MODEL), F32),
        scratch_shapes=[pltpu.VMEM((POST_TILES_PER_STEP, TILE, D_FF), BF16)],
        compiler_params=_cparams(2),
        name="post_ffn",
    )(*stream_arrays, o, o, mod, g_post_mix, g_pre_ffn, g_post_ffn, wo, win, wout)


def _rope_tables():
    t = np.arange(SEQ)
    row = (t // GRID_W).astype(np.float32)
    col = (t % GRID_W).astype(np.float32)
    n_freq = MLA_ROPE // 4
    inv = np.float32(ROPE_THETA) ** (-(np.arange(n_freq, dtype=np.float32) / np.float32(n_freq)))
    ang = np.concatenate([row[:, None] * inv, col[:, None] * inv], axis=-1)
    cos = np.cos(ang.astype(np.float64)).astype(np.float32)
    sin = np.sin(ang.astype(np.float64)).astype(np.float32)
    cos_t = np.concatenate([cos, cos, cos, cos], axis=-1)
    sin_t = np.concatenate([-sin, sin, -sin, sin], axis=-1)
    no_pos = TOKENS_PAD - SEQ
    cos_t = np.concatenate([cos_t, np.ones((no_pos, LANES), np.float32)], axis=0)
    sin_t = np.concatenate([sin_t, np.zeros((no_pos, LANES), np.float32)], axis=0)
    return jnp.asarray(cos_t), jnp.asarray(sin_t)


NA_ROWS_PER_TILE = TILE // GRID_W
NA_BIAS_ROWS = 2 * NA_WIN_ROWS - 1
NA_BIAS_COLS = 2 * NA_WIN_COLS - 1
NA_NB_KEYS = NA_NB_BLOCKS * NA_NB_TOKENS


def _na_table_kernel(u_ref, o_ref):
    rows = SEQ // GRID_W
    kc = lax.broadcasted_iota(jnp.int32, (GRID_W, LANES), 0)
    lane = lax.broadcasted_iota(jnp.int32, (GRID_W, LANES), 1)
    qc = lane % GRID_W
    col_start = jnp.clip(qc - NA_WIN_COLS // 2, 0, GRID_W - NA_WIN_COLS)
    col_ok = (kc >= col_start) & (kc < col_start + NA_WIN_COLS)
    first_half = lane < GRID_W
    neg = jnp.full((GRID_W, LANES), NEG_INF, F32)
    toeplitz = {}
    for i in range(1, NA_BIAS_ROWS):
        row = jnp.broadcast_to(u_ref[i:i + 1, :], (GRID_W, LANES))
        toeplitz[i] = pltpu.roll(row, LANES - (NA_WIN_COLS - 1), 1, stride=1,
                                 stride_axis=0) * LOG2E
    for v, tile in enumerate((0, 1, N_LAT_TILES - 1)):
        for kl in range(NA_NB_BLOCKS * NA_ROWS_PER_TILE):
            kr = NA_ROWS_PER_TILE * (tile - 1) + kl
            for qp in range(NA_ROWS_PER_TILE // 2):
                oks = []
                for half in range(2):
                    qr = NA_ROWS_PER_TILE * tile + 2 * qp + half
                    row_start = min(max(qr - NA_WIN_ROWS // 2, 0), rows - NA_WIN_ROWS)
                    oks.append(row_start <= kr < row_start + NA_WIN_ROWS)
                if oks[0] and oks[1]:
                    ok = col_ok
                elif oks[0]:
                    ok = col_ok & first_half
                elif oks[1]:
                    ok = col_ok & jnp.logical_not(first_half)
                else:
                    ok = None
                i = kl - NA_ROWS_PER_TILE - 2 * qp + NA_WIN_ROWS - 1
                blk = neg if ok is None else jnp.where(ok, toeplitz[i], neg)
                o_ref[v, kl * GRID_W:(kl + 1) * GRID_W, qp * LANES:(qp + 1) * LANES] = blk
    o_ref[N_TABLE_VARIANTS - 1] = jnp.full((NA_NB_KEYS, TILE), NEG_INF, F32)


def _na_table(rpb):
    pad = jnp.full((NA_HEADS, NA_BIAS_ROWS, GRID_W - NA_BIAS_COLS), NEG_INF, F32)
    rev = jnp.concatenate([rpb.astype(F32)[:, :, ::-1], pad], axis=-1)
    off = jnp.full((NA_HEADS, 1, GRID_W), NEG_INF, F32)
    left = jnp.concatenate([rev, off], axis=1)
    right = jnp.concatenate([off, rev], axis=1)
    u = jnp.concatenate([left, right], axis=-1)
    return pl.pallas_call(
        _na_table_kernel,
        grid=(NA_HEADS,),
        in_specs=[pl.BlockSpec((None, NA_BIAS_ROWS + 1, LANES), lambda h: (h, 0, 0))],
        out_specs=pl.BlockSpec((N_TABLE_VARIANTS, None, NA_NB_KEYS, TILE),
                               lambda h: (0, h, 0, 0)),
        out_shape=jax.ShapeDtypeStruct((N_TABLE_VARIANTS, NA_HEADS, NA_NB_KEYS, TILE), F32),
        compiler_params=_cparams(1),
        name="na_table",
    )(u)


def _swa_table():
    j = np.array([0, 1, SEQ // SWA_Q_ROWS - 1])[:, None, None]
    qpos = SWA_Q_ROWS * j + np.arange(SWA_Q_ROWS)[None, None, :]
    kpos = (SWA_Q_ROWS * j - SWA_NB_TOKENS
            + np.arange(SWA_NB_BLOCKS * SWA_NB_TOKENS)[None, :, None])
    ok = (np.abs(kpos - qpos) <= SWA_WINDOW) & (kpos >= 0) & (kpos < SEQ)
    tbl = np.where(ok, 0.0, NEG_INF).astype(np.float32)
    tbl = np.concatenate([tbl, np.full((1,) + tbl.shape[1:], NEG_INF, np.float32)], axis=0)
    return jnp.asarray(tbl[:, None])


def _swap_halves(w):
    half = w.shape[-1] // 2
    return jnp.concatenate([w[..., half:], w[..., :half]], axis=-1)


def _mla_weights(w_in, w_uq, w_ukv):
    lat = MLA_Q_LORA + MLA_KV_LORA
    kpe = w_in[:, lat:]
    z = jnp.zeros((D_MODEL, LANES - MLA_ROPE), w_in.dtype)
    w_in_x = jnp.concatenate([w_in[:, :lat], kpe, z, _swap_halves(kpe), z], axis=1)
    uq = w_uq.reshape(MLA_Q_LORA, MLA_HEADS, MLA_NOPE + MLA_ROPE)
    rope = uq[:, :, MLA_NOPE:]
    zq = jnp.zeros((MLA_Q_LORA, MLA_HEADS, LANES - MLA_ROPE), w_uq.dtype)
    flat = lambda a: a.reshape(a.shape[0], -1)
    wq_x = jnp.concatenate([flat(uq[:, :, :MLA_NOPE]),
                            flat(jnp.concatenate([rope, zq], axis=-1)),
                            flat(jnp.concatenate([_swap_halves(rope), zq], axis=-1))], axis=1)
    ukv = w_ukv.reshape(MLA_KV_LORA, MLA_HEADS, MLA_NOPE + MLA_V)
    wkv_x = jnp.concatenate([flat(ukv[:, :, :MLA_NOPE]), flat(ukv[:, :, MLA_NOPE:])], axis=1)
    return w_in_x.astype(BF16), wq_x.astype(BF16), wkv_x.astype(BF16)


def _swa_weights(w_qkv):
    hd = SWA_HEADS * SWA_HEAD_DIM
    kd = SWA_KV_HEADS * SWA_HEAD_DIM
    wq = w_qkv[:, :hd].reshape(D_MODEL, SWA_HEADS, SWA_HEAD_DIM)
    wk = w_qkv[:, hd:hd + kd].reshape(D_MODEL, SWA_KV_HEADS, 1, SWA_HEAD_DIM)
    wv = w_qkv[:, hd + kd:].reshape(D_MODEL, SWA_KV_HEADS, 1, SWA_HEAD_DIM)
    dup = lambda a: jnp.broadcast_to(a, (D_MODEL, SWA_KV_HEADS, 2, SWA_HEAD_DIM)).reshape(
        D_MODEL, -1)
    flat = lambda a: a.reshape(D_MODEL, -1)
    w = jnp.concatenate([flat(wq), flat(_swap_halves(wq)), dup(wk), dup(_swap_halves(wk)),
                         dup(wv)], axis=1)
    return w.astype(BF16)


def kernel(x, c, ctx, c_ctx, w_mod, b_mod, g_pre_mix, g_post_mix, g_pre_ffn, g_post_ffn,
           w_ffn_in, w_ffn_out, mla_w_in, mla_g_q, mla_w_uq, mla_g_kv, mla_w_ukv, mla_w_o,
           na_w_qkv, na_rpb, na_w_o, swa_w_qkv, swa_sink, swa_w_o):
    assert x.shape == (BATCH, SEQ, D_MODEL) and ctx.shape == (BATCH, CTX_LEN, D_MODEL)
    stream = (x, ctx)
    cc = jnp.concatenate([c, c_ctx[None], jnp.zeros((8 - BATCH - 1, D_MODEL), F32)], axis=0)
    mods = _modulation(cc, w_mod, b_mod)
    mods = mods.reshape(DEPTH, 8, 6, D_MODEL)
    mods = jnp.stack([mods[:, :BATCH],
                      jnp.broadcast_to(mods[:, BATCH:BATCH + 1], (DEPTH, BATCH, 6, D_MODEL))],
                     axis=2)
    cos_t, sin_t = _rope_tables()
    row = lambda g: g.reshape(1, -1)

    for i in range(DEPTH):
        last = i == DEPTH - 1
        kind, j = i % 3, i // 3
        mod = mods[i]
        g_pre = row(g_pre_mix[i])
        if kind == 0:
            w_in_x, wq_x, wkv_x = _mla_weights(mla_w_in[j], mla_w_uq[j], mla_w_ukv[j])
            q, k, vt, ksq = _mla_proj(stream, mod, g_pre, cos_t, sin_t, w_in_x,
                                      row(mla_g_q[j]), row(mla_g_kv[j]), wq_x, wkv_x)
            o = _mla_attention(q, k, vt, jnp.swapaxes(ksq, 1, 2))
            w_o = mla_w_o[j]
        elif kind == 1:
            qk, vt, qsq, ksq = _na_proj(stream, mod, g_pre, na_w_qkv[j].astype(BF16))
            bias_top = jnp.maximum(jnp.max(na_rpb[j].astype(F32), axis=(1, 2)), 0.0) * LOG2E
            o = _local_attention(qk, 0, qk, 1, NA_HEADS * NA_HEAD_DIM, vt, qsq, ksq, NA_Q_ROWS,
                                 NA_NB_TOKENS, NA_NB_BLOCKS, _na_table(na_rpb[j]),
                                 jnp.broadcast_to(bias_top[:, None], (NA_HEADS, NA_Q_ROWS)),
                                 False, lambda p: p, lambda hd: hd, "na_attn")
            w_o = na_w_o[j]
        else:
            q, k, vt, qsq, ksq = _swa_proj(stream, mod, g_pre, cos_t, sin_t,
                                           _swa_weights(swa_w_qkv[j]))
            sink = jnp.broadcast_to((swa_sink[j].astype(F32) * LOG2E)[:, None],
                                    (SWA_HEADS, SWA_Q_ROWS))
            heads_per_kv = SWA_HEADS // SWA_KV_HEADS
            o = _local_attention(q, 0, k, 0, 2 * LANES, vt, qsq, ksq, SWA_Q_ROWS,
                                 SWA_NB_TOKENS, SWA_NB_BLOCKS, _swa_table(), sink, True,
                                 lambda p: 2 * p // heads_per_kv,
                                 lambda hd: 2 * (hd // heads_per_kv), "swa_attn")
            w_o = swa_w_o[j]
        stream = _post_ffn(stream, o, mod, row(g_post_mix[i]), row(g_pre_ffn[i]),
                           row(g_post_ffn[i]), w_o.astype(BF16), w_ffn_in[i].astype(BF16),
                           w_ffn_out[i].astype(BF16), last)
    return stream
```

```python
import functools

import jax
import jax.numpy as jnp
import numpy as np
from jax import lax
from jax.experimental import pallas as pl
from jax.experimental.pallas import tpu as pltpu

D_MODEL = 1024
BATCH = 4
SEQ = 8192
DEPTH = 4
GRID_W = 64
CTX_LEN = 256
D_FF = 2816
NORM_EPS = 1e-6
ROPE_THETA = 10000.0
NEG_INF = -1e30

MLA_HEADS = 8
MLA_Q_LORA = 256
MLA_KV_LORA = 256
MLA_NOPE = 128
MLA_ROPE = 64
MLA_V = 128
MLA_QK_PAD = 256

NA_HEADS = 16
NA_HEAD_DIM = 64
NA_WIN_ROWS = 8
NA_WIN_COLS = 16

SWA_HEADS = 16
SWA_KV_HEADS = 2
SWA_HEAD_DIM = 64
SWA_WINDOW = 128

TOKENS = SEQ + CTX_LEN
TILE = 256
N_TILES = TOKENS // TILE
N_LAT_TILES = SEQ // TILE
POST_TILES_PER_STEP = 2
STEP_ROWS = POST_TILES_PER_STEP * TILE
N_STEPS = -(-N_TILES // POST_TILES_PER_STEP)
CTX_STEP = N_LAT_TILES // POST_TILES_PER_STEP
TOKENS_PAD = N_STEPS * STEP_ROWS
LANES = 128
BF16_ROWS = 16
HEAD_PAIRS = 8
N_TABLE_VARIANTS = 4

MLA_TQ = 1024
MLA_TK = 2048
SOFTMAX_HEADROOM_LOG2 = 64.0
SOFTMAX_MIN_ROW_SUM = 2.0 ** -40
SOFTMAX_BOUND_SLACK = 1.02

NA_Q_ROWS = TILE
NA_NB_TOKENS = TILE
NA_NB_BLOCKS = NA_Q_ROWS // NA_NB_TOKENS + 2
SWA_Q_ROWS = TILE
SWA_NB_TOKENS = SWA_WINDOW
SWA_NB_BLOCKS = SWA_Q_ROWS // SWA_NB_TOKENS + 2

VMEM_LIMIT = 56 * 1024 * 1024

LOG2E = 1.4426950408889634

BF16 = jnp.bfloat16
F32 = jnp.float32


def _cparams(n_axes):
    return pltpu.CompilerParams(
        dimension_semantics=("parallel",) * n_axes, vmem_limit_bytes=VMEM_LIMIT)


def _const_spec(shape):
    nd = len(shape)
    return pl.BlockSpec(shape, lambda *_: (0,) * nd, pipeline_mode=pl.Buffered(1))


def _rms(x, g):
    return x * lax.rsqrt(jnp.mean(x * x, axis=-1, keepdims=True) + NORM_EPS) * g


def _dot(a, b):
    return jnp.dot(a, b, preferred_element_type=F32)


def _dot_nt(a, b):
    return lax.dot_general(a, b, (((1,), (1,)), ((), ())), preferred_element_type=F32)


def _ones_rows(n):
    row = lax.broadcasted_iota(jnp.int32, (BF16_ROWS, n), 0)
    return jnp.where(row == 0, 1.0, 0.0).astype(BF16)


def _mod_kernel(c_ref, w_ref, b_ref, o_ref):
    c = c_ref[...]
    a = c * jax.nn.sigmoid(c)
    o_ref[...] = _dot(a, w_ref[...]) + b_ref[...]


def _modulation(cc, w_mod, b_mod):
    tn = 1536
    return pl.pallas_call(
        _mod_kernel,
        grid=(DEPTH, 6 * D_MODEL // tn),
        in_specs=[
            pl.BlockSpec((8, D_MODEL), lambda l, n: (0, 0)),
            pl.BlockSpec((None, D_MODEL, tn), lambda l, n: (l, 0, n)),
            pl.BlockSpec((None, 1, tn), lambda l, n: (l, 0, n)),
        ],
        out_specs=pl.BlockSpec((None, 8, tn), lambda l, n: (l, 0, n)),
        out_shape=jax.ShapeDtypeStruct((DEPTH, 8, 6 * D_MODEL), F32),
        compiler_params=_cparams(2),
        name="modulation",
    )(cc, w_mod, b_mod.reshape(DEPTH, 1, 6 * D_MODEL))


def _mod_spec():
    return pl.BlockSpec((None, None, 6, D_MODEL),
                        lambda b, j: (b, (j == CTX_STEP).astype(jnp.int32), 0, 0))


def _x_spec():
    return pl.BlockSpec((None, STEP_ROWS, D_MODEL), lambda b, j: (b, j, 0))


def _rope_spec():
    return pl.BlockSpec((STEP_ROWS, LANES), lambda b, j: (j, 0))


def _stream_operands(stream):
    if not isinstance(stream, tuple):
        return [stream], [_x_spec()]
    return list(stream), [
        pl.BlockSpec((None, STEP_ROWS, D_MODEL),
                     lambda b, j: (b, jnp.minimum(j, CTX_STEP - 1), 0)),
        pl.BlockSpec((None, CTX_LEN, D_MODEL), lambda b, j: (b, 0, 0))]


def _stream_rows(refs):
    if len(refs) == 1:
        return refs[0][...]
    x_ref, ctx_ref = refs
    ctx_rows = jnp.concatenate([ctx_ref[...]] * (STEP_ROWS // CTX_LEN), axis=0)
    return jnp.where(pl.program_id(1) == CTX_STEP, ctx_rows, x_ref[...])


def _pre_mix(x, mod_ref, g_ref):
    shift = mod_ref[0:1, :]
    scale = mod_ref[1:2, :]
    h = _rms(x, g_ref[...]) * (1.0 + scale) + shift
    return h.astype(BF16)


def _mla_proj_kernel(*refs, n_stream, scale):
    stream_refs = refs[:n_stream]
    (mod_ref, g_ref, cos_ref, sin_ref, w_in_ref, gq_ref, gkv_ref, wq_ref, wkv_ref,
     q_ref, k_ref, vt_ref, ksq_ref) = refs[n_stream:]
    h = _pre_mix(_stream_rows(stream_refs), mod_ref, g_ref)
    a = _dot(h, w_in_ref[...])
    qn = _rms(a[:, :MLA_Q_LORA], gq_ref[...]).astype(BF16)
    kvn = _rms(a[:, MLA_Q_LORA:MLA_Q_LORA + MLA_KV_LORA], gkv_ref[...]).astype(BF16)
    cos = cos_ref[...]
    sin = sin_ref[...]
    k_rope = (a[:, 512:640] * cos + a[:, 640:768] * sin).astype(BF16)
    qq = _dot(qn, wq_ref[...])
    kv = _dot(kvn, wkv_ref[...])
    hn = MLA_HEADS * LANES
    k_rope_f = k_rope.astype(F32)
    rope_sq = jnp.sum(k_rope_f * k_rope_f, axis=-1, keepdims=True)
    for hd in range(MLA_HEADS):
        lo, hi = hd * LANES, (hd + 1) * LANES
        base = hd * MLA_QK_PAD
        q_ref[:, base:base + LANES] = (qq[:, lo:hi] * scale).astype(BF16)
        q_rope = qq[:, hn + lo:hn + hi] * cos + qq[:, 2 * hn + lo:2 * hn + hi] * sin
        q_ref[:, base + LANES:base + 2 * LANES] = (q_rope * scale).astype(BF16)
        k_nope = kv[:, lo:hi].astype(BF16)
        k_ref[:, base:base + LANES] = k_nope
        k_ref[:, base + LANES:base + 2 * LANES] = k_rope
        k_nope_f = k_nope.astype(F32)
        k_sq = jnp.sum(k_nope_f * k_nope_f, axis=-1, keepdims=True) + rope_sq
        ksq_ref[hd:hd + 1, :] = jnp.broadcast_to(jnp.max(k_sq, axis=0, keepdims=True),
                                                 (1, LANES))
    vt_ref[...] = kv[:, hn:].T.astype(BF16)


def _mla_proj(stream, mod, g_pre, cos, sin, w_in, g_q, g_kv, wq, wkv):
    hq = MLA_HEADS * MLA_QK_PAD
    hv = MLA_HEADS * MLA_V
    out_spec_q = pl.BlockSpec((None, STEP_ROWS, hq), lambda b, j: (b, j, 0))
    out_spec_v = pl.BlockSpec((None, hv, STEP_ROWS), lambda b, j: (b, 0, j))
    scale = float((MLA_NOPE + MLA_ROPE) ** -0.5) * LOG2E
    stream_arrays, stream_specs = _stream_operands(stream)
    return pl.pallas_call(
        functools.partial(_mla_proj_kernel, n_stream=len(stream_arrays), scale=scale),
        grid=(BATCH, N_STEPS),
        in_specs=stream_specs + [
            _mod_spec(), _const_spec((1, D_MODEL)), _rope_spec(), _rope_spec(),
            _const_spec(w_in.shape), _const_spec((1, MLA_Q_LORA)),
            _const_spec((1, MLA_KV_LORA)), _const_spec(wq.shape), _const_spec(wkv.shape)],
        out_specs=[out_spec_q, out_spec_q, out_spec_v,
                   pl.BlockSpec((None, None, MLA_HEADS, LANES), lambda b, j: (b, j, 0, 0))],
        out_shape=[jax.ShapeDtypeStruct((BATCH, TOKENS_PAD, hq), BF16),
                   jax.ShapeDtypeStruct((BATCH, TOKENS_PAD, hq), BF16),
                   jax.ShapeDtypeStruct((BATCH, hv, TOKENS_PAD), BF16),
                   jax.ShapeDtypeStruct((BATCH, N_STEPS, MLA_HEADS, LANES), F32)],
        compiler_params=_cparams(2),
        name="mla_proj",
    )(*stream_arrays, mod, g_pre, cos, sin, w_in, g_q, g_kv, wq, wkv)


def _softmax_update(s, vt, carry):
    m, l, acc = carry
    m_new = jnp.maximum(m, jnp.max(s, axis=0, keepdims=True))
    alpha = jnp.exp2(m - m_new)
    p = jnp.exp2(s - m_new)
    l = alpha * l + jnp.sum(p, axis=0, keepdims=True)
    acc = alpha * acc + _dot(vt, p.astype(BF16))
    return m_new, l, acc


def _mla_attn_kernel(q_ref, k_ref, vt_ref, ksq_ref, o_ref, *, chunks):
    q = q_ref[...]
    tq = q.shape[0]

    def logits(chunk):
        start, size = chunk
        return _dot_nt(k_ref[start:start + size, :], q)

    def single_pass(r):
        l = jnp.zeros((1, tq), F32)
        acc = jnp.zeros((MLA_V, tq), F32)
        s = logits(chunks[0])
        for c, (start, size) in enumerate(chunks):
            s_next = logits(chunks[c + 1]) if c + 1 < len(chunks) else None
            p = jnp.exp2(s - r)
            l = l + jnp.sum(p, axis=0, keepdims=True)
            acc = acc + _dot(vt_ref[:, start:start + size], p.astype(BF16))
            s = s_next
        return l, acc

    def online():
        carry = (jnp.full((1, tq), NEG_INF, F32), jnp.zeros((1, tq), F32),
                 jnp.zeros((MLA_V, tq), F32))
        s = logits(chunks[0])
        for c, (start, size) in enumerate(chunks):
            s_next = logits(chunks[c + 1]) if c + 1 < len(chunks) else None
            carry = _softmax_update(s, vt_ref[:, start:start + size], carry)
            s = s_next
        return carry[1], carry[2]

    if ksq_ref is None:
        l, acc = online()
    else:
        qf = q.astype(F32)
        ones = jnp.ones((8, q.shape[1]), BF16)
        q_sq = _dot_nt(ones, (qf * qf).astype(BF16))[0:1]
        k_sq = jnp.max(ksq_ref[...], axis=0, keepdims=True)[:, 0:1]
        bound = jnp.sqrt(q_sq * k_sq) * SOFTMAX_BOUND_SLACK
        l, acc = single_pass(bound - SOFTMAX_HEADROOM_LOG2)
        l, acc = lax.cond(jnp.min(l) >= SOFTMAX_MIN_ROW_SUM, lambda: (l, acc), online)
    o_ref[...] = (acc / l).T.astype(BF16)


def _mla_attn_ctx_kernel(q_ref, k_ref, vt_ref, latent_out_ref, o_ref):
    del latent_out_ref
    for h in range(MLA_HEADS):
        qk = slice(h * MLA_QK_PAD, (h + 1) * MLA_QK_PAD)
        hv = slice(h * MLA_V, (h + 1) * MLA_V)
        _mla_attn_kernel(q_ref.at[:, qk], k_ref.at[:, qk], vt_ref.at[hv, :], None,
                         o_ref.at[:, hv], chunks=((0, CTX_LEN),))


def _mla_attention(q, k, vt, ksq):
    hv = MLA_HEADS * MLA_V
    chunks = ((SEQ, CTX_LEN),) + tuple((c * MLA_TK, MLA_TK) for c in range(SEQ // MLA_TK))
    o = pl.pallas_call(
        functools.partial(_mla_attn_kernel, chunks=chunks),
        grid=(BATCH, MLA_HEADS, SEQ // MLA_TQ),
        in_specs=[
            pl.BlockSpec((None, MLA_TQ, MLA_QK_PAD), lambda b, h, i: (b, i, h)),
            pl.BlockSpec((None, TOKENS, MLA_QK_PAD), lambda b, h, i: (b, 0, h)),
            pl.BlockSpec((None, MLA_V, TOKENS), lambda b, h, i: (b, h, 0)),
            pl.BlockSpec((None, None, N_STEPS, LANES), lambda b, h, i: (b, h, 0, 0)),
        ],
        out_specs=pl.BlockSpec((None, MLA_TQ, MLA_V), lambda b, h, i: (b, i, h)),
        out_shape=jax.ShapeDtypeStruct((BATCH, TOKENS, hv), BF16),
        compiler_params=_cparams(3),
        name="mla_attn",
    )(q, k, vt, ksq)
    ctx_blk = SEQ // CTX_LEN
    hq = MLA_HEADS * MLA_QK_PAD
    return pl.pallas_call(
        _mla_attn_ctx_kernel,
        grid=(BATCH,),
        in_specs=[
            pl.BlockSpec((None, CTX_LEN, hq), lambda b: (b, ctx_blk, 0)),
            pl.BlockSpec((None, CTX_LEN, hq), lambda b: (b, ctx_blk, 0)),
            pl.BlockSpec((None, hv, CTX_LEN), lambda b: (b, 0, ctx_blk)),
            pl.BlockSpec(memory_space=pl.ANY),
        ],
        out_specs=pl.BlockSpec((None, CTX_LEN, hv), lambda b: (b, ctx_blk, 0)),
        out_shape=jax.ShapeDtypeStruct((BATCH, TOKENS, hv), BF16),
        input_output_aliases={3: 0},
        compiler_params=_cparams(1),
        name="mla_attn_ctx",
    )(q, k, vt, o)


def _group_sq_norms(x):
    w = x.shape[1]
    xf = x.astype(F32)
    group = lax.broadcasted_iota(jnp.int32, (w, LANES), 0) // NA_HEAD_DIM
    col = lax.broadcasted_iota(jnp.int32, (w, LANES), 1)
    indicator = jnp.where(group == col, 1.0, 0.0).astype(BF16)
    return _dot((xf * xf).astype(BF16), indicator)


def _store_norms(qb, kb, qsq_ref, ksq_ref):
    qsq_ref[...] = _group_sq_norms(qb).T[:NA_HEADS, :]
    k_max = jnp.max(_group_sq_norms(kb), axis=0, keepdims=True)
    ksq_ref[...] = jnp.broadcast_to(k_max, (8, LANES))


def _norm_specs():
    return ([pl.BlockSpec((None, NA_HEADS, STEP_ROWS), lambda b, j: (b, 0, j)),
             pl.BlockSpec((None, None, 8, LANES), lambda b, j: (b, j, 0, 0))],
            [jax.ShapeDtypeStruct((BATCH, NA_HEADS, TOKENS_PAD), F32),
             jax.ShapeDtypeStruct((BATCH, N_STEPS, 8, LANES), F32)])


def _na_proj_kernel(x_ref, mod_ref, g_ref, w_ref, qk_ref, vt_ref, qsq_ref, ksq_ref, *, scale):
    h = _pre_mix(x_ref[...], mod_ref, g_ref)
    a = _dot(h, w_ref[...])
    hd = NA_HEADS * NA_HEAD_DIM
    q = a[:, :hd] * scale
    k = a[:, hd:2 * hd]
    qk_ref[:, :hd] = q.astype(BF16)
    qk_ref[:, hd:] = k.astype(BF16)
    vt_ref[...] = a[:, 2 * hd:].T.astype(BF16)
    _store_norms(q, k, qsq_ref, ksq_ref)


def _na_proj(x, mod, g_pre, w):
    hd = NA_HEADS * NA_HEAD_DIM
    norm_specs, norm_shapes = _norm_specs()
    return pl.pallas_call(
        functools.partial(_na_proj_kernel, scale=float(NA_HEAD_DIM ** -0.5) * LOG2E),
        grid=(BATCH, N_STEPS),
        in_specs=[_x_spec(), _mod_spec(), _const_spec((1, D_MODEL)), _const_spec(w.shape)],
        out_specs=[pl.BlockSpec((None, STEP_ROWS, 2 * hd), lambda b, j: (b, j, 0)),
                   pl.BlockSpec((None, hd, STEP_ROWS), lambda b, j: (b, 0, j))] + norm_specs,
        out_shape=[jax.ShapeDtypeStruct((BATCH, TOKENS_PAD, 2 * hd), BF16),
                   jax.ShapeDtypeStruct((BATCH, hd, TOKENS_PAD), BF16)] + norm_shapes,
        compiler_params=_cparams(2),
        name="na_proj",
    )(x, mod, g_pre, w)


def _swa_proj_kernel(x_ref, mod_ref, g_ref, cos_ref, sin_ref, w_ref, q_ref, k_ref, vt_ref,
                     qsq_ref, ksq_ref, *, scale):
    h = _pre_mix(x_ref[...], mod_ref, g_ref)
    a = _dot(h, w_ref[...])
    cos = cos_ref[...]
    sin = sin_ref[...]
    hd = SWA_HEADS * SWA_HEAD_DIM
    for p in range(HEAD_PAIRS):
        lo, hi = p * LANES, (p + 1) * LANES
        q = a[:, lo:hi] * cos + a[:, hd + lo:hd + hi] * sin
        q_ref[:, lo:hi] = (q * scale).astype(BF16)
    kw = 2 * LANES
    for p in range(2):
        lo, hi = p * LANES, (p + 1) * LANES
        k = a[:, 2 * hd + lo:2 * hd + hi] * cos + a[:, 2 * hd + kw + lo:2 * hd + kw + hi] * sin
        k_ref[:, lo:hi] = k.astype(BF16)
    vt_ref[...] = a[:, 2 * hd + 2 * kw:].T.astype(BF16)
    _store_norms(q_ref[...], k_ref[...], qsq_ref, ksq_ref)


def _swa_proj(x, mod, g_pre, cos, sin, w):
    hd = SWA_HEADS * SWA_HEAD_DIM
    kw = 2 * LANES
    norm_specs, norm_shapes = _norm_specs()
    return pl.pallas_call(
        functools.partial(_swa_proj_kernel, scale=float(SWA_HEAD_DIM ** -0.5) * LOG2E),
        grid=(BATCH, N_STEPS),
        in_specs=[_x_spec(), _mod_spec(), _const_spec((1, D_MODEL)), _rope_spec(), _rope_spec(),
                  _const_spec(w.shape)],
        out_specs=[pl.BlockSpec((None, STEP_ROWS, hd), lambda b, j: (b, j, 0)),
                   pl.BlockSpec((None, STEP_ROWS, kw), lambda b, j: (b, j, 0)),
                   pl.BlockSpec((None, kw, STEP_ROWS), lambda b, j: (b, 0, j))] + norm_specs,
        out_shape=[jax.ShapeDtypeStruct((BATCH, TOKENS_PAD, hd), BF16),
                   jax.ShapeDtypeStruct((BATCH, TOKENS_PAD, kw), BF16),
                   jax.ShapeDtypeStruct((BATCH, kw, TOKENS_PAD), BF16)] + norm_shapes,
        compiler_params=_cparams(2),
        name="swa_proj",
    )(x, mod, g_pre, cos, sin, w)


def _local_attn_kernel(*refs, n_nb, kv_slot, k_norm_lane, per_head_table, has_sink):
    q_ref = refs[0]
    k_refs = refs[1:1 + n_nb]
    kc_ref = refs[1 + n_nb]
    vt_refs = refs[2 + n_nb:2 + 2 * n_nb]
    vtc_ref = refs[2 + 2 * n_nb]
    tbl_ref, head_ref, qsq_ref, ksq_ref, o_ref = refs[3 + 2 * n_nb:]
    hdim = NA_HEAD_DIM
    tq = q_ref.shape[0]
    low = lax.broadcasted_iota(jnp.int32, (tq, LANES), 1) < hdim

    def pair_logits(p):
        qp = q_ref[:, p * LANES:(p + 1) * LANES]
        zero = jnp.zeros_like(qp)
        q2 = jnp.concatenate([jnp.where(low, qp, zero), jnp.where(low, zero, qp)], axis=0)
        ks = slice(kv_slot(p) * LANES, (kv_slot(p) + 1) * LANES)
        keys = jnp.concatenate([r[:, ks] for r in k_refs] + [kc_ref[:, ks]], axis=0)
        return _dot_nt(keys, q2)

    def attend(reference):
        s = pair_logits(0)
        n_nb_keys = n_nb * k_refs[0].shape[0]
        outs, l_min = [], None
        for p in range(HEAD_PAIRS):
            s_next = pair_logits(p + 1) if p + 1 < HEAD_PAIRS else None
            pair_out = []
            for half in range(2):
                hd = 2 * p + half
                cols = slice(half * tq, (half + 1) * tq)
                s_n = s[:n_nb_keys, cols] + tbl_ref[hd if per_head_table else 0]
                s_c = s[n_nb_keys:, cols]
                r = reference(hd, s_n, s_c)
                probs = jnp.concatenate([jnp.exp2(s_n - r).astype(BF16),
                                         jnp.exp2(s_c - r).astype(BF16)], axis=0)
                lo = kv_slot(p) * LANES + half * hdim
                vt = jnp.concatenate([v[lo:lo + hdim, :] for v in vt_refs]
                                     + [vtc_ref[lo:lo + hdim, :]], axis=1)
                vt = jnp.concatenate([vt, _ones_rows(vt.shape[1])], axis=0)
                acc = _dot(vt, probs)
                l = acc[hdim:hdim + 1]
                if has_sink:
                    l = l + jnp.exp2(head_ref[hd:hd + 1, :] - r)
                l_min = l if l_min is None else jnp.minimum(l_min, l)
                pair_out.append(acc[:hdim] / l)
            outs.append(jnp.concatenate(pair_out, axis=0).T.astype(BF16))
            s = s_next
        return tuple(outs), l_min

    def row_max(hd, s_n, s_c):
        m = jnp.maximum(jnp.max(s_n, axis=0, keepdims=True), jnp.max(s_c, axis=0, keepdims=True))
        return jnp.maximum(m, head_ref[hd:hd + 1, :]) if has_sink else m

    k_sq = jnp.max(ksq_ref[...], axis=0)[0:1, :]

    def bound(hd, s_n, s_c):
        lane = k_norm_lane(hd)
        qk = jnp.sqrt(qsq_ref[hd:hd + 1, :] * k_sq[:, lane:lane + 1]) * SOFTMAX_BOUND_SLACK
        extra = head_ref[hd:hd + 1, :]
        top = jnp.maximum(qk, extra) if has_sink else qk + extra
        return top - SOFTMAX_HEADROOM_LOG2

    outs, l_min = attend(bound)
    outs = lax.cond(jnp.min(l_min) >= SOFTMAX_MIN_ROW_SUM, lambda: outs,
                    lambda: attend(row_max)[0])
    for p, pair in enumerate(outs):
        o_ref[:, p * LANES:(p + 1) * LANES] = pair


def _local_attention(q, q_col, k, k_col, k_width, vt, qsq, ksq, q_rows, nb_tokens, n_nb, table,
                     head_rows, has_sink, kv_slot, k_norm_lane, name):
    hd = NA_HEADS * NA_HEAD_DIM
    per_head_table = table.shape[1] > 1
    n_steps = -(-TOKENS // q_rows)
    lat_hi = SEQ // q_rows - 1
    per_tile = q_rows // nb_tokens
    nb_hi = SEQ // nb_tokens - 1

    def nb_index(j, n):
        return jnp.clip(j * per_tile - 1 + n, 0, nb_hi)

    def k_spec(n):
        return pl.BlockSpec((None, nb_tokens, k_width), lambda b, j: (b, nb_index(j, n), k_col))

    def vt_spec(n):
        return pl.BlockSpec((None, k_width, nb_tokens), lambda b, j: (b, 0, nb_index(j, n)))

    def variant(j):
        return jnp.where(j == 0, 0, jnp.where(j == lat_hi, 2, jnp.where(j > lat_hi, 3, 1)))

    tbl_spec = pl.BlockSpec((None,) + table.shape[1:], lambda b, j: (variant(j), 0, 0, 0))
    in_specs = ([pl.BlockSpec((None, q_rows, hd), lambda b, j: (b, j, q_col))]
                + [k_spec(n) for n in range(n_nb)]
                + [pl.BlockSpec((None, CTX_LEN, k_width), lambda b, j: (b, N_LAT_TILES, k_col))]
                + [vt_spec(n) for n in range(n_nb)]
                + [pl.BlockSpec((None, k_width, CTX_LEN), lambda b, j: (b, 0, N_LAT_TILES))]
                + [tbl_spec, _const_spec((NA_HEADS, q_rows)),
                   pl.BlockSpec((None, NA_HEADS, q_rows), lambda b, j: (b, 0, j)),
                   pl.BlockSpec((None, N_STEPS, 8, LANES), lambda b, j: (b, 0, 0, 0))])
    return pl.pallas_call(
        functools.partial(_local_attn_kernel, n_nb=n_nb, kv_slot=kv_slot,
                          k_norm_lane=k_norm_lane, per_head_table=per_head_table,
                          has_sink=has_sink),
        grid=(BATCH, n_steps),
        in_specs=in_specs,
        out_specs=pl.BlockSpec((None, q_rows, hd), lambda b, j: (b, j, 0)),
        out_shape=jax.ShapeDtypeStruct((BATCH, n_steps * q_rows, hd), BF16),
        compiler_params=_cparams(2),
        name=name,
    )(q, *([k] * (n_nb + 1)), *([vt] * (n_nb + 1)), table, head_rows, qsq, ksq)


def _post_ffn_kernel(*refs, n_stream):
    (oa_ref, ob_ref, mod_ref, gpm_ref, gpf_ref, gof_ref, wo_ref, win_ref, wout_ref,
     y_ref, act_ref) = refs[n_stream:]
    x = _stream_rows(refs[:n_stream])
    gate_m = mod_ref[2:3, :]
    shift_f = mod_ref[3:4, :]
    scale_f = mod_ref[4:5, :]
    gate_f = mod_ref[5:6, :]
    rows = [slice(i * TILE, (i + 1) * TILE) for i in range(POST_TILES_PER_STEP)]
    mixed = [_dot(o_ref[...], wo_ref[...]) for o_ref in (oa_ref, ob_ref)]
    x1 = [x[r, :] + gate_m * _rms(mx, gpm_ref[...]) for r, mx in zip(rows, mixed)]
    h = [(_rms(xi, gpf_ref[...]) * (1.0 + scale_f) + shift_f).astype(BF16) for xi in x1]
    chunk = 256
    for i, r in enumerate(rows):
        for f in range(D_FF // chunk):
            lo, hi = f * chunk, (f + 1) * chunk
            gate = _dot(h[i], win_ref[:, lo:hi])
            up = _dot(h[i], win_ref[:, D_FF + lo:D_FF + hi])
            act_ref[i, :, lo:hi] = (gate * jax.nn.sigmoid(gate) * up).astype(BF16)
        ffn = _dot(act_ref[i], wout_ref[...])
        y_ref[r, :] = x1[i] + gate_f * _rms(ffn, gof_ref[...])


def _post_ffn(stream, o, mod, g_post_mix, g_pre_ffn, g_post_ffn, wo, win, wout, last):
    vec = _const_spec((1, D_MODEL))
    n_steps = SEQ // STEP_ROWS if last else N_STEPS
    stream_arrays, stream_specs = _stream_operands(stream)

    def o_spec(i):
        return pl.BlockSpec(
            (None, TILE, D_MODEL),
            lambda b, j: (b, jnp.minimum(POST_TILES_PER_STEP * j + i, N_LAT_TILES), 0))

    return pl.pallas_call(
        functools.partial(_post_ffn_kernel, n_stream=len(stream_arrays)),
        grid=(BATCH, n_steps),
        in_specs=stream_specs + [
            o_spec(0), o_spec(1), _mod_spec(), vec, vec, vec,
            _const_spec(wo.shape), _const_spec(win.shape), _const_spec(wout.shape)],
        out_specs=_x_spec(),
        out_shape=jax.ShapeDtypeStruct((BATCH, n_steps * STEP_ROWS, D_MODEL), F32),
        scratch_shapes=[pltpu.VMEM((POST_TILES_PER_STEP, TILE, D_FF), BF16)],
        compiler_params=_cparams(2),
        name="post_ffn",
    )(*stream_arrays, o, o, mod, g_post_mix, g_pre_ffn, g_post_ffn, wo, win, wout)


def _rope_tables():
    t = np.arange(SEQ)
    row = (t // GRID_W).astype(np.float32)
    col = (t % GRID_W).astype(np.float32)
    n_freq = MLA_ROPE // 4
    inv = np.float32(ROPE_THETA) ** (-(np.arange(n_freq, dtype=np.float32) / np.float32(n_freq)))
    ang = np.concatenate([row[:, None] * inv, col[:, None] * inv], axis=-1)
    cos = np.cos(ang.astype(np.float64)).astype(np.float32)
    sin = np.sin(ang.astype(np.float64)).astype(np.float32)
    cos_t = np.concatenate([cos, cos, cos, cos], axis=-1)
    sin_t = np.concatenate([-sin, sin, -sin, sin], axis=-1)
    no_pos = TOKENS_PAD - SEQ
    cos_t = np.concatenate([cos_t, np.ones((no_pos, LANES), np.float32)], axis=0)
    sin_t = np.concatenate([sin_t, np.zeros((no_pos, LANES), np.float32)], axis=0)
    return jnp.asarray(cos_t), jnp.asarray(sin_t)


NA_ROWS_PER_TILE = TILE // GRID_W
NA_BIAS_ROWS = 2 * NA_WIN_ROWS - 1
NA_BIAS_COLS = 2 * NA_WIN_COLS - 1
NA_NB_KEYS = NA_NB_BLOCKS * NA_NB_TOKENS


def _na_table_kernel(u_ref, o_ref):
    rows = SEQ // GRID_W
    kc = lax.broadcasted_iota(jnp.int32, (GRID_W, LANES), 0)
    lane = lax.broadcasted_iota(jnp.int32, (GRID_W, LANES), 1)
    qc = lane % GRID_W
    col_start = jnp.clip(qc - NA_WIN_COLS // 2, 0, GRID_W - NA_WIN_COLS)
    col_ok = (kc >= col_start) & (kc < col_start + NA_WIN_COLS)
    first_half = lane < GRID_W
    neg = jnp.full((GRID_W, LANES), NEG_INF, F32)
    toeplitz = {}
    for i in range(1, NA_BIAS_ROWS):
        row = jnp.broadcast_to(u_ref[i:i + 1, :], (GRID_W, LANES))
        toeplitz[i] = pltpu.roll(row, LANES - (NA_WIN_COLS - 1), 1, stride=1,
                                 stride_axis=0) * LOG2E
    for v, tile in enumerate((0, 1, N_LAT_TILES - 1)):
        for kl in range(NA_NB_BLOCKS * NA_ROWS_PER_TILE):
            kr = NA_ROWS_PER_TILE * (tile - 1) + kl
            for qp in range(NA_ROWS_PER_TILE // 2):
                oks = []
                for half in range(2):
                    qr = NA_ROWS_PER_TILE * tile + 2 * qp + half
                    row_start = min(max(qr - NA_WIN_ROWS // 2, 0), rows - NA_WIN_ROWS)
                    oks.append(row_start <= kr < row_start + NA_WIN_ROWS)
                if oks[0] and oks[1]:
                    ok = col_ok
                elif oks[0]:
                    ok = col_ok & first_half
                elif oks[1]:
                    ok = col_ok & jnp.logical_not(first_half)
                else:
                    ok = None
                i = kl - NA_ROWS_PER_TILE - 2 * qp + NA_WIN_ROWS - 1
                blk = neg if ok is None else jnp.where(ok, toeplitz[i], neg)
                o_ref[v, kl * GRID_W:(kl + 1) * GRID_W, qp * LANES:(qp + 1) * LANES] = blk
    o_ref[N_TABLE_VARIANTS - 1] = jnp.full((NA_NB_KEYS, TILE), NEG_INF, F32)


def _na_table(rpb):
    pad = jnp.full((NA_HEADS, NA_BIAS_ROWS, GRID_W - NA_BIAS_COLS), NEG_INF, F32)
    rev = jnp.concatenate([rpb.astype(F32)[:, :, ::-1], pad], axis=-1)
    off = jnp.full((NA_HEADS, 1, GRID_W), NEG_INF, F32)
    left = jnp.concatenate([rev, off], axis=1)
    right = jnp.concatenate([off, rev], axis=1)
    u = jnp.concatenate([left, right], axis=-1)
    return pl.pallas_call(
        _na_table_kernel,
        grid=(NA_HEADS,),
        in_specs=[pl.BlockSpec((None, NA_BIAS_ROWS + 1, LANES), lambda h: (h, 0, 0))],
        out_specs=pl.BlockSpec((N_TABLE_VARIANTS, None, NA_NB_KEYS, TILE),
                               lambda h: (0, h, 0, 0)),
        out_shape=jax.ShapeDtypeStruct((N_TABLE_VARIANTS, NA_HEADS, NA_NB_KEYS, TILE), F32),
        compiler_params=_cparams(1),
        name="na_table",
    )(u)


def _swa_table():
    j = np.array([0, 1, SEQ // SWA_Q_ROWS - 1])[:, None, None]
    qpos = SWA_Q_ROWS * j + np.arange(SWA_Q_ROWS)[None, None, :]
    kpos = (SWA_Q_ROWS * j - SWA_NB_TOKENS
            + np.arange(SWA_NB_BLOCKS * SWA_NB_TOKENS)[None, :, None])
    ok = (np.abs(kpos - qpos) <= SWA_WINDOW) & (kpos >= 0) & (kpos < SEQ)
    tbl = np.where(ok, 0.0, NEG_INF).astype(np.float32)
    tbl = np.concatenate([tbl, np.full((1,) + tbl.shape[1:], NEG_INF, np.float32)], axis=0)
    return jnp.asarray(tbl[:, None])


def _swap_halves(w):
    half = w.shape[-1] // 2
    return jnp.concatenate([w[..., half:], w[..., :half]], axis=-1)


def _mla_weights(w_in, w_uq, w_ukv):
    lat = MLA_Q_LORA + MLA_KV_LORA
    kpe = w_in[:, lat:]
    z = jnp.zeros((D_MODEL, LANES - MLA_ROPE), w_in.dtype)
    w_in_x = jnp.concatenate([w_in[:, :lat], kpe, z, _swap_halves(kpe), z], axis=1)
    uq = w_uq.reshape(MLA_Q_LORA, MLA_HEADS, MLA_NOPE + MLA_ROPE)
    rope = uq[:, :, MLA_NOPE:]
    zq = jnp.zeros((MLA_Q_LORA, MLA_HEADS, LANES - MLA_ROPE), w_uq.dtype)
    flat = lambda a: a.reshape(a.shape[0], -1)
    wq_x = jnp.concatenate([flat(uq[:, :, :MLA_NOPE]),
                            flat(jnp.concatenate([rope, zq], axis=-1)),
                            flat(jnp.concatenate([_swap_halves(rope), zq], axis=-1))], axis=1)
    ukv = w_ukv.reshape(MLA_KV_LORA, MLA_HEADS, MLA_NOPE + MLA_V)
    wkv_x = jnp.concatenate([flat(ukv[:, :, :MLA_NOPE]), flat(ukv[:, :, MLA_NOPE:])], axis=1)
    return w_in_x.astype(BF16), wq_x.astype(BF16), wkv_x.astype(BF16)


def _swa_weights(w_qkv):
    hd = SWA_HEADS * SWA_HEAD_DIM
    kd = SWA_KV_HEADS * SWA_HEAD_DIM
    wq = w_qkv[:, :hd].reshape(D_MODEL, SWA_HEADS, SWA_HEAD_DIM)
    wk = w_qkv[:, hd:hd + kd].reshape(D_MODEL, SWA_KV_HEADS, 1, SWA_HEAD_DIM)
    wv = w_qkv[:, hd + kd:].reshape(D_MODEL, SWA_KV_HEADS, 1, SWA_HEAD_DIM)
    dup = lambda a: jnp.broadcast_to(a, (D_MODEL, SWA_KV_HEADS, 2, SWA_HEAD_DIM)).reshape(
        D_MODEL, -1)
    flat = lambda a: a.reshape(D_MODEL, -1)
    w = jnp.concatenate([flat(wq), flat(_swap_halves(wq)), dup(wk), dup(_swap_halves(wk)),
                         dup(wv)], axis=1)
    return w.astype(BF16)


def kernel(x, c, ctx, c_ctx, w_mod, b_mod, g_pre_mix, g_post_mix, g_pre_ffn, g_post_ffn,
           w_ffn_in, w_ffn_out, mla_w_in, mla_g_q, mla_w_uq, mla_g_kv, mla_w_ukv, mla_w_o,
           na_w_qkv, na_rpb, na_w_o, swa_w_qkv, swa_sink, swa_w_o):
    assert x.shape == (BATCH, SEQ, D_MODEL) and ctx.shape == (BATCH, CTX_LEN, D_MODEL)
    stream = (x, ctx)
    cc = jnp.concatenate([c, c_ctx[None], jnp.zeros((8 - BATCH - 1, D_MODEL), F32)], axis=0)
    mods = _modulation(cc, w_mod, b_mod)
    mods = mods.reshape(DEPTH, 8, 6, D_MODEL)
    mods = jnp.stack([mods[:, :BATCH],
                      jnp.broadcast_to(mods[:, BATCH:BATCH + 1], (DEPTH, BATCH, 6, D_MODEL))],
                     axis=2)
    cos_t, sin_t = _rope_tables()
    row = lambda g: g.reshape(1, -1)

    for i in range(DEPTH):
        last = i == DEPTH - 1
        kind, j = i % 3, i // 3
        mod = mods[i]
        g_pre = row(g_pre_mix[i])
        if kind == 0:
            w_in_x, wq_x, wkv_x = _mla_weights(mla_w_in[j], mla_w_uq[j], mla_w_ukv[j])
            q, k, vt, ksq = _mla_proj(stream, mod, g_pre, cos_t, sin_t, w_in_x,
                                      row(mla_g_q[j]), row(mla_g_kv[j]), wq_x, wkv_x)
            o = _mla_attention(q, k, vt, jnp.swapaxes(ksq, 1, 2))
            w_o = mla_w_o[j]
        elif kind == 1:
            qk, vt, qsq, ksq = _na_proj(stream, mod, g_pre, na_w_qkv[j].astype(BF16))
            bias_top = jnp.maximum(jnp.max(na_rpb[j].astype(F32), axis=(1, 2)), 0.0) * LOG2E
            o = _local_attention(qk, 0, qk, 1, NA_HEADS * NA_HEAD_DIM, vt, qsq, ksq, NA_Q_ROWS,
                                 NA_NB_TOKENS, NA_NB_BLOCKS, _na_table(na_rpb[j]),
                                 jnp.broadcast_to(bias_top[:, None], (NA_HEADS, NA_Q_ROWS)),
                                 False, lambda p: p, lambda hd: hd, "na_attn")
            w_o = na_w_o[j]
        else:
            q, k, vt, qsq, ksq = _swa_proj(stream, mod, g_pre, cos_t, sin_t,
                                           _swa_weights(swa_w_qkv[j]))
            sink = jnp.broadcast_to((swa_sink[j].astype(F32) * LOG2E)[:, None],
                                    (SWA_HEADS, SWA_Q_ROWS))
            heads_per_kv = SWA_HEADS // SWA_KV_HEADS
            o = _local_attention(q, 0, k, 0, 2 * LANES, vt, qsq, ksq, SWA_Q_ROWS,
                                 SWA_NB_TOKENS, SWA_NB_BLOCKS, _swa_table(), sink, True,
                                 lambda p: 2 * p // heads_per_kv,
                                 lambda hd: 2 * (hd // heads_per_kv), "swa_attn")
            w_o = swa_w_o[j]
        stream = _post_ffn(stream, o, mod, row(g_post_mix[i]), row(g_pre_ffn[i]),
                           row(g_post_ffn[i]), w_o.astype(BF16), w_ffn_in[i].astype(BF16),
                           w_ffn_out[i].astype(BF16), last)
    return stream
```

```python
import functools
from typing import Callable, NamedTuple

import jax
import jax.numpy as jnp
import numpy as np
from jax import lax
from jax.experimental import pallas as pl
from jax.experimental.pallas import tpu as pltpu

D_MODEL = 1024
BATCH = 4
SEQ = 8192
DEPTH = 4
GRID_W = 64
CTX_LEN = 256
D_FF = 2816
NORM_EPS = 1e-6
ROPE_THETA = 10000.0
NEG_INF = -1e30

MLA_HEADS = 8
MLA_Q_LORA = 256
MLA_KV_LORA = 256
MLA_NOPE = 128
MLA_ROPE = 64
MLA_V = 128
MLA_QK_PAD = 256

NA_HEADS = 16
NA_HEAD_DIM = 64
NA_WIN_ROWS = 8
NA_WIN_COLS = 16

SWA_HEADS = 16
SWA_KV_HEADS = 2
SWA_HEAD_DIM = 64
SWA_WINDOW = 128

TOKENS = SEQ + CTX_LEN
TILE = 256
N_TILES = TOKENS // TILE
N_LAT_TILES = SEQ // TILE
POST_TILES_PER_STEP = 2
STEP_ROWS = POST_TILES_PER_STEP * TILE
N_STEPS = -(-N_TILES // POST_TILES_PER_STEP)
CTX_STEP = N_LAT_TILES // POST_TILES_PER_STEP
TOKENS_PAD = N_STEPS * STEP_ROWS
LANES = 128
BF16_ROWS = 16
HEAD_PAIRS = 8
N_TABLE_VARIANTS = 4

MLA_TQ = 1024
MLA_TK = 2048
SOFTMAX_HEADROOM_LOG2 = 64.0
SOFTMAX_MIN_ROW_SUM = 2.0 ** -40
SOFTMAX_BOUND_SLACK = 1.02

NA_Q_ROWS = TILE
NA_NB_TOKENS = TILE
NA_NB_BLOCKS = NA_Q_ROWS // NA_NB_TOKENS + 2
SWA_Q_ROWS = TILE
SWA_NB_TOKENS = SWA_WINDOW
SWA_NB_BLOCKS = SWA_Q_ROWS // SWA_NB_TOKENS + 2

VMEM_LIMIT = 56 * 1024 * 1024

LOG2E = 1.4426950408889634

BF16 = jnp.bfloat16
F32 = jnp.float32


def _cparams(n_axes):
    return pltpu.CompilerParams(
        dimension_semantics=("parallel",) * n_axes, vmem_limit_bytes=VMEM_LIMIT)


def _const_spec(shape):
    nd = len(shape)
    return pl.BlockSpec(shape, lambda *_: (0,) * nd, pipeline_mode=pl.Buffered(1))


def _rms(x, g):
    return x * lax.rsqrt(jnp.mean(x * x, axis=-1, keepdims=True) + NORM_EPS) * g


def _dot(a, b):
    return jnp.dot(a, b, preferred_element_type=F32)


def _dot_nt(a, b):
    return lax.dot_general(a, b, (((1,), (1,)), ((), ())), preferred_element_type=F32)


def _ones_rows(n):
    row = lax.broadcasted_iota(jnp.int32, (BF16_ROWS, n), 0)
    return jnp.where(row == 0, 1.0, 0.0).astype(BF16)


def _mod_kernel(c_ref, w_ref, b_ref, o_ref):
    c = c_ref[...]
    a = c * jax.nn.sigmoid(c)
    o_ref[...] = _dot(a, w_ref[...]) + b_ref[...]


def _modulation(cc, w_mod, b_mod):
    tn = 1536
    return pl.pallas_call(
        _mod_kernel,
        grid=(DEPTH, 6 * D_MODEL // tn),
        in_specs=[
            pl.BlockSpec((8, D_MODEL), lambda l, n: (0, 0)),
            pl.BlockSpec((None, D_MODEL, tn), lambda l, n: (l, 0, n)),
            pl.BlockSpec((None, 1, tn), lambda l, n: (l, 0, n)),
        ],
        out_specs=pl.BlockSpec((None, 8, tn), lambda l, n: (l, 0, n)),
        out_shape=jax.ShapeDtypeStruct((DEPTH, 8, 6 * D_MODEL), F32),
        compiler_params=_cparams(2),
        name="modulation",
    )(cc, w_mod, b_mod.reshape(DEPTH, 1, 6 * D_MODEL))


def _mod_spec():
    return pl.BlockSpec((None, None, 6, D_MODEL),
                        lambda b, j: (b, (j == CTX_STEP).astype(jnp.int32), 0, 0))


def _x_spec():
    return pl.BlockSpec((None, STEP_ROWS, D_MODEL), lambda b, j: (b, j, 0))


def _rope_spec():
    return pl.BlockSpec((STEP_ROWS, LANES), lambda b, j: (j, 0))


def _stream_operands(stream):
    if not isinstance(stream, tuple):
        return [stream], [_x_spec()]
    return list(stream), [
        pl.BlockSpec((None, STEP_ROWS, D_MODEL),
                     lambda b, j: (b, jnp.minimum(j, CTX_STEP - 1), 0)),
        pl.BlockSpec((None, CTX_LEN, D_MODEL), lambda b, j: (b, 0, 0))]


def _stream_rows(refs):
    if len(refs) == 1:
        return refs[0][...]
    x_ref, ctx_ref = refs
    ctx_rows = jnp.concatenate([ctx_ref[...]] * (STEP_ROWS // CTX_LEN), axis=0)
    return jnp.where(pl.program_id(1) == CTX_STEP, ctx_rows, x_ref[...])


def _pre_mix(x, mod_ref, g_ref):
    shift = mod_ref[0:1, :]
    scale = mod_ref[1:2, :]
    h = _rms(x, g_ref[...]) * (1.0 + scale) + shift
    return h.astype(BF16)


class _Projection(NamedTuple):
    tile: Callable
    finish: Callable
    arrays: list
    in_specs: list
    out_specs: list
    out_shapes: list


def _tile_rows():
    return [slice(i * TILE, (i + 1) * TILE) for i in range(POST_TILES_PER_STEP)]


def _mla_tile(x, rows, in_refs, out_refs, *, scale):
    mod_ref, g_ref, cos_ref, sin_ref, w_in_ref, gq_ref, gkv_ref, wq_ref, wkv_ref = in_refs
    q_ref, k_ref, vt_ref, _ = out_refs
    h = _pre_mix(x, mod_ref, g_ref)
    a = _dot(h, w_in_ref[...])
    qn = _rms(a[:, :MLA_Q_LORA], gq_ref[...]).astype(BF16)
    kvn = _rms(a[:, MLA_Q_LORA:MLA_Q_LORA + MLA_KV_LORA], gkv_ref[...]).astype(BF16)
    cos = cos_ref[rows, :]
    sin = sin_ref[rows, :]
    k_rope = (a[:, 512:640] * cos + a[:, 640:768] * sin).astype(BF16)
    qq = _dot(qn, wq_ref[...])
    kv = _dot(kvn, wkv_ref[...])
    hn = MLA_HEADS * LANES
    k_rope_f = k_rope.astype(F32)
    rope_sq = jnp.sum(k_rope_f * k_rope_f, axis=-1, keepdims=True)
    tops = []
    for hd in range(MLA_HEADS):
        lo, hi = hd * LANES, (hd + 1) * LANES
        base = hd * MLA_QK_PAD
        q_ref[rows, base:base + LANES] = (qq[:, lo:hi] * scale).astype(BF16)
        q_rope = qq[:, hn + lo:hn + hi] * cos + qq[:, 2 * hn + lo:2 * hn + hi] * sin
        q_ref[rows, base + LANES:base + 2 * LANES] = (q_rope * scale).astype(BF16)
        k_nope = kv[:, lo:hi].astype(BF16)
        k_ref[rows, base:base + LANES] = k_nope
        k_ref[rows, base + LANES:base + 2 * LANES] = k_rope
        k_nope_f = k_nope.astype(F32)
        k_sq = jnp.sum(k_nope_f * k_nope_f, axis=-1, keepdims=True) + rope_sq
        tops.append(jnp.max(k_sq, axis=0, keepdims=True))
    vt_ref[:, rows] = kv[:, hn:].T.astype(BF16)
    return tops


def _mla_finish(tile_tops, out_refs):
    ksq_ref = out_refs[3]
    for hd in range(MLA_HEADS):
        top = functools.reduce(jnp.maximum, [tops[hd] for tops in tile_tops])
        ksq_ref[hd:hd + 1, :] = jnp.broadcast_to(top, (1, LANES))


def _mla_projection(mod, g_pre, cos, sin, w_in, g_q, g_kv, wq, wkv):
    hq = MLA_HEADS * MLA_QK_PAD
    hv = MLA_HEADS * MLA_V
    out_spec_q = pl.BlockSpec((None, STEP_ROWS, hq), lambda b, j: (b, j, 0))
    out_spec_v = pl.BlockSpec((None, hv, STEP_ROWS), lambda b, j: (b, 0, j))
    scale = float((MLA_NOPE + MLA_ROPE) ** -0.5) * LOG2E
    return _Projection(
        tile=functools.partial(_mla_tile, scale=scale),
        finish=_mla_finish,
        arrays=[mod, g_pre, cos, sin, w_in, g_q, g_kv, wq, wkv],
        in_specs=[_mod_spec(), _const_spec((1, D_MODEL)), _rope_spec(), _rope_spec(),
                  _const_spec(w_in.shape), _const_spec((1, MLA_Q_LORA)),
                  _const_spec((1, MLA_KV_LORA)), _const_spec(wq.shape), _const_spec(wkv.shape)],
        out_specs=[out_spec_q, out_spec_q, out_spec_v,
                   pl.BlockSpec((None, None, MLA_HEADS, LANES), lambda b, j: (b, j, 0, 0))],
        out_shapes=[jax.ShapeDtypeStruct((BATCH, TOKENS_PAD, hq), BF16),
                    jax.ShapeDtypeStruct((BATCH, TOKENS_PAD, hq), BF16),
                    jax.ShapeDtypeStruct((BATCH, hv, TOKENS_PAD), BF16),
                    jax.ShapeDtypeStruct((BATCH, N_STEPS, MLA_HEADS, LANES), F32)])


def _project_kernel(*refs, n_stream, proj):
    x = _stream_rows(refs[:n_stream])
    in_refs = refs[n_stream:n_stream + len(proj.arrays)]
    out_refs = refs[n_stream + len(proj.arrays):]
    proj.finish([proj.tile(x[r, :], r, in_refs, out_refs) for r in _tile_rows()], out_refs)


def _project(stream, proj):
    stream_arrays, stream_specs = _stream_operands(stream)
    return pl.pallas_call(
        functools.partial(_project_kernel, n_stream=len(stream_arrays), proj=proj),
        grid=(BATCH, N_STEPS),
        in_specs=stream_specs + proj.in_specs,
        out_specs=proj.out_specs,
        out_shape=proj.out_shapes,
        compiler_params=_cparams(2),
        name="projection",
    )(*stream_arrays, *proj.arrays)


def _softmax_update(s, vt, carry):
    m, l, acc = carry
    m_new = jnp.maximum(m, jnp.max(s, axis=0, keepdims=True))
    alpha = jnp.exp2(m - m_new)
    p = jnp.exp2(s - m_new)
    l = alpha * l + jnp.sum(p, axis=0, keepdims=True)
    acc = alpha * acc + _dot(vt, p.astype(BF16))
    return m_new, l, acc


def _mla_attn_kernel(q_ref, k_ref, vt_ref, ksq_ref, o_ref, *, chunks):
    q = q_ref[...]
    tq = q.shape[0]

    def logits(chunk):
        start, size = chunk
        return _dot_nt(k_ref[start:start + size, :], q)

    def single_pass(r):
        l = jnp.zeros((1, tq), F32)
        acc = jnp.zeros((MLA_V, tq), F32)
        s = logits(chunks[0])
        for c, (start, size) in enumerate(chunks):
            s_next = logits(chunks[c + 1]) if c + 1 < len(chunks) else None
            p = jnp.exp2(s - r)
            l = l + jnp.sum(p, axis=0, keepdims=True)
            acc = acc + _dot(vt_ref[:, start:start + size], p.astype(BF16))
            s = s_next
        return l, acc

    def online():
        carry = (jnp.full((1, tq), NEG_INF, F32), jnp.zeros((1, tq), F32),
                 jnp.zeros((MLA_V, tq), F32))
        s = logits(chunks[0])
        for c, (start, size) in enumerate(chunks):
            s_next = logits(chunks[c + 1]) if c + 1 < len(chunks) else None
            carry = _softmax_update(s, vt_ref[:, start:start + size], carry)
            s = s_next
        return carry[1], carry[2]

    if ksq_ref is None:
        l, acc = online()
    else:
        qf = q.astype(F32)
        ones = jnp.ones((8, q.shape[1]), BF16)
        q_sq = _dot_nt(ones, (qf * qf).astype(BF16))[0:1]
        k_sq = jnp.max(ksq_ref[...], axis=0, keepdims=True)[:, 0:1]
        bound = jnp.sqrt(q_sq * k_sq) * SOFTMAX_BOUND_SLACK
        l, acc = single_pass(bound - SOFTMAX_HEADROOM_LOG2)
        l, acc = lax.cond(jnp.min(l) >= SOFTMAX_MIN_ROW_SUM, lambda: (l, acc), online)
    o_ref[...] = (acc / l).T.astype(BF16)


def _mla_attn_ctx_kernel(q_ref, k_ref, vt_ref, latent_out_ref, o_ref):
    del latent_out_ref
    for h in range(MLA_HEADS):
        qk = slice(h * MLA_QK_PAD, (h + 1) * MLA_QK_PAD)
        hv = slice(h * MLA_V, (h + 1) * MLA_V)
        _mla_attn_kernel(q_ref.at[:, qk], k_ref.at[:, qk], vt_ref.at[hv, :], None,
                         o_ref.at[:, hv], chunks=((0, CTX_LEN),))


def _mla_attention(q, k, vt, ksq):
    hv = MLA_HEADS * MLA_V
    chunks = ((SEQ, CTX_LEN),) + tuple((c * MLA_TK, MLA_TK) for c in range(SEQ // MLA_TK))
    o = pl.pallas_call(
        functools.partial(_mla_attn_kernel, chunks=chunks),
        grid=(BATCH, MLA_HEADS, SEQ // MLA_TQ),
        in_specs=[
            pl.BlockSpec((None, MLA_TQ, MLA_QK_PAD), lambda b, h, i: (b, i, h)),
            pl.BlockSpec((None, TOKENS, MLA_QK_PAD), lambda b, h, i: (b, 0, h)),
            pl.BlockSpec((None, MLA_V, TOKENS), lambda b, h, i: (b, h, 0)),
            pl.BlockSpec((None, None, N_STEPS, LANES), lambda b, h, i: (b, h, 0, 0)),
        ],
        out_specs=pl.BlockSpec((None, MLA_TQ, MLA_V), lambda b, h, i: (b, i, h)),
        out_shape=jax.ShapeDtypeStruct((BATCH, TOKENS, hv), BF16),
        compiler_params=_cparams(3),
        name="mla_attn",
    )(q, k, vt, ksq)
    ctx_blk = SEQ // CTX_LEN
    hq = MLA_HEADS * MLA_QK_PAD
    return pl.pallas_call(
        _mla_attn_ctx_kernel,
        grid=(BATCH,),
        in_specs=[
            pl.BlockSpec((None, CTX_LEN, hq), lambda b: (b, ctx_blk, 0)),
            pl.BlockSpec((None, CTX_LEN, hq), lambda b: (b, ctx_blk, 0)),
            pl.BlockSpec((None, hv, CTX_LEN), lambda b: (b, 0, ctx_blk)),
            pl.BlockSpec(memory_space=pl.ANY),
        ],
        out_specs=pl.BlockSpec((None, CTX_LEN, hv), lambda b: (b, ctx_blk, 0)),
        out_shape=jax.ShapeDtypeStruct((BATCH, TOKENS, hv), BF16),
        input_output_aliases={3: 0},
        compiler_params=_cparams(1),
        name="mla_attn_ctx",
    )(q, k, vt, o)


def _group_sq_norms(x):
    w = x.shape[1]
    xf = x.astype(F32)
    group = lax.broadcasted_iota(jnp.int32, (w, LANES), 0) // NA_HEAD_DIM
    col = lax.broadcasted_iota(jnp.int32, (w, LANES), 1)
    indicator = jnp.where(group == col, 1.0, 0.0).astype(BF16)
    return _dot((xf * xf).astype(BF16), indicator)


def _norm_specs():
    return ([pl.BlockSpec((None, NA_HEADS, STEP_ROWS), lambda b, j: (b, 0, j)),
             pl.BlockSpec((None, None, 8, LANES), lambda b, j: (b, j, 0, 0))],
            [jax.ShapeDtypeStruct((BATCH, NA_HEADS, TOKENS_PAD), F32),
             jax.ShapeDtypeStruct((BATCH, N_STEPS, 8, LANES), F32)])


def _store_query_norms(rows, q, qsq_ref):
    qsq_ref[:, rows] = _group_sq_norms(q).T[:NA_HEADS, :]


def _group_finish(tile_tops, out_refs):
    ksq_ref = out_refs[-1]
    ksq_ref[...] = jnp.broadcast_to(functools.reduce(jnp.maximum, tile_tops), ksq_ref.shape)


def _na_tile(x, rows, in_refs, out_refs, *, scale):
    mod_ref, g_ref, w_ref = in_refs
    qk_ref, vt_ref, qsq_ref, _ = out_refs
    h = _pre_mix(x, mod_ref, g_ref)
    a = _dot(h, w_ref[...])
    hd = NA_HEADS * NA_HEAD_DIM
    q = a[:, :hd] * scale
    k = a[:, hd:2 * hd]
    qk_ref[rows, :hd] = q.astype(BF16)
    qk_ref[rows, hd:] = k.astype(BF16)
    vt_ref[:, rows] = a[:, 2 * hd:].T.astype(BF16)
    _store_query_norms(rows, q, qsq_ref)
    return jnp.max(_group_sq_norms(k), axis=0, keepdims=True)


def _na_projection(mod, g_pre, w):
    hd = NA_HEADS * NA_HEAD_DIM
    norm_specs, norm_shapes = _norm_specs()
    return _Projection(
        tile=functools.partial(_na_tile, scale=float(NA_HEAD_DIM ** -0.5) * LOG2E),
        finish=_group_finish,
        arrays=[mod, g_pre, w],
        in_specs=[_mod_spec(), _const_spec((1, D_MODEL)), _const_spec(w.shape)],
        out_specs=[pl.BlockSpec((None, STEP_ROWS, 2 * hd), lambda b, j: (b, j, 0)),
                   pl.BlockSpec((None, hd, STEP_ROWS), lambda b, j: (b, 0, j))] + norm_specs,
        out_shapes=[jax.ShapeDtypeStruct((BATCH, TOKENS_PAD, 2 * hd), BF16),
                    jax.ShapeDtypeStruct((BATCH, hd, TOKENS_PAD), BF16)] + norm_shapes)


def _swa_tile(x, rows, in_refs, out_refs, *, scale):
    mod_ref, g_ref, cos_ref, sin_ref, w_ref = in_refs
    q_ref, k_ref, vt_ref, qsq_ref, _ = out_refs
    h = _pre_mix(x, mod_ref, g_ref)
    a = _dot(h, w_ref[...])
    cos = cos_ref[rows, :]
    sin = sin_ref[rows, :]
    hd = SWA_HEADS * SWA_HEAD_DIM
    for p in range(HEAD_PAIRS):
        lo, hi = p * LANES, (p + 1) * LANES
        q = a[:, lo:hi] * cos + a[:, hd + lo:hd + hi] * sin
        q_ref[rows, lo:hi] = (q * scale).astype(BF16)
    kw = 2 * LANES
    for p in range(2):
        lo, hi = p * LANES, (p + 1) * LANES
        k = a[:, 2 * hd + lo:2 * hd + hi] * cos + a[:, 2 * hd + kw + lo:2 * hd + kw + hi] * sin
        k_ref[rows, lo:hi] = k.astype(BF16)
    vt_ref[:, rows] = a[:, 2 * hd + 2 * kw:].T.astype(BF16)
    _store_query_norms(rows, q_ref[rows, :], qsq_ref)
    return jnp.max(_group_sq_norms(k_ref[rows, :]), axis=0, keepdims=True)


def _swa_projection(mod, g_pre, cos, sin, w):
    hd = SWA_HEADS * SWA_HEAD_DIM
    kw = 2 * LANES
    norm_specs, norm_shapes = _norm_specs()
    return _Projection(
        tile=functools.partial(_swa_tile, scale=float(SWA_HEAD_DIM ** -0.5) * LOG2E),
        finish=_group_finish,
        arrays=[mod, g_pre, cos, sin, w],
        in_specs=[_mod_spec(), _const_spec((1, D_MODEL)), _rope_spec(), _rope_spec(),
                  _const_spec(w.shape)],
        out_specs=[pl.BlockSpec((None, STEP_ROWS, hd), lambda b, j: (b, j, 0)),
                   pl.BlockSpec((None, STEP_ROWS, kw), lambda b, j: (b, j, 0)),
                   pl.BlockSpec((None, kw, STEP_ROWS), lambda b, j: (b, 0, j))] + norm_specs,
        out_shapes=[jax.ShapeDtypeStruct((BATCH, TOKENS_PAD, hd), BF16),
                    jax.ShapeDtypeStruct((BATCH, TOKENS_PAD, kw), BF16),
                    jax.ShapeDtypeStruct((BATCH, kw, TOKENS_PAD), BF16)] + norm_shapes)


def _local_attn_kernel(*refs, n_nb, kv_slot, k_norm_lane, per_head_table, has_sink):
    q_ref = refs[0]
    k_refs = refs[1:1 + n_nb]
    kc_ref = refs[1 + n_nb]
    vt_refs = refs[2 + n_nb:2 + 2 * n_nb]
    vtc_ref = refs[2 + 2 * n_nb]
    tbl_ref, head_ref, qsq_ref, ksq_ref, o_ref = refs[3 + 2 * n_nb:]
    hdim = NA_HEAD_DIM
    tq = q_ref.shape[0]
    low = lax.broadcasted_iota(jnp.int32, (tq, LANES), 1) < hdim

    def pair_logits(p):
        qp = q_ref[:, p * LANES:(p + 1) * LANES]
        zero = jnp.zeros_like(qp)
        q2 = jnp.concatenate([jnp.where(low, qp, zero), jnp.where(low, zero, qp)], axis=0)
        ks = slice(kv_slot(p) * LANES, (kv_slot(p) + 1) * LANES)
        keys = jnp.concatenate([r[:, ks] for r in k_refs] + [kc_ref[:, ks]], axis=0)
        return _dot_nt(keys, q2)

    def attend(reference):
        s = pair_logits(0)
        n_nb_keys = n_nb * k_refs[0].shape[0]
        outs, l_min = [], None
        for p in range(HEAD_PAIRS):
            s_next = pair_logits(p + 1) if p + 1 < HEAD_PAIRS else None
            pair_out = []
            for half in range(2):
                hd = 2 * p + half
                cols = slice(half * tq, (half + 1) * tq)
                s_n = s[:n_nb_keys, cols] + tbl_ref[hd if per_head_table else 0]
                s_c = s[n_nb_keys:, cols]
                r = reference(hd, s_n, s_c)
                probs = jnp.concatenate([jnp.exp2(s_n - r).astype(BF16),
                                         jnp.exp2(s_c - r).astype(BF16)], axis=0)
                lo = kv_slot(p) * LANES + half * hdim
                vt = jnp.concatenate([v[lo:lo + hdim, :] for v in vt_refs]
                                     + [vtc_ref[lo:lo + hdim, :]], axis=1)
                vt = jnp.concatenate([vt, _ones_rows(vt.shape[1])], axis=0)
                acc = _dot(vt, probs)
                l = acc[hdim:hdim + 1]
                if has_sink:
                    l = l + jnp.exp2(head_ref[hd:hd + 1, :] - r)
                l_min = l if l_min is None else jnp.minimum(l_min, l)
                pair_out.append(acc[:hdim] / l)
            outs.append(jnp.concatenate(pair_out, axis=0).T.astype(BF16))
            s = s_next
        return tuple(outs), l_min

    def row_max(hd, s_n, s_c):
        m = jnp.maximum(jnp.max(s_n, axis=0, keepdims=True), jnp.max(s_c, axis=0, keepdims=True))
        return jnp.maximum(m, head_ref[hd:hd + 1, :]) if has_sink else m

    k_sq = jnp.max(ksq_ref[...], axis=0)[0:1, :]

    def bound(hd, s_n, s_c):
        lane = k_norm_lane(hd)
        qk = jnp.sqrt(qsq_ref[hd:hd + 1, :] * k_sq[:, lane:lane + 1]) * SOFTMAX_BOUND_SLACK
        extra = head_ref[hd:hd + 1, :]
        top = jnp.maximum(qk, extra) if has_sink else qk + extra
        return top - SOFTMAX_HEADROOM_LOG2

    outs, l_min = attend(bound)
    outs = lax.cond(jnp.min(l_min) >= SOFTMAX_MIN_ROW_SUM, lambda: outs,
                    lambda: attend(row_max)[0])
    for p, pair in enumerate(outs):
        o_ref[:, p * LANES:(p + 1) * LANES] = pair


def _local_attention(q, q_col, k, k_col, k_width, vt, qsq, ksq, q_rows, nb_tokens, n_nb, table,
                     head_rows, has_sink, kv_slot, k_norm_lane, name):
    hd = NA_HEADS * NA_HEAD_DIM
    per_head_table = table.shape[1] > 1
    n_steps = -(-TOKENS // q_rows)
    lat_hi = SEQ // q_rows - 1
    per_tile = q_rows // nb_tokens
    nb_hi = SEQ // nb_tokens - 1

    def nb_index(j, n):
        return jnp.clip(j * per_tile - 1 + n, 0, nb_hi)

    def k_spec(n):
        return pl.BlockSpec((None, nb_tokens, k_width), lambda b, j: (b, nb_index(j, n), k_col))

    def vt_spec(n):
        return pl.BlockSpec((None, k_width, nb_tokens), lambda b, j: (b, 0, nb_index(j, n)))

    def variant(j):
        return jnp.where(j == 0, 0, jnp.where(j == lat_hi, 2, jnp.where(j > lat_hi, 3, 1)))

    tbl_spec = pl.BlockSpec((None,) + table.shape[1:], lambda b, j: (variant(j), 0, 0, 0))
    in_specs = ([pl.BlockSpec((None, q_rows, hd), lambda b, j: (b, j, q_col))]
                + [k_spec(n) for n in range(n_nb)]
                + [pl.BlockSpec((None, CTX_LEN, k_width), lambda b, j: (b, N_LAT_TILES, k_col))]
                + [vt_spec(n) for n in range(n_nb)]
                + [pl.BlockSpec((None, k_width, CTX_LEN), lambda b, j: (b, 0, N_LAT_TILES))]
                + [tbl_spec, _const_spec((NA_HEADS, q_rows)),
                   pl.BlockSpec((None, NA_HEADS, q_rows), lambda b, j: (b, 0, j)),
                   pl.BlockSpec((None, N_STEPS, 8, LANES), lambda b, j: (b, 0, 0, 0))])
    return pl.pallas_call(
        functools.partial(_local_attn_kernel, n_nb=n_nb, kv_slot=kv_slot,
                          k_norm_lane=k_norm_lane, per_head_table=per_head_table,
                          has_sink=has_sink),
        grid=(BATCH, n_steps),
        in_specs=in_specs,
        out_specs=pl.BlockSpec((None, q_rows, hd), lambda b, j: (b, j, 0)),
        out_shape=jax.ShapeDtypeStruct((BATCH, n_steps * q_rows, hd), BF16),
        compiler_params=_cparams(2),
        name=name,
    )(q, *([k] * (n_nb + 1)), *([vt] * (n_nb + 1)), table, head_rows, qsq, ksq)


POST_FFN_OPERANDS = 9


def _post_ffn_kernel(*refs, n_stream, proj):
    x = _stream_rows(refs[:n_stream])
    (oa_ref, ob_ref, mod_ref, gpm_ref, gpf_ref, gof_ref, wo_ref, win_ref,
     wout_ref) = refs[n_stream:n_stream + POST_FFN_OPERANDS]
    rest = refs[n_stream + POST_FFN_OPERANDS:]
    n_proj_in = len(proj.arrays) if proj else 0
    proj_in, y_ref, proj_out, act_ref = (rest[:n_proj_in], rest[n_proj_in],
                                         rest[n_proj_in + 1:-1], rest[-1])
    gate_m = mod_ref[2:3, :]
    shift_f = mod_ref[3:4, :]
    scale_f = mod_ref[4:5, :]
    gate_f = mod_ref[5:6, :]
    rows = _tile_rows()
    mixed = [_dot(o_ref[...], wo_ref[...]) for o_ref in (oa_ref, ob_ref)]
    x1 = [x[r, :] + gate_m * _rms(mx, gpm_ref[...]) for r, mx in zip(rows, mixed)]
    h = [(_rms(xi, gpf_ref[...]) * (1.0 + scale_f) + shift_f).astype(BF16) for xi in x1]
    chunk = 256
    y = []
    for i, r in enumerate(rows):
        for f in range(D_FF // chunk):
            lo, hi = f * chunk, (f + 1) * chunk
            gate = _dot(h[i], win_ref[:, lo:hi])
            up = _dot(h[i], win_ref[:, D_FF + lo:D_FF + hi])
            act_ref[i, :, lo:hi] = (gate * jax.nn.sigmoid(gate) * up).astype(BF16)
        ffn = _dot(act_ref[i], wout_ref[...])
        y.append(x1[i] + gate_f * _rms(ffn, gof_ref[...]))
        y_ref[r, :] = y[i]
    if proj:
        proj.finish([proj.tile(yi, r, proj_in, proj_out) for yi, r in zip(y, rows)], proj_out)


def _post_ffn(stream, o, mod, g_post_mix, g_pre_ffn, g_post_ffn, wo, win, wout, proj):
    vec = _const_spec((1, D_MODEL))
    n_steps = N_STEPS if proj else SEQ // STEP_ROWS
    stream_arrays, stream_specs = _stream_operands(stream)

    def o_spec(i):
        return pl.BlockSpec(
            (None, TILE, D_MODEL),
            lambda b, j: (b, jnp.minimum(POST_TILES_PER_STEP * j + i, N_LAT_TILES), 0))

    out = pl.pallas_call(
        functools.partial(_post_ffn_kernel, n_stream=len(stream_arrays), proj=proj),
        grid=(BATCH, n_steps),
        in_specs=stream_specs + [
            o_spec(0), o_spec(1), _mod_spec(), vec, vec, vec,
            _const_spec(wo.shape), _const_spec(win.shape), _const_spec(wout.shape)
        ] + (proj.in_specs if proj else []),
        out_specs=[_x_spec()] + (proj.out_specs if proj else []),
        out_shape=[jax.ShapeDtypeStruct((BATCH, n_steps * STEP_ROWS, D_MODEL), F32)]
        + (proj.out_shapes if proj else []),
        scratch_shapes=[pltpu.VMEM((POST_TILES_PER_STEP, TILE, D_FF), BF16)],
        compiler_params=_cparams(2),
        name="post_ffn",
    )(*stream_arrays, o, o, mod, g_post_mix, g_pre_ffn, g_post_ffn, wo, win, wout,
      *(proj.arrays if proj else []))
    return out if proj else out[0]


def _rope_tables():
    t = np.arange(SEQ)
    row = (t // GRID_W).astype(np.float32)
    col = (t % GRID_W).astype(np.float32)
    n_freq = MLA_ROPE // 4
    inv = np.float32(ROPE_THETA) ** (-(np.arange(n_freq, dtype=np.float32) / np.float32(n_freq)))
    ang = np.concatenate([row[:, None] * inv, col[:, None] * inv], axis=-1)
    cos = np.cos(ang.astype(np.float64)).astype(np.float32)
    sin = np.sin(ang.astype(np.float64)).astype(np.float32)
    cos_t = np.concatenate([cos, cos, cos, cos], axis=-1)
    sin_t = np.concatenate([-sin, sin, -sin, sin], axis=-1)
    no_pos = TOKENS_PAD - SEQ
    cos_t = np.concatenate([cos_t, np.ones((no_pos, LANES), np.float32)], axis=0)
    sin_t = np.concatenate([sin_t, np.zeros((no_pos, LANES), np.float32)], axis=0)
    return jnp.asarray(cos_t), jnp.asarray(sin_t)


NA_ROWS_PER_TILE = TILE // GRID_W
NA_BIAS_ROWS = 2 * NA_WIN_ROWS - 1
NA_BIAS_COLS = 2 * NA_WIN_COLS - 1
NA_NB_KEYS = NA_NB_BLOCKS * NA_NB_TOKENS


def _na_table_kernel(u_ref, o_ref):
    rows = SEQ // GRID_W
    kc = lax.broadcasted_iota(jnp.int32, (GRID_W, LANES), 0)
    lane = lax.broadcasted_iota(jnp.int32, (GRID_W, LANES), 1)
    qc = lane % GRID_W
    col_start = jnp.clip(qc - NA_WIN_COLS // 2, 0, GRID_W - NA_WIN_COLS)
    col_ok = (kc >= col_start) & (kc < col_start + NA_WIN_COLS)
    first_half = lane < GRID_W
    neg = jnp.full((GRID_W, LANES), NEG_INF, F32)
    toeplitz = {}
    for i in range(1, NA_BIAS_ROWS):
        row = jnp.broadcast_to(u_ref[i:i + 1, :], (GRID_W, LANES))
        toeplitz[i] = pltpu.roll(row, LANES - (NA_WIN_COLS - 1), 1, stride=1,
                                 stride_axis=0) * LOG2E
    for v, tile in enumerate((0, 1, N_LAT_TILES - 1)):
        for kl in range(NA_NB_BLOCKS * NA_ROWS_PER_TILE):
            kr = NA_ROWS_PER_TILE * (tile - 1) + kl
            for qp in range(NA_ROWS_PER_TILE // 2):
                oks = []
                for half in range(2):
                    qr = NA_ROWS_PER_TILE * tile + 2 * qp + half
                    row_start = min(max(qr - NA_WIN_ROWS // 2, 0), rows - NA_WIN_ROWS)
                    oks.append(row_start <= kr < row_start + NA_WIN_ROWS)
                if oks[0] and oks[1]:
                    ok = col_ok
                elif oks[0]:
                    ok = col_ok & first_half
                elif oks[1]:
                    ok = col_ok & jnp.logical_not(first_half)
                else:
                    ok = None
                i = kl - NA_ROWS_PER_TILE - 2 * qp + NA_WIN_ROWS - 1
                blk = neg if ok is None else jnp.where(ok, toeplitz[i], neg)
                o_ref[v, kl * GRID_W:(kl + 1) * GRID_W, qp * LANES:(qp + 1) * LANES] = blk
    o_ref[N_TABLE_VARIANTS - 1] = jnp.full((NA_NB_KEYS, TILE), NEG_INF, F32)


def _na_table(rpb):
    pad = jnp.full((NA_HEADS, NA_BIAS_ROWS, GRID_W - NA_BIAS_COLS), NEG_INF, F32)
    rev = jnp.concatenate([rpb.astype(F32)[:, :, ::-1], pad], axis=-1)
    off = jnp.full((NA_HEADS, 1, GRID_W), NEG_INF, F32)
    left = jnp.concatenate([rev, off], axis=1)
    right = jnp.concatenate([off, rev], axis=1)
    u = jnp.concatenate([left, right], axis=-1)
    return pl.pallas_call(
        _na_table_kernel,
        grid=(NA_HEADS,),
        in_specs=[pl.BlockSpec((None, NA_BIAS_ROWS + 1, LANES), lambda h: (h, 0, 0))],
        out_specs=pl.BlockSpec((N_TABLE_VARIANTS, None, NA_NB_KEYS, TILE),
                               lambda h: (0, h, 0, 0)),
        out_shape=jax.ShapeDtypeStruct((N_TABLE_VARIANTS, NA_HEADS, NA_NB_KEYS, TILE), F32),
        compiler_params=_cparams(1),
        name="na_table",
    )(u)


def _swa_table():
    j = np.array([0, 1, SEQ // SWA_Q_ROWS - 1])[:, None, None]
    qpos = SWA_Q_ROWS * j + np.arange(SWA_Q_ROWS)[None, None, :]
    kpos = (SWA_Q_ROWS * j - SWA_NB_TOKENS
            + np.arange(SWA_NB_BLOCKS * SWA_NB_TOKENS)[None, :, None])
    ok = (np.abs(kpos - qpos) <= SWA_WINDOW) & (kpos >= 0) & (kpos < SEQ)
    tbl = np.where(ok, 0.0, NEG_INF).astype(np.float32)
    tbl = np.concatenate([tbl, np.full((1,) + tbl.shape[1:], NEG_INF, np.float32)], axis=0)
    return jnp.asarray(tbl[:, None])


def _swap_halves(w):
    half = w.shape[-1] // 2
    return jnp.concatenate([w[..., half:], w[..., :half]], axis=-1)


def _mla_weights(w_in, w_uq, w_ukv):
    lat = MLA_Q_LORA + MLA_KV_LORA
    kpe = w_in[:, lat:]
    z = jnp.zeros((D_MODEL, LANES - MLA_ROPE), w_in.dtype)
    w_in_x = jnp.concatenate([w_in[:, :lat], kpe, z, _swap_halves(kpe), z], axis=1)
    uq = w_uq.reshape(MLA_Q_LORA, MLA_HEADS, MLA_NOPE + MLA_ROPE)
    rope = uq[:, :, MLA_NOPE:]
    zq = jnp.zeros((MLA_Q_LORA, MLA_HEADS, LANES - MLA_ROPE), w_uq.dtype)
    flat = lambda a: a.reshape(a.shape[0], -1)
    wq_x = jnp.concatenate([flat(uq[:, :, :MLA_NOPE]),
                            flat(jnp.concatenate([rope, zq], axis=-1)),
                            flat(jnp.concatenate([_swap_halves(rope), zq], axis=-1))], axis=1)
    ukv = w_ukv.reshape(MLA_KV_LORA, MLA_HEADS, MLA_NOPE + MLA_V)
    wkv_x = jnp.concatenate([flat(ukv[:, :, :MLA_NOPE]), flat(ukv[:, :, MLA_NOPE:])], axis=1)
    return w_in_x.astype(BF16), wq_x.astype(BF16), wkv_x.astype(BF16)


def _swa_weights(w_qkv):
    hd = SWA_HEADS * SWA_HEAD_DIM
    kd = SWA_KV_HEADS * SWA_HEAD_DIM
    wq = w_qkv[:, :hd].reshape(D_MODEL, SWA_HEADS, SWA_HEAD_DIM)
    wk = w_qkv[:, hd:hd + kd].reshape(D_MODEL, SWA_KV_HEADS, 1, SWA_HEAD_DIM)
    wv = w_qkv[:, hd + kd:].reshape(D_MODEL, SWA_KV_HEADS, 1, SWA_HEAD_DIM)
    dup = lambda a: jnp.broadcast_to(a, (D_MODEL, SWA_KV_HEADS, 2, SWA_HEAD_DIM)).reshape(
        D_MODEL, -1)
    flat = lambda a: a.reshape(D_MODEL, -1)
    w = jnp.concatenate([flat(wq), flat(_swap_halves(wq)), dup(wk), dup(_swap_halves(wk)),
                         dup(wv)], axis=1)
    return w.astype(BF16)


def kernel(x, c, ctx, c_ctx, w_mod, b_mod, g_pre_mix, g_post_mix, g_pre_ffn, g_post_ffn,
           w_ffn_in, w_ffn_out, mla_w_in, mla_g_q, mla_w_uq, mla_g_kv, mla_w_ukv, mla_w_o,
           na_w_qkv, na_rpb, na_w_o, swa_w_qkv, swa_sink, swa_w_o):
    assert x.shape == (BATCH, SEQ, D_MODEL) and ctx.shape == (BATCH, CTX_LEN, D_MODEL)
    stream = (x, ctx)
    cc = jnp.concatenate([c, c_ctx[None], jnp.zeros((8 - BATCH - 1, D_MODEL), F32)], axis=0)
    mods = _modulation(cc, w_mod, b_mod)
    mods = mods.reshape(DEPTH, 8, 6, D_MODEL)
    mods = jnp.stack([mods[:, :BATCH],
                      jnp.broadcast_to(mods[:, BATCH:BATCH + 1], (DEPTH, BATCH, 6, D_MODEL))],
                     axis=2)
    cos_t, sin_t = _rope_tables()
    row = lambda g: g.reshape(1, -1)

    def projection(i):
        kind, j = i % 3, i // 3
        mod, g_pre = mods[i], row(g_pre_mix[i])
        if kind == 0:
            w_in_x, wq_x, wkv_x = _mla_weights(mla_w_in[j], mla_w_uq[j], mla_w_ukv[j])
            return _mla_projection(mod, g_pre, cos_t, sin_t, w_in_x, row(mla_g_q[j]),
                                   row(mla_g_kv[j]), wq_x, wkv_x)
        if kind == 1:
            return _na_projection(mod, g_pre, na_w_qkv[j].astype(BF16))
        return _swa_projection(mod, g_pre, cos_t, sin_t, _swa_weights(swa_w_qkv[j]))

    projected = _project(stream, projection(0))
    for i in range(DEPTH):
        last = i == DEPTH - 1
        kind, j = i % 3, i // 3
        if kind == 0:
            q, k, vt, ksq = projected
            o = _mla_attention(q, k, vt, jnp.swapaxes(ksq, 1, 2))
            w_o = mla_w_o[j]
        elif kind == 1:
            qk, vt, qsq, ksq = projected
            bias_top = jnp.maximum(jnp.max(na_rpb[j].astype(F32), axis=(1, 2)), 0.0) * LOG2E
            o = _local_attention(qk, 0, qk, 1, NA_HEADS * NA_HEAD_DIM, vt, qsq, ksq, NA_Q_ROWS,
                                 NA_NB_TOKENS, NA_NB_BLOCKS, _na_table(na_rpb[j]),
                                 jnp.broadcast_to(bias_top[:, None], (NA_HEADS, NA_Q_ROWS)),
                                 False, lambda p: p, lambda hd: hd, "na_attn")
            w_o = na_w_o[j]
        else:
            q, k, vt, qsq, ksq = projected
            sink = jnp.broadcast_to((swa_sink[j].astype(F32) * LOG2E)[:, None],
                                    (SWA_HEADS, SWA_Q_ROWS))
            heads_per_kv = SWA_HEADS // SWA_KV_HEADS
            o = _local_attention(q, 0, k, 0, 2 * LANES, vt, qsq, ksq, SWA_Q_ROWS,
                                 SWA_NB_TOKENS, SWA_NB_BLOCKS, _swa_table(), sink, True,
                                 lambda p: 2 * p // heads_per_kv,
                                 lambda hd: 2 * (hd // heads_per_kv), "swa_attn")
            w_o = swa_w_o[j]
        out = _post_ffn(stream, o, mods[i], row(g_post_mix[i]), row(g_pre_ffn[i]),
                        row(g_post_ffn[i]), w_o.astype(BF16), w_ffn_in[i].astype(BF16),
                        w_ffn_out[i].astype(BF16), None if last else projection(i + 1))
        if last:
            return out
        stream, projected = out[0], out[1:]
```

```python
import functools
from typing import Callable, NamedTuple

import jax
import jax.numpy as jnp
import numpy as np
from jax import lax
from jax.experimental import pallas as pl
from jax.experimental.pallas import tpu as pltpu

D_MODEL = 1024
BATCH = 4
SEQ = 8192
DEPTH = 4
GRID_W = 64
CTX_LEN = 256
D_FF = 2816
NORM_EPS = 1e-6
ROPE_THETA = 10000.0
NEG_INF = -1e30

MLA_HEADS = 8
MLA_Q_LORA = 256
MLA_KV_LORA = 256
MLA_NOPE = 128
MLA_ROPE = 64
MLA_V = 128
MLA_QK_PAD = 256

NA_HEADS = 16
NA_HEAD_DIM = 64
NA_WIN_ROWS = 8
NA_WIN_COLS = 16

SWA_HEADS = 16
SWA_KV_HEADS = 2
SWA_HEAD_DIM = 64
SWA_WINDOW = 128

TOKENS = SEQ + CTX_LEN
TILE = 256
N_TILES = TOKENS // TILE
N_LAT_TILES = SEQ // TILE
POST_TILES_PER_STEP = 2
STEP_ROWS = POST_TILES_PER_STEP * TILE
N_STEPS = -(-N_TILES // POST_TILES_PER_STEP)
CTX_STEP = N_LAT_TILES // POST_TILES_PER_STEP
TOKENS_PAD = N_STEPS * STEP_ROWS
LANES = 128
SUBLANES = 8
BF16_ROWS = 16
HEAD_PAIRS = 8
N_TABLE_VARIANTS = 4

MOD_COLS_PER_STEP = 1536
FFN_CHUNK = 256
MLA_TQ = 1024
MLA_TK = 2048
SOFTMAX_HEADROOM_LOG2 = 64.0
SOFTMAX_MIN_ROW_SUM = 2.0 ** -40
SOFTMAX_BOUND_SLACK = 1.02

NA_Q_ROWS = TILE
NA_NB_TOKENS = TILE
NA_NB_BLOCKS = NA_Q_ROWS // NA_NB_TOKENS + 2
SWA_Q_ROWS = TILE
SWA_NB_TOKENS = SWA_WINDOW
SWA_NB_BLOCKS = SWA_Q_ROWS // SWA_NB_TOKENS + 2

VMEM_LIMIT = 56 * 1024 * 1024

LOG2E = 1.4426950408889634

BF16 = jnp.bfloat16
F32 = jnp.float32


def _cparams(n_axes):
    return pltpu.CompilerParams(
        dimension_semantics=("parallel",) * n_axes, vmem_limit_bytes=VMEM_LIMIT)


def _const_spec(shape):
    nd = len(shape)
    return pl.BlockSpec(shape, lambda *_: (0,) * nd, pipeline_mode=pl.Buffered(1))


def _rms(x, g):
    return x * lax.rsqrt(jnp.mean(x * x, axis=-1, keepdims=True) + NORM_EPS) * g


def _dot(a, b):
    return jnp.dot(a, b, preferred_element_type=F32)


def _dot_nt(a, b):
    return lax.dot_general(a, b, (((1,), (1,)), ((), ())), preferred_element_type=F32)


def _ones_rows(n):
    row = lax.broadcasted_iota(jnp.int32, (BF16_ROWS, n), 0)
    return jnp.where(row == 0, 1.0, 0.0).astype(BF16)


def _mod_kernel(c_ref, w_ref, b_ref, o_ref):
    c = c_ref[...]
    a = c * jax.nn.sigmoid(c)
    o_ref[...] = _dot(a, w_ref[...]) + b_ref[...]


def _modulation(cc, w_mod, b_mod):
    tn = MOD_COLS_PER_STEP
    return pl.pallas_call(
        _mod_kernel,
        grid=(DEPTH, 6 * D_MODEL // tn),
        in_specs=[
            pl.BlockSpec((SUBLANES, D_MODEL), lambda l, n: (0, 0)),
            pl.BlockSpec((None, D_MODEL, tn), lambda l, n: (l, 0, n)),
            pl.BlockSpec((None, 1, tn), lambda l, n: (l, 0, n)),
        ],
        out_specs=pl.BlockSpec((None, SUBLANES, tn), lambda l, n: (l, 0, n)),
        out_shape=jax.ShapeDtypeStruct((DEPTH, SUBLANES, 6 * D_MODEL), F32),
        compiler_params=_cparams(2),
        name="modulation",
    )(cc, w_mod, b_mod.reshape(DEPTH, 1, 6 * D_MODEL))


def _mod_spec():
    return pl.BlockSpec((None, None, 6, D_MODEL),
                        lambda b, j: (b, (j == CTX_STEP).astype(jnp.int32), 0, 0))


def _x_spec():
    return pl.BlockSpec((None, STEP_ROWS, D_MODEL), lambda b, j: (b, j, 0))


def _rope_spec():
    return pl.BlockSpec((STEP_ROWS, LANES), lambda b, j: (j, 0))


def _stream_operands(stream):
    if not isinstance(stream, tuple):
        return [stream], [_x_spec()]
    return list(stream), [
        pl.BlockSpec((None, STEP_ROWS, D_MODEL),
                     lambda b, j: (b, jnp.minimum(j, CTX_STEP - 1), 0)),
        pl.BlockSpec((None, CTX_LEN, D_MODEL), lambda b, j: (b, 0, 0))]


def _stream_rows(refs):
    if len(refs) == 1:
        return refs[0][...]
    x_ref, ctx_ref = refs
    ctx_rows = jnp.concatenate([ctx_ref[...]] * (STEP_ROWS // CTX_LEN), axis=0)
    return jnp.where(pl.program_id(1) == CTX_STEP, ctx_rows, x_ref[...])


def _pre_mix(x, mod_ref, g_ref):
    shift = mod_ref[0:1, :]
    scale = mod_ref[1:2, :]
    h = _rms(x, g_ref[...]) * (1.0 + scale) + shift
    return h.astype(BF16)


class _Projection(NamedTuple):
    tile: Callable
    finish: Callable
    arrays: list
    in_specs: list
    out_specs: list
    out_shapes: list


def _tile_rows():
    return [slice(i * TILE, (i + 1) * TILE) for i in range(POST_TILES_PER_STEP)]


def _mla_tile(x, rows, in_refs, out_refs, *, scale):
    mod_ref, g_ref, cos_ref, sin_ref, w_in_ref, gq_ref, gkv_ref, wq_ref, wkv_ref = in_refs
    q_ref, k_ref, vt_ref, _ = out_refs
    h = _pre_mix(x, mod_ref, g_ref)
    a = _dot(h, w_in_ref[...])
    qn = _rms(a[:, :MLA_Q_LORA], gq_ref[...]).astype(BF16)
    kvn = _rms(a[:, MLA_Q_LORA:MLA_Q_LORA + MLA_KV_LORA], gkv_ref[...]).astype(BF16)
    cos = cos_ref[rows, :]
    sin = sin_ref[rows, :]
    k_rope = (a[:, 512:640] * cos + a[:, 640:768] * sin).astype(BF16)
    qq = _dot(qn, wq_ref[...])
    kv = _dot(kvn, wkv_ref[...])
    hn = MLA_HEADS * LANES
    k_rope_f = k_rope.astype(F32)
    rope_sq = jnp.sum(k_rope_f * k_rope_f, axis=-1, keepdims=True)
    tops = []
    for hd in range(MLA_HEADS):
        lo, hi = hd * LANES, (hd + 1) * LANES
        base = hd * MLA_QK_PAD
        q_ref[rows, base:base + LANES] = (qq[:, lo:hi] * scale).astype(BF16)
        q_rope = qq[:, hn + lo:hn + hi] * cos + qq[:, 2 * hn + lo:2 * hn + hi] * sin
        q_ref[rows, base + LANES:base + 2 * LANES] = (q_rope * scale).astype(BF16)
        k_nope = kv[:, lo:hi].astype(BF16)
        k_ref[rows, base:base + LANES] = k_nope
        k_ref[rows, base + LANES:base + 2 * LANES] = k_rope
        k_nope_f = k_nope.astype(F32)
        k_sq = jnp.sum(k_nope_f * k_nope_f, axis=-1, keepdims=True) + rope_sq
        tops.append(jnp.max(k_sq, axis=0, keepdims=True))
    vt_ref[:, rows] = kv[:, hn:].T.astype(BF16)
    return tops


def _mla_finish(tile_tops, out_refs):
    ksq_ref = out_refs[3]
    for hd in range(MLA_HEADS):
        top = functools.reduce(jnp.maximum, [tops[hd] for tops in tile_tops])
        ksq_ref[hd:hd + 1, :] = jnp.broadcast_to(top, (1, LANES))


def _mla_projection(mod, g_pre, cos, sin, w_in, g_q, g_kv, wq, wkv):
    hq = MLA_HEADS * MLA_QK_PAD
    hv = MLA_HEADS * MLA_V
    out_spec_q = pl.BlockSpec((None, STEP_ROWS, hq), lambda b, j: (b, j, 0))
    out_spec_v = pl.BlockSpec((None, hv, STEP_ROWS), lambda b, j: (b, 0, j))
    scale = float((MLA_NOPE + MLA_ROPE) ** -0.5) * LOG2E
    return _Projection(
        tile=functools.partial(_mla_tile, scale=scale),
        finish=_mla_finish,
        arrays=[mod, g_pre, cos, sin, w_in, g_q, g_kv, wq, wkv],
        in_specs=[_mod_spec(), _const_spec((1, D_MODEL)), _rope_spec(), _rope_spec(),
                  _const_spec(w_in.shape), _const_spec((1, MLA_Q_LORA)),
                  _const_spec((1, MLA_KV_LORA)), _const_spec(wq.shape), _const_spec(wkv.shape)],
        out_specs=[out_spec_q, out_spec_q, out_spec_v,
                   pl.BlockSpec((None, None, MLA_HEADS, LANES), lambda b, j: (b, j, 0, 0))],
        out_shapes=[jax.ShapeDtypeStruct((BATCH, TOKENS_PAD, hq), BF16),
                    jax.ShapeDtypeStruct((BATCH, TOKENS_PAD, hq), BF16),
                    jax.ShapeDtypeStruct((BATCH, hv, TOKENS_PAD), BF16),
                    jax.ShapeDtypeStruct((BATCH, N_STEPS, MLA_HEADS, LANES), F32)])


def _project_kernel(*refs, n_stream, proj):
    x = _stream_rows(refs[:n_stream])
    in_refs = refs[n_stream:n_stream + len(proj.arrays)]
    out_refs = refs[n_stream + len(proj.arrays):]
    proj.finish([proj.tile(x[r, :], r, in_refs, out_refs) for r in _tile_rows()], out_refs)


def _project(stream, proj):
    stream_arrays, stream_specs = _stream_operands(stream)
    return pl.pallas_call(
        functools.partial(_project_kernel, n_stream=len(stream_arrays), proj=proj),
        grid=(BATCH, N_STEPS),
        in_specs=stream_specs + proj.in_specs,
        out_specs=proj.out_specs,
        out_shape=proj.out_shapes,
        compiler_params=_cparams(2),
        name="projection",
    )(*stream_arrays, *proj.arrays)


def _softmax_update(s, vt, carry):
    m, l, acc = carry
    m_new = jnp.maximum(m, jnp.max(s, axis=0, keepdims=True))
    alpha = jnp.exp2(m - m_new)
    p = jnp.exp2(s - m_new)
    l = alpha * l + jnp.sum(p, axis=0, keepdims=True)
    acc = alpha * acc + _dot(vt, p.astype(BF16))
    return m_new, l, acc


def _mla_attn_kernel(q_ref, k_ref, vt_ref, ksq_ref, o_ref, *, chunks):
    q = q_ref[...]
    tq = q.shape[0]

    def logits(chunk):
        start, size = chunk
        return _dot_nt(k_ref[start:start + size, :], q)

    def single_pass(r):
        l = jnp.zeros((1, tq), F32)
        acc = jnp.zeros((MLA_V, tq), F32)
        s = logits(chunks[0])
        for c, (start, size) in enumerate(chunks):
            s_next = logits(chunks[c + 1]) if c + 1 < len(chunks) else None
            p = jnp.exp2(s - r)
            l = l + jnp.sum(p, axis=0, keepdims=True)
            acc = acc + _dot(vt_ref[:, start:start + size], p.astype(BF16))
            s = s_next
        return l, acc

    def online():
        carry = (jnp.full((1, tq), NEG_INF, F32), jnp.zeros((1, tq), F32),
                 jnp.zeros((MLA_V, tq), F32))
        s = logits(chunks[0])
        for c, (start, size) in enumerate(chunks):
            s_next = logits(chunks[c + 1]) if c + 1 < len(chunks) else None
            carry = _softmax_update(s, vt_ref[:, start:start + size], carry)
            s = s_next
        return carry[1], carry[2]

    if ksq_ref is None:
        l, acc = online()
    else:
        qf = q.astype(F32)
        ones = jnp.ones((SUBLANES, q.shape[1]), BF16)
        q_sq = _dot_nt(ones, (qf * qf).astype(BF16))[0:1]
        k_sq = jnp.max(ksq_ref[...], axis=0, keepdims=True)[:, 0:1]
        bound = jnp.sqrt(q_sq * k_sq) * SOFTMAX_BOUND_SLACK
        l, acc = single_pass(bound - SOFTMAX_HEADROOM_LOG2)
        l, acc = lax.cond(jnp.min(l) >= SOFTMAX_MIN_ROW_SUM, lambda: (l, acc), online)
    o_ref[...] = (acc / l).T.astype(BF16)


def _mla_attn_ctx_kernel(q_ref, k_ref, vt_ref, latent_out_ref, o_ref):
    del latent_out_ref
    for h in range(MLA_HEADS):
        qk = slice(h * MLA_QK_PAD, (h + 1) * MLA_QK_PAD)
        hv = slice(h * MLA_V, (h + 1) * MLA_V)
        _mla_attn_kernel(q_ref.at[:, qk], k_ref.at[:, qk], vt_ref.at[hv, :], None,
                         o_ref.at[:, hv], chunks=((0, CTX_LEN),))


def _mla_attention(q, k, vt, ksq):
    hv = MLA_HEADS * MLA_V
    chunks = ((SEQ, CTX_LEN),) + tuple((c * MLA_TK, MLA_TK) for c in range(SEQ // MLA_TK))
    o = pl.pallas_call(
        functools.partial(_mla_attn_kernel, chunks=chunks),
        grid=(BATCH, MLA_HEADS, SEQ // MLA_TQ),
        in_specs=[
            pl.BlockSpec((None, MLA_TQ, MLA_QK_PAD), lambda b, h, i: (b, i, h)),
            pl.BlockSpec((None, TOKENS, MLA_QK_PAD), lambda b, h, i: (b, 0, h)),
            pl.BlockSpec((None, MLA_V, TOKENS), lambda b, h, i: (b, h, 0)),
            pl.BlockSpec((None, None, N_STEPS, LANES), lambda b, h, i: (b, h, 0, 0)),
        ],
        out_specs=pl.BlockSpec((None, MLA_TQ, MLA_V), lambda b, h, i: (b, i, h)),
        out_shape=jax.ShapeDtypeStruct((BATCH, TOKENS, hv), BF16),
        compiler_params=_cparams(3),
        name="mla_attn",
    )(q, k, vt, ksq)
    ctx_blk = SEQ // CTX_LEN
    hq = MLA_HEADS * MLA_QK_PAD
    return pl.pallas_call(
        _mla_attn_ctx_kernel,
        grid=(BATCH,),
        in_specs=[
            pl.BlockSpec((None, CTX_LEN, hq), lambda b: (b, ctx_blk, 0)),
            pl.BlockSpec((None, CTX_LEN, hq), lambda b: (b, ctx_blk, 0)),
            pl.BlockSpec((None, hv, CTX_LEN), lambda b: (b, 0, ctx_blk)),
            pl.BlockSpec(memory_space=pl.ANY),
        ],
        out_specs=pl.BlockSpec((None, CTX_LEN, hv), lambda b: (b, ctx_blk, 0)),
        out_shape=jax.ShapeDtypeStruct((BATCH, TOKENS, hv), BF16),
        input_output_aliases={3: 0},
        compiler_params=_cparams(1),
        name="mla_attn_ctx",
    )(q, k, vt, o)


def _group_sq_norms(x):
    w = x.shape[1]
    xf = x.astype(F32)
    group = lax.broadcasted_iota(jnp.int32, (w, LANES), 0) // NA_HEAD_DIM
    col = lax.broadcasted_iota(jnp.int32, (w, LANES), 1)
    indicator = jnp.where(group == col, 1.0, 0.0).astype(BF16)
    return _dot((xf * xf).astype(BF16), indicator)


def _norm_specs():
    return ([pl.BlockSpec((None, NA_HEADS, STEP_ROWS), lambda b, j: (b, 0, j)),
             pl.BlockSpec((None, None, SUBLANES, LANES), lambda b, j: (b, j, 0, 0))],
            [jax.ShapeDtypeStruct((BATCH, NA_HEADS, TOKENS_PAD), F32),
             jax.ShapeDtypeStruct((BATCH, N_STEPS, SUBLANES, LANES), F32)])


def _store_query_norms(rows, q, qsq_ref):
    qsq_ref[:, rows] = _group_sq_norms(q).T[:NA_HEADS, :]


def _group_finish(tile_tops, out_refs):
    ksq_ref = out_refs[-1]
    ksq_ref[...] = jnp.broadcast_to(functools.reduce(jnp.maximum, tile_tops), ksq_ref.shape)


def _na_tile(x, rows, in_refs, out_refs, *, scale):
    mod_ref, g_ref, w_ref = in_refs
    qk_ref, vt_ref, qsq_ref, _ = out_refs
    h = _pre_mix(x, mod_ref, g_ref)
    a = _dot(h, w_ref[...])
    hd = NA_HEADS * NA_HEAD_DIM
    q = a[:, :hd] * scale
    k = a[:, hd:2 * hd]
    qk_ref[rows, :hd] = q.astype(BF16)
    qk_ref[rows, hd:] = k.astype(BF16)
    vt_ref[:, rows] = a[:, 2 * hd:].T.astype(BF16)
    _store_query_norms(rows, q, qsq_ref)
    return jnp.max(_group_sq_norms(k), axis=0, keepdims=True)


def _na_projection(mod, g_pre, w):
    hd = NA_HEADS * NA_HEAD_DIM
    norm_specs, norm_shapes = _norm_specs()
    return _Projection(
        tile=functools.partial(_na_tile, scale=float(NA_HEAD_DIM ** -0.5) * LOG2E),
        finish=_group_finish,
        arrays=[mod, g_pre, w],
        in_specs=[_mod_spec(), _const_spec((1, D_MODEL)), _const_spec(w.shape)],
        out_specs=[pl.BlockSpec((None, STEP_ROWS, 2 * hd), lambda b, j: (b, j, 0)),
                   pl.BlockSpec((None, hd, STEP_ROWS), lambda b, j: (b, 0, j))] + norm_specs,
        out_shapes=[jax.ShapeDtypeStruct((BATCH, TOKENS_PAD, 2 * hd), BF16),
                    jax.ShapeDtypeStruct((BATCH, hd, TOKENS_PAD), BF16)] + norm_shapes)


def _swa_tile(x, rows, in_refs, out_refs, *, scale):
    mod_ref, g_ref, cos_ref, sin_ref, w_ref = in_refs
    q_ref, k_ref, vt_ref, qsq_ref, _ = out_refs
    h = _pre_mix(x, mod_ref, g_ref)
    a = _dot(h, w_ref[...])
    cos = cos_ref[rows, :]
    sin = sin_ref[rows, :]
    hd = SWA_HEADS * SWA_HEAD_DIM
    for p in range(HEAD_PAIRS):
        lo, hi = p * LANES, (p + 1) * LANES
        q = a[:, lo:hi] * cos + a[:, hd + lo:hd + hi] * sin
        q_ref[rows, lo:hi] = (q * scale).astype(BF16)
    kw = 2 * LANES
    for p in range(2):
        lo, hi = p * LANES, (p + 1) * LANES
        k = a[:, 2 * hd + lo:2 * hd + hi] * cos + a[:, 2 * hd + kw + lo:2 * hd + kw + hi] * sin
        k_ref[rows, lo:hi] = k.astype(BF16)
    vt_ref[:, rows] = a[:, 2 * hd + 2 * kw:].T.astype(BF16)
    _store_query_norms(rows, q_ref[rows, :], qsq_ref)
    return jnp.max(_group_sq_norms(k_ref[rows, :]), axis=0, keepdims=True)


def _swa_projection(mod, g_pre, cos, sin, w):
    hd = SWA_HEADS * SWA_HEAD_DIM
    kw = 2 * LANES
    norm_specs, norm_shapes = _norm_specs()
    return _Projection(
        tile=functools.partial(_swa_tile, scale=float(SWA_HEAD_DIM ** -0.5) * LOG2E),
        finish=_group_finish,
        arrays=[mod, g_pre, cos, sin, w],
        in_specs=[_mod_spec(), _const_spec((1, D_MODEL)), _rope_spec(), _rope_spec(),
                  _const_spec(w.shape)],
        out_specs=[pl.BlockSpec((None, STEP_ROWS, hd), lambda b, j: (b, j, 0)),
                   pl.BlockSpec((None, STEP_ROWS, kw), lambda b, j: (b, j, 0)),
                   pl.BlockSpec((None, kw, STEP_ROWS), lambda b, j: (b, 0, j))] + norm_specs,
        out_shapes=[jax.ShapeDtypeStruct((BATCH, TOKENS_PAD, hd), BF16),
                    jax.ShapeDtypeStruct((BATCH, TOKENS_PAD, kw), BF16),
                    jax.ShapeDtypeStruct((BATCH, kw, TOKENS_PAD), BF16)] + norm_shapes)


def _local_attn_kernel(*refs, n_nb, kv_slot, k_norm_lane, per_head_table, has_sink):
    q_ref = refs[0]
    k_refs = refs[1:1 + n_nb]
    kc_ref = refs[1 + n_nb]
    vt_refs = refs[2 + n_nb:2 + 2 * n_nb]
    vtc_ref = refs[2 + 2 * n_nb]
    tbl_ref, head_ref, qsq_ref, ksq_ref, o_ref = refs[3 + 2 * n_nb:]
    hdim = NA_HEAD_DIM
    tq = q_ref.shape[0]
    low = lax.broadcasted_iota(jnp.int32, (tq, LANES), 1) < hdim

    def pair_logits(p):
        qp = q_ref[:, p * LANES:(p + 1) * LANES]
        zero = jnp.zeros_like(qp)
        q2 = jnp.concatenate([jnp.where(low, qp, zero), jnp.where(low, zero, qp)], axis=0)
        ks = slice(kv_slot(p) * LANES, (kv_slot(p) + 1) * LANES)
        keys = jnp.concatenate([r[:, ks] for r in k_refs] + [kc_ref[:, ks]], axis=0)
        return _dot_nt(keys, q2)

    def attend(reference):
        s = pair_logits(0)
        n_nb_keys = n_nb * k_refs[0].shape[0]
        outs, l_min = [], None
        for p in range(HEAD_PAIRS):
            s_next = pair_logits(p + 1) if p + 1 < HEAD_PAIRS else None
            pair_out = []
            for half in range(2):
                hd = 2 * p + half
                cols = slice(half * tq, (half + 1) * tq)
                s_n = s[:n_nb_keys, cols] + tbl_ref[hd if per_head_table else 0]
                s_c = s[n_nb_keys:, cols]
                r = reference(hd, s_n, s_c)
                probs = jnp.concatenate([jnp.exp2(s_n - r).astype(BF16),
                                         jnp.exp2(s_c - r).astype(BF16)], axis=0)
                lo = kv_slot(p) * LANES + half * hdim
                vt = jnp.concatenate([v[lo:lo + hdim, :] for v in vt_refs]
                                     + [vtc_ref[lo:lo + hdim, :]], axis=1)
                vt = jnp.concatenate([vt, _ones_rows(vt.shape[1])], axis=0)
                acc = _dot(vt, probs)
                l = acc[hdim:hdim + 1]
                if has_sink:
                    l = l + jnp.exp2(head_ref[hd:hd + 1, :] - r)
                l_min = l if l_min is None else jnp.minimum(l_min, l)
                pair_out.append(acc[:hdim] / l)
            outs.append(jnp.concatenate(pair_out, axis=0).T.astype(BF16))
            s = s_next
        return tuple(outs), l_min

    def row_max(hd, s_n, s_c):
        m = jnp.maximum(jnp.max(s_n, axis=0, keepdims=True), jnp.max(s_c, axis=0, keepdims=True))
        return jnp.maximum(m, head_ref[hd:hd + 1, :]) if has_sink else m

    k_sq = jnp.max(ksq_ref[...], axis=0)[0:1, :]

    def bound(hd, s_n, s_c):
        lane = k_norm_lane(hd)
        qk = jnp.sqrt(qsq_ref[hd:hd + 1, :] * k_sq[:, lane:lane + 1]) * SOFTMAX_BOUND_SLACK
        extra = head_ref[hd:hd + 1, :]
        top = jnp.maximum(qk, extra) if has_sink else qk + extra
        return top - SOFTMAX_HEADROOM_LOG2

    outs, l_min = attend(bound)
    outs = lax.cond(jnp.min(l_min) >= SOFTMAX_MIN_ROW_SUM, lambda: outs,
                    lambda: attend(row_max)[0])
    for p, pair in enumerate(outs):
        o_ref[:, p * LANES:(p + 1) * LANES] = pair


def _local_attention(q, q_col, k, k_col, k_width, vt, qsq, ksq, q_rows, nb_tokens, n_nb, table,
                     head_rows, has_sink, kv_slot, k_norm_lane, name):
    hd = NA_HEADS * NA_HEAD_DIM
    per_head_table = table.shape[1] > 1
    n_steps = -(-TOKENS // q_rows)
    lat_hi = SEQ // q_rows - 1
    per_tile = q_rows // nb_tokens
    nb_hi = SEQ // nb_tokens - 1

    def nb_index(j, n):
        return jnp.clip(j * per_tile - 1 + n, 0, nb_hi)

    def k_spec(n):
        return pl.BlockSpec((None, nb_tokens, k_width), lambda b, j: (b, nb_index(j, n), k_col))

    def vt_spec(n):
        return pl.BlockSpec((None, k_width, nb_tokens), lambda b, j: (b, 0, nb_index(j, n)))

    def variant(j):
        return jnp.where(j == 0, 0, jnp.where(j == lat_hi, 2, jnp.where(j > lat_hi, 3, 1)))

    tbl_spec = pl.BlockSpec((None,) + table.shape[1:], lambda b, j: (variant(j), 0, 0, 0))
    in_specs = ([pl.BlockSpec((None, q_rows, hd), lambda b, j: (b, j, q_col))]
                + [k_spec(n) for n in range(n_nb)]
                + [pl.BlockSpec((None, CTX_LEN, k_width), lambda b, j: (b, N_LAT_TILES, k_col))]
                + [vt_spec(n) for n in range(n_nb)]
                + [pl.BlockSpec((None, k_width, CTX_LEN), lambda b, j: (b, 0, N_LAT_TILES))]
                + [tbl_spec, _const_spec((NA_HEADS, q_rows)),
                   pl.BlockSpec((None, NA_HEADS, q_rows), lambda b, j: (b, 0, j)),
                   pl.BlockSpec((None, N_STEPS, SUBLANES, LANES), lambda b, j: (b, 0, 0, 0))])
    return pl.pallas_call(
        functools.partial(_local_attn_kernel, n_nb=n_nb, kv_slot=kv_slot,
                          k_norm_lane=k_norm_lane, per_head_table=per_head_table,
                          has_sink=has_sink),
        grid=(BATCH, n_steps),
        in_specs=in_specs,
        out_specs=pl.BlockSpec((None, q_rows, hd), lambda b, j: (b, j, 0)),
        out_shape=jax.ShapeDtypeStruct((BATCH, n_steps * q_rows, hd), BF16),
        compiler_params=_cparams(2),
        name=name,
    )(q, *([k] * (n_nb + 1)), *([vt] * (n_nb + 1)), table, head_rows, qsq, ksq)


POST_FFN_OPERANDS = 9


def _post_ffn_kernel(*refs, n_stream, proj):
    x = _stream_rows(refs[:n_stream])
    (oa_ref, ob_ref, mod_ref, gpm_ref, gpf_ref, gof_ref, wo_ref, win_ref,
     wout_ref) = refs[n_stream:n_stream + POST_FFN_OPERANDS]
    rest = refs[n_stream + POST_FFN_OPERANDS:]
    n_proj_in = len(proj.arrays) if proj else 0
    proj_in, y_ref, proj_out, act_ref = (rest[:n_proj_in], rest[n_proj_in],
                                         rest[n_proj_in + 1:-1], rest[-1])
    gate_m = mod_ref[2:3, :]
    shift_f = mod_ref[3:4, :]
    scale_f = mod_ref[4:5, :]
    gate_f = mod_ref[5:6, :]
    rows = _tile_rows()
    mixed = [_dot(o_ref[...], wo_ref[...]) for o_ref in (oa_ref, ob_ref)]
    x1 = [x[r, :] + gate_m * _rms(mx, gpm_ref[...]) for r, mx in zip(rows, mixed)]
    h = [(_rms(xi, gpf_ref[...]) * (1.0 + scale_f) + shift_f).astype(BF16) for xi in x1]
    chunk = FFN_CHUNK
    y = []
    for i, r in enumerate(rows):
        for f in range(D_FF // chunk):
            lo, hi = f * chunk, (f + 1) * chunk
            gate = _dot(h[i], win_ref[:, lo:hi])
            up = _dot(h[i], win_ref[:, D_FF + lo:D_FF + hi])
            act_ref[i, :, lo:hi] = (gate * jax.nn.sigmoid(gate) * up).astype(BF16)
        ffn = _dot(act_ref[i], wout_ref[...])
        y.append(x1[i] + gate_f * _rms(ffn, gof_ref[...]))
        y_ref[r, :] = y[i]
    if proj:
        proj.finish([proj.tile(yi, r, proj_in, proj_out) for yi, r in zip(y, rows)], proj_out)


def _post_ffn(stream, o, mod, g_post_mix, g_pre_ffn, g_post_ffn, wo, win, wout, proj):
    vec = _const_spec((1, D_MODEL))
    n_steps = N_STEPS if proj else SEQ // STEP_ROWS
    stream_arrays, stream_specs = _stream_operands(stream)

    def o_spec(i):
        return pl.BlockSpec(
            (None, TILE, D_MODEL),
            lambda b, j: (b, jnp.minimum(POST_TILES_PER_STEP * j + i, N_LAT_TILES), 0))

    out = pl.pallas_call(
        functools.partial(_post_ffn_kernel, n_stream=len(stream_arrays), proj=proj),
        grid=(BATCH, n_steps),
        in_specs=stream_specs + [
            o_spec(0), o_spec(1), _mod_spec(), vec, vec, vec,
            _const_spec(wo.shape), _const_spec(win.shape), _const_spec(wout.shape)
        ] + (proj.in_specs if proj else []),
        out_specs=[_x_spec()] + (proj.out_specs if proj else []),
        out_shape=[jax.ShapeDtypeStruct((BATCH, n_steps * STEP_ROWS, D_MODEL), F32)]
        + (proj.out_shapes if proj else []),
        scratch_shapes=[pltpu.VMEM((POST_TILES_PER_STEP, TILE, D_FF), BF16)],
        compiler_params=_cparams(2),
        name="post_ffn",
    )(*stream_arrays, o, o, mod, g_post_mix, g_pre_ffn, g_post_ffn, wo, win, wout,
      *(proj.arrays if proj else []))
    return out if proj else out[0]


def _rope_tables():
    t = np.arange(SEQ)
    row = (t // GRID_W).astype(np.float32)
    col = (t % GRID_W).astype(np.float32)
    n_freq = MLA_ROPE // 4
    inv = np.float32(ROPE_THETA) ** (-(np.arange(n_freq, dtype=np.float32) / np.float32(n_freq)))
    ang = np.concatenate([row[:, None] * inv, col[:, None] * inv], axis=-1)
    cos = np.cos(ang.astype(np.float64)).astype(np.float32)
    sin = np.sin(ang.astype(np.float64)).astype(np.float32)
    cos_t = np.concatenate([cos, cos, cos, cos], axis=-1)
    sin_t = np.concatenate([-sin, sin, -sin, sin], axis=-1)
    no_pos = TOKENS_PAD - SEQ
    cos_t = np.concatenate([cos_t, np.ones((no_pos, LANES), np.float32)], axis=0)
    sin_t = np.concatenate([sin_t, np.zeros((no_pos, LANES), np.float32)], axis=0)
    return jnp.asarray(cos_t), jnp.asarray(sin_t)


NA_ROWS_PER_TILE = TILE // GRID_W
NA_BIAS_ROWS = 2 * NA_WIN_ROWS - 1
NA_BIAS_COLS = 2 * NA_WIN_COLS - 1
NA_NB_KEYS = NA_NB_BLOCKS * NA_NB_TOKENS


def _na_table_kernel(u_ref, o_ref):
    rows = SEQ // GRID_W
    kc = lax.broadcasted_iota(jnp.int32, (GRID_W, LANES), 0)
    lane = lax.broadcasted_iota(jnp.int32, (GRID_W, LANES), 1)
    qc = lane % GRID_W
    col_start = jnp.clip(qc - NA_WIN_COLS // 2, 0, GRID_W - NA_WIN_COLS)
    col_ok = (kc >= col_start) & (kc < col_start + NA_WIN_COLS)
    first_half = lane < GRID_W
    neg = jnp.full((GRID_W, LANES), NEG_INF, F32)
    toeplitz = {}
    for i in range(1, NA_BIAS_ROWS):
        row = jnp.broadcast_to(u_ref[i:i + 1, :], (GRID_W, LANES))
        toeplitz[i] = pltpu.roll(row, LANES - (NA_WIN_COLS - 1), 1, stride=1,
                                 stride_axis=0) * LOG2E
    for v, tile in enumerate((0, 1, N_LAT_TILES - 1)):
        for kl in range(NA_NB_BLOCKS * NA_ROWS_PER_TILE):
            kr = NA_ROWS_PER_TILE * (tile - 1) + kl
            for qp in range(NA_ROWS_PER_TILE // 2):
                oks = []
                for half in range(2):
                    qr = NA_ROWS_PER_TILE * tile + 2 * qp + half
                    row_start = min(max(qr - NA_WIN_ROWS // 2, 0), rows - NA_WIN_ROWS)
                    oks.append(row_start <= kr < row_start + NA_WIN_ROWS)
                if oks[0] and oks[1]:
                    ok = col_ok
                elif oks[0]:
                    ok = col_ok & first_half
                elif oks[1]:
                    ok = col_ok & jnp.logical_not(first_half)
                else:
                    ok = None
                i = kl - NA_ROWS_PER_TILE - 2 * qp + NA_WIN_ROWS - 1
                blk = neg if ok is None else jnp.where(ok, toeplitz[i], neg)
                o_ref[v, kl * GRID_W:(kl + 1) * GRID_W, qp * LANES:(qp + 1) * LANES] = blk
    o_ref[N_TABLE_VARIANTS - 1] = jnp.full((NA_NB_KEYS, TILE), NEG_INF, F32)


def _na_table(rpb):
    pad = jnp.full((NA_HEADS, NA_BIAS_ROWS, GRID_W - NA_BIAS_COLS), NEG_INF, F32)
    rev = jnp.concatenate([rpb.astype(F32)[:, :, ::-1], pad], axis=-1)
    off = jnp.full((NA_HEADS, 1, GRID_W), NEG_INF, F32)
    left = jnp.concatenate([rev, off], axis=1)
    right = jnp.concatenate([off, rev], axis=1)
    u = jnp.concatenate([left, right], axis=-1)
    return pl.pallas_call(
        _na_table_kernel,
        grid=(NA_HEADS,),
        in_specs=[pl.BlockSpec((None, NA_BIAS_ROWS + 1, LANES), lambda h: (h, 0, 0))],
        out_specs=pl.BlockSpec((N_TABLE_VARIANTS, None, NA_NB_KEYS, TILE),
                               lambda h: (0, h, 0, 0)),
        out_shape=jax.ShapeDtypeStruct((N_TABLE_VARIANTS, NA_HEADS, NA_NB_KEYS, TILE), F32),
        compiler_params=_cparams(1),
        name="na_table",
    )(u)


def _swa_table():
    j = np.array([0, 1, SEQ // SWA_Q_ROWS - 1])[:, None, None]
    qpos = SWA_Q_ROWS * j + np.arange(SWA_Q_ROWS)[None, None, :]
    kpos = (SWA_Q_ROWS * j - SWA_NB_TOKENS
            + np.arange(SWA_NB_BLOCKS * SWA_NB_TOKENS)[None, :, None])
    ok = (np.abs(kpos - qpos) <= SWA_WINDOW) & (kpos >= 0) & (kpos < SEQ)
    tbl = np.where(ok, 0.0, NEG_INF).astype(np.float32)
    tbl = np.concatenate([tbl, np.full((1,) + tbl.shape[1:], NEG_INF, np.float32)], axis=0)
    return jnp.asarray(tbl[:, None])


def _swap_halves(w):
    half = w.shape[-1] // 2
    return jnp.concatenate([w[..., half:], w[..., :half]], axis=-1)


def _mla_weights(w_in, w_uq, w_ukv):
    lat = MLA_Q_LORA + MLA_KV_LORA
    kpe = w_in[:, lat:]
    z = jnp.zeros((D_MODEL, LANES - MLA_ROPE), w_in.dtype)
    w_in_x = jnp.concatenate([w_in[:, :lat], kpe, z, _swap_halves(kpe), z], axis=1)
    uq = w_uq.reshape(MLA_Q_LORA, MLA_HEADS, MLA_NOPE + MLA_ROPE)
    rope = uq[:, :, MLA_NOPE:]
    zq = jnp.zeros((MLA_Q_LORA, MLA_HEADS, LANES - MLA_ROPE), w_uq.dtype)
    flat = lambda a: a.reshape(a.shape[0], -1)
    wq_x = jnp.concatenate([flat(uq[:, :, :MLA_NOPE]),
                            flat(jnp.concatenate([rope, zq], axis=-1)),
                            flat(jnp.concatenate([_swap_halves(rope), zq], axis=-1))], axis=1)
    ukv = w_ukv.reshape(MLA_KV_LORA, MLA_HEADS, MLA_NOPE + MLA_V)
    wkv_x = jnp.concatenate([flat(ukv[:, :, :MLA_NOPE]), flat(ukv[:, :, MLA_NOPE:])], axis=1)
    return w_in_x.astype(BF16), wq_x.astype(BF16), wkv_x.astype(BF16)


def _swa_weights(w_qkv):
    hd = SWA_HEADS * SWA_HEAD_DIM
    kd = SWA_KV_HEADS * SWA_HEAD_DIM
    wq = w_qkv[:, :hd].reshape(D_MODEL, SWA_HEADS, SWA_HEAD_DIM)
    wk = w_qkv[:, hd:hd + kd].reshape(D_MODEL, SWA_KV_HEADS, 1, SWA_HEAD_DIM)
    wv = w_qkv[:, hd + kd:].reshape(D_MODEL, SWA_KV_HEADS, 1, SWA_HEAD_DIM)
    dup = lambda a: jnp.broadcast_to(a, (D_MODEL, SWA_KV_HEADS, 2, SWA_HEAD_DIM)).reshape(
        D_MODEL, -1)
    flat = lambda a: a.reshape(D_MODEL, -1)
    w = jnp.concatenate([flat(wq), flat(_swap_halves(wq)), dup(wk), dup(_swap_halves(wk)),
                         dup(wv)], axis=1)
    return w.astype(BF16)


def kernel(x, c, ctx, c_ctx, w_mod, b_mod, g_pre_mix, g_post_mix, g_pre_ffn, g_post_ffn,
           w_ffn_in, w_ffn_out, mla_w_in, mla_g_q, mla_w_uq, mla_g_kv, mla_w_ukv, mla_w_o,
           na_w_qkv, na_rpb, na_w_o, swa_w_qkv, swa_sink, swa_w_o):
    assert x.shape == (BATCH, SEQ, D_MODEL) and ctx.shape == (BATCH, CTX_LEN, D_MODEL)
    stream = (x, ctx)
    cc = jnp.concatenate(
        [c, c_ctx[None], jnp.zeros((SUBLANES - BATCH - 1, D_MODEL), F32)], axis=0)
    mods = _modulation(cc, w_mod, b_mod)
    mods = mods.reshape(DEPTH, SUBLANES, 6, D_MODEL)
    mods = jnp.stack([mods[:, :BATCH],
                      jnp.broadcast_to(mods[:, BATCH:BATCH + 1], (DEPTH, BATCH, 6, D_MODEL))],
                     axis=2)
    cos_t, sin_t = _rope_tables()
    row = lambda g: g.reshape(1, -1)

    def projection(i):
        kind, j = i % 3, i // 3
        mod, g_pre = mods[i], row(g_pre_mix[i])
        if kind == 0:
            w_in_x, wq_x, wkv_x = _mla_weights(mla_w_in[j], mla_w_uq[j], mla_w_ukv[j])
            return _mla_projection(mod, g_pre, cos_t, sin_t, w_in_x, row(mla_g_q[j]),
                                   row(mla_g_kv[j]), wq_x, wkv_x)
        if kind == 1:
            return _na_projection(mod, g_pre, na_w_qkv[j].astype(BF16))
        return _swa_projection(mod, g_pre, cos_t, sin_t, _swa_weights(swa_w_qkv[j]))

    projected = _project(stream, projection(0))
    for i in range(DEPTH):
        last = i == DEPTH - 1
        kind, j = i % 3, i // 3
        if kind == 0:
            q, k, vt, ksq = projected
            o = _mla_attention(q, k, vt, jnp.swapaxes(ksq, 1, 2))
            w_o = mla_w_o[j]
        elif kind == 1:
            qk, vt, qsq, ksq = projected
            bias_top = jnp.maximum(jnp.max(na_rpb[j].astype(F32), axis=(1, 2)), 0.0) * LOG2E
            o = _local_attention(qk, 0, qk, 1, NA_HEADS * NA_HEAD_DIM, vt, qsq, ksq, NA_Q_ROWS,
                                 NA_NB_TOKENS, NA_NB_BLOCKS, _na_table(na_rpb[j]),
                                 jnp.broadcast_to(bias_top[:, None], (NA_HEADS, NA_Q_ROWS)),
                                 False, lambda p: p, lambda hd: hd, "na_attn")
            w_o = na_w_o[j]
        else:
            q, k, vt, qsq, ksq = projected
            sink = jnp.broadcast_to((swa_sink[j].astype(F32) * LOG2E)[:, None],
                                    (SWA_HEADS, SWA_Q_ROWS))
            heads_per_kv = SWA_HEADS // SWA_KV_HEADS
            o = _local_attention(q, 0, k, 0, 2 * LANES, vt, qsq, ksq, SWA_Q_ROWS,
                                 SWA_NB_TOKENS, SWA_NB_BLOCKS, _swa_table(), sink, True,
                                 lambda p: 2 * p // heads_per_kv,
                                 lambda hd: 2 * (hd // heads_per_kv), "swa_attn")
            w_o = swa_w_o[j]
        out = _post_ffn(stream, o, mods[i], row(g_post_mix[i]), row(g_pre_ffn[i]),
                        row(g_post_ffn[i]), w_o.astype(BF16), w_ffn_in[i].astype(BF16),
                        w_ffn_out[i].astype(BF16), None if last else projection(i + 1))
        if last:
            return out
        stream, projected = out[0], out[1:]
```
